```python
import math
import jax, jax.numpy as jnp
from jax import lax
import numpy as np

D_MODEL = 1024
BATCH = 8
SEQ = 2048
DEPTH = 4
DEC_BATCH = 128
DEC_SEQ = 1
PAST_LEN = 16384
PAGE_SIZE = 128

N_MIXERS = 2
N_A = (DEPTH + 1) // 2
N_B = DEPTH // 2
A_HEADS = 8
A_DK = 128
A_DV = 128
A_QK_DIM = A_HEADS * A_DK
A_V_DIM = A_HEADS * A_DV
A_CONV_DIM = 2 * A_QK_DIM + A_V_DIM
A_IN_DIM = A_CONV_DIM + A_V_DIM + 2 * A_HEADS
CONV_W = 4
CHUNK_A = 64
B_WIDTH = 2 * D_MODEL
B_GROUPS = 8
B_GROUP_DIM = B_WIDTH // B_GROUPS
CHUNK_B = 128
D_FF = 2816
FFN_RES = 0.5
N_SUB = 3
EPS = 1e-6

kernel_name = "hybrid_gdn_gmlp_macaron_adaln_step"


def rms_norm(x, w):
    xf = x.astype(jnp.float32)
    y = xf * lax.rsqrt(jnp.mean(xf * xf, axis=-1, keepdims=True) + EPS)
    return (y * w.astype(jnp.float32)).astype(x.dtype)


def layer_norm(x, w, b):
    xf = x.astype(jnp.float32)
    mu = jnp.mean(xf, axis=-1, keepdims=True)
    xc = xf - mu
    y = xc * lax.rsqrt(jnp.mean(xc * xc, axis=-1, keepdims=True) + EPS)
    return (y * w.astype(jnp.float32) + b.astype(jnp.float32)).astype(x.dtype)


def l2_norm(x):
    xf = x.astype(jnp.float32)
    return xf * lax.rsqrt(jnp.sum(xf * xf, axis=-1, keepdims=True) + EPS)


def swiglu(h, w_gu, w_down):
    gu = h @ w_gu
    return (jax.nn.silu(gu[..., :D_FF]) * gu[..., D_FF:]) @ w_down


def gated_delta_rule(q, k, v, g, beta, S0):
    f32 = jnp.float32
    Bn, L, H, _ = q.shape
    C = min(CHUNK_A, L)
    n = -(-L // C)
    pad = n * C - L

    def blocks(t):
        t = jnp.pad(t.astype(f32), [(0, 0), (0, pad)] + [(0, 0)] * (t.ndim - 2))
        t = t.reshape((Bn, n, C) + t.shape[2:])
        return jnp.moveaxis(t, 3, 1)

    q = blocks(q) * (A_DK ** -0.5)
    k, v, g, beta = blocks(k), blocks(v), blocks(g), blocks(beta)
    decay = jnp.cumsum(g, axis=-1)
    idx = jnp.arange(C)
    causal = idx[:, None] >= idx[None, :]
    strict = idx[:, None] > idx[None, :]
    gamma = jnp.exp(jnp.where(causal, decay[..., :, None] - decay[..., None, :], -jnp.inf))
    k_beta = k * beta[..., None]
    m = jnp.where(strict, jnp.einsum('bhnck,bhnsk->bhncs', k_beta, k) * gamma, 0.0)
    eye = jnp.eye(C, dtype=f32)
    rhs = jnp.concatenate([v * beta[..., None], k_beta * jnp.exp(decay)[..., None]], axis=-1)
    sol = lax.linalg.triangular_solve(eye + m, rhs, left_side=True, lower=True, unit_diagonal=True)
    u_val, w_key = sol[..., :A_DV], sol[..., A_DV:]
    attn = jnp.where(causal, jnp.einsum('bhnck,bhnsk->bhncs', q, k) * gamma, 0.0)

    def step(S, xs):
        q_c, k_c, u_c, w_c, attn_c, d_c = xs
        v_new = u_c - jnp.einsum('bhck,bhkv->bhcv', w_c, S)
        o_c = (jnp.einsum('bhck,bhkv->bhcv', q_c * jnp.exp(d_c)[..., None], S)
               + jnp.einsum('bhcs,bhsv->bhcv', attn_c, v_new))
        d_last = d_c[..., -1]
        S = (S * jnp.exp(d_last)[..., None, None]
             + jnp.einsum('bhck,bhcv->bhkv', k_c * jnp.exp(d_last[..., None] - d_c)[..., None], v_new))
        return S, o_c

    xs = tuple(jnp.moveaxis(t, 2, 0) for t in (q, k, u_val, w_key, attn, decay))
    S, o = lax.scan(step, S0.astype(f32), xs)
    o = jnp.moveaxis(jnp.moveaxis(o, 0, 2), 1, 3)
    o = o.reshape(Bn, n * C, H, A_DV)[:, :L]
    return o, S


def mixer_a(h, conv_buf, S0, w_in, w_conv, a_log, dt_bias, w_onorm, w_out):
    Bn, L, _ = h.shape
    proj = h @ w_in
    qkv_pre = proj[..., :A_CONV_DIM]
    z = proj[..., A_CONV_DIM:A_CONV_DIM + A_V_DIM]
    b_logit = proj[..., A_CONV_DIM + A_V_DIM:A_CONV_DIM + A_V_DIM + A_HEADS]
    a_logit = proj[..., A_CONV_DIM + A_V_DIM + A_HEADS:]
    full = jnp.concatenate([conv_buf.astype(qkv_pre.dtype), qkv_pre], axis=1)
    conv = full[:, 0:L] * w_conv[0]
    for j in range(1, CONV_W):
        conv = conv + full[:, j:j + L] * w_conv[j]
    qkv = jax.nn.silu(conv)
    new_buf = full[:, L:]
    q = l2_norm(qkv[..., :A_QK_DIM].reshape(Bn, L, A_HEADS, A_DK))
    k = l2_norm(qkv[..., A_QK_DIM:2 * A_QK_DIM].reshape(Bn, L, A_HEADS, A_DK))
    v = qkv[..., 2 * A_QK_DIM:].reshape(Bn, L, A_HEADS, A_DV)
    beta = jax.nn.sigmoid(b_logit.astype(jnp.float32))
    g = -jnp.exp(a_log.astype(jnp.float32)) * jax.nn.softplus(a_logit.astype(jnp.float32) + dt_bias.astype(jnp.float32))
    o, S = gated_delta_rule(q, k, v, g, beta, S0)
    o = rms_norm(o.astype(h.dtype), w_onorm) * jax.nn.silu(z.reshape(Bn, L, A_HEADS, A_DV))
    y = o.reshape(Bn, L, A_V_DIM) @ w_out
    return y, new_buf, S.astype(S0.dtype)


def mixer_b(h, w_in, vnorm_w, vnorm_b, w_s, b_s, w_out):
    Bn, L, _ = h.shape
    zz = jax.nn.gelu(h @ w_in)
    u, v = zz[..., :B_WIDTH], zz[..., B_WIDTH:]
    v = layer_norm(v, vnorm_w, vnorm_b)
    n = -(-L // CHUNK_B)
    pad = n * CHUNK_B - L
    vb = jnp.pad(v, ((0, 0), (0, pad), (0, 0))).reshape(Bn, n, CHUNK_B, B_GROUPS, B_GROUP_DIM)
    causal = jnp.tril(jnp.ones((CHUNK_B, CHUNK_B), dtype=bool))
    ws = jnp.where(causal, w_s, 0.0)
    mixed = jnp.einsum('gts,bnsgd->bntgd', ws, vb) + b_s.T[:, :, None]
    mixed = mixed.reshape(Bn, n * CHUNK_B, B_WIDTH)[:, :L]
    y = (u * mixed) @ w_out
    return y, v


def trunk(x, c, conv_bufs, states, w_ada, b_ada, norm_w, ffn_w_gu, ffn_w_down,
          a_w_in, a_w_conv, a_log, a_dt_bias, a_w_onorm, a_w_out,
          b_w_in, b_vnorm_w, b_vnorm_b, b_w_s, b_b_s, b_w_out, final_norm_w):
    Bn = x.shape[0]
    new_conv, new_S, v_rows = [], [], []
    for i in range(DEPTH):
        mod = (jax.nn.silu(c) @ w_ada[i] + b_ada[i]).reshape(Bn, N_SUB, 3, D_MODEL)
        shift, scale, gate = mod[:, :, 0], mod[:, :, 1], mod[:, :, 2]
        h = rms_norm(x, norm_w[i, 0]) * (1.0 + scale[:, 0, None]) + shift[:, 0, None]
        x = x + FFN_RES * (1.0 + gate[:, 0, None]) * swiglu(h, ffn_w_gu[i, 0], ffn_w_down[i, 0])
        h = rms_norm(x, norm_w[i, 1]) * (1.0 + scale[:, 1, None]) + shift[:, 1, None]
        li = i // N_MIXERS
        if i % N_MIXERS == 0:
            y, buf, S = mixer_a(h, conv_bufs[li], states[li], a_w_in[li], a_w_conv[li], a_log[li],
                                a_dt_bias[li], a_w_onorm[li], a_w_out[li])
            new_conv.append(buf)
            new_S.append(S)
        else:
            y, v = mixer_b(h, b_w_in[li], b_vnorm_w[li], b_vnorm_b[li], b_w_s[li], b_b_s[li], b_w_out[li])
            v_rows.append(v)
        x = x + (1.0 + gate[:, 1, None]) * y
        h = rms_norm(x, norm_w[i, 2]) * (1.0 + scale[:, 2, None]) + shift[:, 2, None]
        x = x + FFN_RES * (1.0 + gate[:, 2, None]) * swiglu(h, ffn_w_gu[i, 1], ffn_w_down[i, 1])
    y = rms_norm(x, final_norm_w)
    return y, jnp.stack(new_conv), jnp.stack(new_S), jnp.stack(v_rows)


def setup_inputs(seed: int = 0) -> dict:
    key = jax.random.key(seed)
    ks = jax.random.split(key, 26)
    f32 = jnp.float32

    def nrm(k, shape, s):
        return jax.random.normal(k, shape, f32) * s

    dt = jnp.exp(jax.random.uniform(ks[14], (N_A, A_HEADS), f32, math.log(1e-3), math.log(1e-1)))
    return {
        "x_prompt": nrm(ks[0], (BATCH, SEQ, D_MODEL), 1.0),
        "x_sample": nrm(ks[1], (DEC_BATCH, DEC_SEQ, D_MODEL), 1.0),
        "state_a_conv": nrm(ks[2], (N_A, DEC_BATCH, CONV_W - 1, A_CONV_DIM), 1.0),
        "state_a_S": nrm(ks[3], (N_A, DEC_BATCH, A_HEADS, A_DK, A_DV), 0.5),
        "c_prompt": nrm(ks[4], (BATCH, D_MODEL), 1.0),
        "c_sample": nrm(ks[5], (DEC_BATCH, D_MODEL), 1.0),
        "w_ada": nrm(ks[6], (DEPTH, D_MODEL, N_SUB * 3 * D_MODEL), 0.1 * D_MODEL ** -0.5),
        "b_ada": nrm(ks[7], (DEPTH, N_SUB * 3 * D_MODEL), 0.01),
        "norm_w": 1.0 + nrm(ks[8], (DEPTH, N_SUB, D_MODEL), 0.02),
        "ffn_w_gu": nrm(ks[9], (DEPTH, 2, D_MODEL, 2 * D_FF), D_MODEL ** -0.5),
        "ffn_w_down": nrm(ks[10], (DEPTH, 2, D_FF, D_MODEL), D_FF ** -0.5),
        "a_w_in": nrm(ks[11], (N_A, D_MODEL, A_IN_DIM), D_MODEL ** -0.5),
        "a_w_conv": nrm(ks[12], (N_A, CONV_W, A_CONV_DIM), CONV_W ** -0.5),
        "a_log": jnp.log(jax.random.uniform(ks[13], (N_A, A_HEADS), f32, 1.0, 16.0)),
        "a_dt_bias": dt + jnp.log(-jnp.expm1(-dt)),
        "a_w_onorm": 1.0 + nrm(ks[15], (N_A, A_DV), 0.02),
        "a_w_out": nrm(ks[16], (N_A, A_V_DIM, D_MODEL), A_V_DIM ** -0.5),
        "b_w_in": nrm(ks[17], (N_B, D_MODEL, 2 * B_WIDTH), D_MODEL ** -0.5),
        "b_vnorm_w": 1.0 + nrm(ks[18], (N_B, B_WIDTH), 0.02),
        "b_vnorm_b": nrm(ks[19], (N_B, B_WIDTH), 0.01),
        "b_w_s": nrm(ks[20], (N_B, B_GROUPS, CHUNK_B, CHUNK_B), CHUNK_B ** -0.5),
        "b_b_s": 1.0 + nrm(ks[21], (N_B, B_GROUPS, CHUNK_B), 0.02),
        "b_w_out": nrm(ks[22], (N_B, B_WIDTH, D_MODEL), B_WIDTH ** -0.5),
        "final_norm_w": 1.0 + nrm(ks[23], (D_MODEL,), 0.02),
    }


def reference(x_prompt, x_sample, state_a_conv, state_a_S, c_prompt, c_sample,
              w_ada, b_ada, norm_w, ffn_w_gu, ffn_w_down,
              a_w_in, a_w_conv, a_log, a_dt_bias, a_w_onorm, a_w_out,
              b_w_in, b_vnorm_w, b_vnorm_b, b_w_s, b_b_s, b_w_out, final_norm_w):
    zero_conv = jnp.zeros((N_A, x_prompt.shape[0], CONV_W - 1, A_CONV_DIM), x_prompt.dtype)
    zero_S = jnp.zeros((N_A, x_prompt.shape[0], A_HEADS, A_DK, A_DV), state_a_S.dtype)
    y_prompt, conv_prompt, S_prompt, _ = trunk(
        x_prompt, c_prompt, zero_conv, zero_S, w_ada, b_ada, norm_w, ffn_w_gu, ffn_w_down,
        a_w_in, a_w_conv, a_log, a_dt_bias, a_w_onorm, a_w_out,
        b_w_in, b_vnorm_w, b_vnorm_b, b_w_s, b_b_s, b_w_out, final_norm_w)
    y_sample, conv_sample, S_sample, v_sample = trunk(
        x_sample, c_sample, state_a_conv, state_a_S, w_ada, b_ada, norm_w, ffn_w_gu, ffn_w_down,
        a_w_in, a_w_conv, a_log, a_dt_bias, a_w_onorm, a_w_out,
        b_w_in, b_vnorm_w, b_vnorm_b, b_w_s, b_b_s, b_w_out, final_norm_w)
    return (y_prompt, y_sample, conv_prompt, S_prompt, conv_sample, S_sample, v_sample)
```

```python
import functools

import jax
import jax.numpy as jnp
from jax import lax
from jax.experimental import pallas as pl
from jax.experimental.pallas import tpu as pltpu

f32 = jnp.float32
bf16 = jnp.bfloat16

D_MODEL = 1024
DEPTH = 4
N_MIXERS = 2
A_HEADS = 8
A_DK = 128
A_DV = 128
A_QK_DIM = A_HEADS * A_DK
A_V_DIM = A_HEADS * A_DV
A_CONV_DIM = 2 * A_QK_DIM + A_V_DIM
A_MAIN_DIM = A_CONV_DIM + A_V_DIM
CONV_W = 4
CHUNK_A = 64
B_WIDTH = 2 * D_MODEL
B_GROUPS = 8
B_GROUP_DIM = B_WIDTH // B_GROUPS
CHUNK_B = 128
D_FF = 2816
FFN_RES = 0.5
N_SUB = 3
EPS = 1e-6

LANES = 128
FF_TILE = 256
GATE_COLS = 3 * A_HEADS
VMEM_LIMIT = 52 * 1024 * 1024


def _params(sem):
    return pltpu.CompilerParams(dimension_semantics=sem, vmem_limit_bytes=VMEM_LIMIT)


def _resident(shape, index_map):
    return pl.BlockSpec(shape, index_map, pipeline_mode=pl.Buffered(1))


def _bdot(a, b):
    return jnp.dot(a.astype(bf16), b.astype(bf16), preferred_element_type=f32)


def _bdot_nt(a, b):
    return lax.dot_general(a.astype(bf16), b.astype(bf16), (((1,), (1,)), ((), ())),
                           preferred_element_type=f32)


def _bdot_tn(a, b):
    return lax.dot_general(a.astype(bf16), b.astype(bf16), (((0,), (0,)), ((), ())),
                           preferred_element_type=f32)


def _split2(a):
    hi = a.astype(bf16)
    lo = (a - hi.astype(f32)).astype(bf16)
    return hi, lo


def _split3(a):
    hi = a.astype(bf16)
    r = a - hi.astype(f32)
    mid = r.astype(bf16)
    lo = (r - mid.astype(f32)).astype(bf16)
    return hi, mid, lo


def _dot3(a, b):
    ah, al = _split2(a)
    bh, bl = _split2(b)
    d = functools.partial(jnp.dot, preferred_element_type=f32)
    return d(ah, bh) + d(al, bh) + d(ah, bl)


def _silu(x):
    return x * jax.nn.sigmoid(x)


def _softplus(x):
    return jnp.maximum(x, 0.0) + jnp.log1p(jnp.exp(-jnp.abs(x)))


def _rms(x):
    return x * lax.rsqrt(jnp.mean(x * x, axis=-1, keepdims=True) + EPS)


def _norm_mod(x, nw, sc, sh):
    return (_rms(x) * nw) * (1.0 + sc) + sh


def _l2n(x):
    return x * lax.rsqrt(jnp.sum(x * x, axis=-1, keepdims=True) + EPS)


def _ada_kernel(cp_ref, cs_ref, w_ref, b_ref, op_ref, os_ref):
    w = w_ref[...].astype(bf16)
    b = b_ref[...]
    op_ref[...] = _bdot(_silu(cp_ref[...]), w) + b
    os_ref[...] = _bdot(_silu(cs_ref[...]), w) + b


def _ada(c_p, c_s, w_ada, b_ada):
    n = N_SUB * 3 * D_MODEL
    tn = 9 * LANES
    bp, bs = c_p.shape[0], c_s.shape[0]
    return pl.pallas_call(
        _ada_kernel,
        grid=(DEPTH, n // tn),
        in_specs=[
            pl.BlockSpec((bp, D_MODEL), lambda l, j: (0, 0)),
            pl.BlockSpec((bs, D_MODEL), lambda l, j: (0, 0)),
            pl.BlockSpec((None, D_MODEL, tn), lambda l, j: (l, 0, j)),
            pl.BlockSpec((None, 1, tn), lambda l, j: (l, 0, j)),
        ],
        out_specs=[
            pl.BlockSpec((None, bp, tn), lambda l, j: (l, 0, j)),
            pl.BlockSpec((None, bs, tn), lambda l, j: (l, 0, j)),
        ],
        out_shape=[jax.ShapeDtypeStruct((DEPTH, bp, n), f32),
                   jax.ShapeDtypeStruct((DEPTH, bs, n), f32)],
        compiler_params=_params(("arbitrary", "arbitrary")),
        name="ada",
    )(c_p, c_s, w_ada, b_ada.reshape(DEPTH, 1, n))


class _Mods:
    def __init__(self, arr, per_row):
        self.arr = arr
        self.per_row = per_row

    def spec(self, l, sub, kind, tm):
        col = sub * 3 + kind
        if self.per_row:
            return pl.BlockSpec((None, None, tm, D_MODEL), lambda g, i: (l, g, i, col))
        return pl.BlockSpec((None, None, 1, D_MODEL), lambda g, i: (l, g, 0, col))


def _tok_spec(tm, width, colblk=0):
    return pl.BlockSpec((None, tm, width), lambda g, i: (g, i, colblk))


def _ffn_kernel(x_ref, sh_ref, sc_ref, gt_ref, nw_ref, wg_ref, wu_ref, wd_ref, *rest, final):
    if final:
        fw_ref, o_ref, act_ref = rest
    else:
        o_ref, act_ref = rest
    x = x_ref[...]
    h = _norm_mod(x, nw_ref[...], sc_ref[...], sh_ref[...]).astype(bf16)
    for c in range(D_FF // FF_TILE):
        sl = slice(c * FF_TILE, (c + 1) * FF_TILE)
        g = jnp.dot(h, wg_ref[:, sl], preferred_element_type=f32)
        u = jnp.dot(h, wu_ref[:, sl], preferred_element_type=f32)
        act_ref[:, sl] = (_silu(g) * u).astype(bf16)
    y = jnp.dot(act_ref[...], wd_ref[...], preferred_element_type=f32)
    out = x + FFN_RES * (1.0 + gt_ref[...]) * y
    if final:
        out = _rms(out) * fw_ref[...]
    o_ref[...] = out


def _ffn(x, mods, l, j, sub, nw3, wgu, wdn, final_w=None):
    G, R, _ = x.shape
    tm = min(R, 512)
    final = final_w is not None
    in_specs = [
        _tok_spec(tm, D_MODEL),
        mods.spec(l, sub, 0, tm), mods.spec(l, sub, 1, tm), mods.spec(l, sub, 2, tm),
        pl.BlockSpec((None, 1, D_MODEL), lambda g, i: (l * N_SUB + sub, 0, 0)),
        _resident((None, None, D_MODEL, D_FF), lambda g, i: (l, j, 0, 0)),
        _resident((None, None, D_MODEL, D_FF), lambda g, i: (l, j, 0, 1)),
        _resident((None, None, D_FF, D_MODEL), lambda g, i: (l, j, 0, 0)),
    ]
    args = [x, mods.arr, mods.arr, mods.arr, nw3, wgu, wgu, wdn]
    if final:
        in_specs.append(pl.BlockSpec((1, D_MODEL), lambda g, i: (0, 0)))
        args.append(final_w.reshape(1, D_MODEL))
    return pl.pallas_call(
        functools.partial(_ffn_kernel, final=final),
        grid=(G, R // tm),
        in_specs=in_specs,
        out_specs=_tok_spec(tm, D_MODEL),
        out_shape=jax.ShapeDtypeStruct(x.shape, f32),
        scratch_shapes=[pltpu.VMEM((tm, D_FF), bf16)],
        compiler_params=_params(("arbitrary", "arbitrary")),
        name="ffn",
    )(*args)


def _a_in_kernel(x_ref, sh_ref, sc_ref, nw_ref, wm_ref, ws_ref, alog_ref, dtb_ref,
                 proj_ref, gates_ref, *, cumsum):
    tm = x_ref.shape[0]
    hf = _norm_mod(x_ref[...], nw_ref[...], sc_ref[...], sh_ref[...])
    h = hf.astype(bf16)
    for c in range(A_MAIN_DIM // D_MODEL):
        sl = slice(c * D_MODEL, (c + 1) * D_MODEL)
        proj_ref[:, sl] = jnp.dot(h, wm_ref[:, sl], preferred_element_type=f32)
    p = _dot3(hf, ws_ref[...])
    lane = lax.broadcasted_iota(jnp.int32, p.shape, 1)
    beta = jax.nn.sigmoid(p)
    g = -jnp.exp(alog_ref[...]) * _softplus(p + dtb_ref[...])
    is_g = (lane >= A_HEADS) & (lane < 2 * A_HEADS)
    gates = jnp.where(lane < A_HEADS, beta, jnp.where(is_g, g, 0.0))
    if cumsum:
        gsh = pltpu.roll(jnp.where(is_g, g, 0.0), A_HEADS, 1)
        ghi, gmid, glo = _split3(gsh)
        ri = lax.broadcasted_iota(jnp.int32, (CHUNK_A, CHUNK_A), 0)
        ci = lax.broadcasted_iota(jnp.int32, (CHUNK_A, CHUNK_A), 1)
        tri = jnp.where(ri >= ci, 1.0, 0.0).astype(bf16)
        d = functools.partial(jnp.dot, preferred_element_type=f32)
        for c in range(tm // CHUNK_A):
            rs = slice(c * CHUNK_A, (c + 1) * CHUNK_A)
            dc = d(tri, ghi[rs]) + d(tri, gmid[rs]) + d(tri, glo[rs])
            gates_ref[rs, :] = gates[rs] + dc
    else:
        gates_ref[...] = gates


def _a_in(x, mods, l, nw3, wm, ws, alog, dtb, cumsum):
    G, R, _ = x.shape
    tm = min(R, 512)
    sub = 1
    return pl.pallas_call(
        functools.partial(_a_in_kernel, cumsum=cumsum),
        grid=(G, R // tm),
        in_specs=[
            _tok_spec(tm, D_MODEL),
            mods.spec(l, sub, 0, tm), mods.spec(l, sub, 1, tm),
            pl.BlockSpec((None, 1, D_MODEL), lambda g, i: (l * N_SUB + sub, 0, 0)),
            _resident((D_MODEL, A_MAIN_DIM), lambda g, i: (0, 0)),
            _resident((D_MODEL, LANES), lambda g, i: (0, 0)),
            pl.BlockSpec((1, LANES), lambda g, i: (0, 0)),
            pl.BlockSpec((1, LANES), lambda g, i: (0, 0)),
        ],
        out_specs=[_tok_spec(tm, A_MAIN_DIM), _tok_spec(tm, LANES)],
        out_shape=[jax.ShapeDtypeStruct((G, R, A_MAIN_DIM), f32),
                   jax.ShapeDtypeStruct((G, R, LANES), f32)],
        compiler_params=_params(("arbitrary", "arbitrary")),
        name="a_in",
    )(x, mods.arr, mods.arr, nw3, wm, ws, alog, dtb)


A_LT = 256


def _unit_lower_inverse(m):
    ri = lax.broadcasted_iota(jnp.int32, m.shape, 0)
    ci = lax.broadcasted_iota(jnp.int32, m.shape, 1)
    p = jnp.where(ri == ci, 1.0, 0.0) - m
    mp = m
    n = 2
    while n < CHUNK_A:
        mp = _dot3(mp, mp)
        p = p + _dot3(p, mp)
        n *= 2
    return p


def _a_delta_kernel(q_ref, k_ref, v_ref, gcol_ref, grow_ref, wq_ref, wk_ref, wv_ref,
                    o_ref, s_ref, cbuf):
    h = pl.program_id(1)
    t = pl.program_id(2)
    lt = q_ref.shape[0]

    @pl.when(t == 0)
    def _():
        s_ref[...] = jnp.zeros(s_ref.shape, f32)
        cbuf[:, 0:8, :] = jnp.zeros((3, 8, LANES), f32)

    def conv_silu(idx, pre_ref, w_ref):
        cbuf[idx, 8:8 + lt, :] = pre_ref[...]
        acc = cbuf[idx, 5:5 + lt, :] * w_ref[0:1, :]
        for j in range(1, CONV_W):
            acc = acc + cbuf[idx, 5 + j:5 + j + lt, :] * w_ref[j:j + 1, :]
        cbuf[idx, 0:8, :] = cbuf[idx, lt:lt + 8, :]
        return _silu(acc)

    q = _l2n(conv_silu(0, q_ref, wq_ref)) * (A_DK ** -0.5)
    k = _l2n(conv_silu(1, k_ref, wk_ref))
    v = conv_silu(2, v_ref, wv_ref)

    gates = gcol_ref[...]
    lane = lax.broadcasted_iota(jnp.int32, gates.shape, 1)
    beta_all = jnp.sum(jnp.where(lane == h, gates, 0.0), axis=-1, keepdims=True)
    d_all = jnp.sum(jnp.where(lane == h + 2 * A_HEADS, gates, 0.0), axis=-1, keepdims=True)

    ri = lax.broadcasted_iota(jnp.int32, (CHUNK_A, CHUNK_A), 0)
    ci = lax.broadcasted_iota(jnp.int32, (CHUNK_A, CHUNK_A), 1)
    causal = ri >= ci
    strict = ri > ci

    s = s_ref[...]
    for c in range(lt // CHUNK_A):
        rs = slice(c * CHUNK_A, (c + 1) * CHUNK_A)
        qc, kc, vc = q[rs], k[rs], v[rs]
        bcol, dcol = beta_all[rs], d_all[rs]
        drow = grow_ref[c, pl.ds(h + 2 * A_HEADS, 1), :]
        dlast = dcol[CHUNK_A - 1:CHUNK_A, :]
        gam = jnp.where(causal, jnp.exp(jnp.where(causal, dcol - drow, 0.0)), 0.0)
        kb = kc * bcol
        m = jnp.where(strict, _bdot_nt(kb, kc) * gam, 0.0)
        attn = _bdot_nt(qc, kc) * gam
        ed = jnp.exp(dcol)
        tinv = _unit_lower_inverse(m)
        u = _bdot(tinv, vc * bcol)
        w = _bdot(tinv, kb * ed)
        v_new = u - _bdot(w, s)
        o_ref[rs, :] = _bdot(qc * ed, s) + _bdot(attn, v_new)
        s = s * jnp.exp(dlast) + _bdot_tn(kc * jnp.exp(dlast - dcol), v_new)
    s_ref[...] = s


def _a_delta(proj, gates, grow, wconv):
    B, L, _ = proj.shape
    lt = A_LT
    nq = A_QK_DIM // LANES
    tok = lambda off: pl.BlockSpec((None, lt, LANES), lambda b, h, t: (b, t, off + h))
    wsp = lambda off: pl.BlockSpec((CONV_W, LANES), lambda b, h, t: (0, off + h))
    return pl.pallas_call(
        _a_delta_kernel,
        grid=(B, A_HEADS, L // lt),
        in_specs=[
            tok(0), tok(nq), tok(2 * nq),
            pl.BlockSpec((None, lt, LANES), lambda b, h, t: (b, t, 0)),
            pl.BlockSpec((None, lt // CHUNK_A, GATE_COLS, CHUNK_A), lambda b, h, t: (b, t, 0, 0)),
            wsp(0), wsp(nq), wsp(2 * nq),
        ],
        out_specs=[
            pl.BlockSpec((None, lt, LANES), lambda b, h, t: (b, t, h)),
            pl.BlockSpec((None, None, A_DK, A_DV), lambda b, h, t: (b, h, 0, 0)),
        ],
        out_shape=[jax.ShapeDtypeStruct((B, L, A_V_DIM), f32),
                   jax.ShapeDtypeStruct((B, A_HEADS, A_DK, A_DV), f32)],
        scratch_shapes=[pltpu.VMEM((3, lt + 8, LANES), f32)],
        compiler_params=_params(("arbitrary", "arbitrary", "arbitrary")),
        name="a_delta",
    )(proj, proj, proj, gates, grow, wconv, wconv, wconv)


A_BT = 8


def _a_step_kernel(pre_ref, conv_ref, gates_ref, wc_ref, s_ref, o_ref, so_ref):
    bt = pre_ref.shape[0]
    wc = wc_ref[...]
    acc = pre_ref[...] * wc[CONV_W - 1:CONV_W, :]
    for j in range(CONV_W - 1):
        acc = acc + conv_ref[:, j, :] * wc[j:j + 1, :]
    qkv = _silu(acc)
    gates = gates_ref[...]
    row = lax.broadcasted_iota(jnp.int32, (bt, 8, LANES), 1)
    for h in range(A_HEADS):
        q = _l2n(qkv[:, h * A_DK:(h + 1) * A_DK]) * (A_DK ** -0.5)
        k = _l2n(qkv[:, A_QK_DIM + h * A_DK:A_QK_DIM + (h + 1) * A_DK])
        v = qkv[:, 2 * A_QK_DIM + h * A_DV:2 * A_QK_DIM + (h + 1) * A_DV]
        beta = gates[:, h:h + 1]
        eg = jnp.exp(gates[:, A_HEADS + h:A_HEADS + h + 1])
        s = s_ref[:, h]
        kq = jnp.where(row == 0, k[:, None, :], jnp.where(row == 1, q[:, None, :], 0.0))
        ks = jnp.einsum('bmk,bkn->bmn', kq.astype(bf16), s.astype(bf16),
                        preferred_element_type=f32)
        v_new = beta * (v - eg * ks[:, 0, :])
        qk = jnp.sum(q * k, axis=-1, keepdims=True)
        o_ref[:, h * A_DV:(h + 1) * A_DV] = eg * ks[:, 1, :] + qk * v_new
        kh = k.astype(bf16).astype(f32)
        vh = v_new.astype(bf16).astype(f32)
        kl, vl = k - kh, v_new - vh
        lhs = jnp.where(row == 2, kl[:, None, :], jnp.where(row < 2, kh[:, None, :], 0.0))
        rhs = jnp.where(row == 1, vl[:, None, :], jnp.where(row < 3, vh[:, None, :], 0.0))
        outer = lax.dot_general(lhs.astype(bf16), rhs.astype(bf16), (((1,), (1,)), ((0,), (0,))),
                                preferred_element_type=f32)
        so_ref[:, h] = s * eg[:, :, None] + outer


def _a_step(proj, conv_state, gates, wconv, s0):
    nb = proj.shape[0]
    bt = A_BT
    return pl.pallas_call(
        _a_step_kernel,
        grid=(nb // bt,),
        in_specs=[
            pl.BlockSpec((bt, A_CONV_DIM), lambda i: (i, 0)),
            pl.BlockSpec((bt, CONV_W - 1, A_CONV_DIM), lambda i: (i, 0, 0)),
            pl.BlockSpec((bt, LANES), lambda i: (i, 0)),
            pl.BlockSpec((CONV_W, A_CONV_DIM), lambda i: (0, 0)),
            pl.BlockSpec((bt, A_HEADS, A_DK, A_DV), lambda i: (i, 0, 0, 0)),
        ],
        out_specs=[
            pl.BlockSpec((bt, A_V_DIM), lambda i: (i, 0)),
            pl.BlockSpec((bt, A_HEADS, A_DK, A_DV), lambda i: (i, 0, 0, 0)),
        ],
        out_shape=[jax.ShapeDtypeStruct((nb, A_V_DIM), f32),
                   jax.ShapeDtypeStruct(s0.shape, f32)],
        compiler_params=_params(("arbitrary",)),
        name="a_step",
    )(proj, conv_state, gates, wconv, s0)


def _a_out_kernel(x_ref, o_ref, z_ref, gt_ref, won_ref, wo_ref, y_ref, og_ref):
    won = won_ref[...]
    for h in range(A_HEADS):
        sl = slice(h * A_DV, (h + 1) * A_DV)
        og_ref[:, sl] = (_rms(o_ref[:, sl]) * won * _silu(z_ref[:, sl])).astype(bf16)
    y = jnp.dot(og_ref[...], wo_ref[...], preferred_element_type=f32)
    y_ref[...] = x_ref[...] + (1.0 + gt_ref[...]) * y


def _a_out(x, o, proj, mods, l, won, wo):
    G, R, _ = x.shape
    tm = min(R, 512)
    return pl.pallas_call(
        _a_out_kernel,
        grid=(G, R // tm),
        in_specs=[
            _tok_spec(tm, D_MODEL),
            _tok_spec(tm, A_V_DIM),
            _tok_spec(tm, A_V_DIM, A_CONV_DIM // A_V_DIM),
            mods.spec(l, 1, 2, tm),
            pl.BlockSpec((1, A_DV), lambda g, i: (0, 0)),
            _resident((A_V_DIM, D_MODEL), lambda g, i: (0, 0)),
        ],
        out_specs=_tok_spec(tm, D_MODEL),
        out_shape=jax.ShapeDtypeStruct(x.shape, f32),
        scratch_shapes=[pltpu.VMEM((tm, A_V_DIM), bf16)],
        compiler_params=_params(("arbitrary", "arbitrary")),
        name="a_out",
    )(x, o, proj, mods.arr, won, wo)


def _b_kernel(x_ref, sh_ref, sc_ref, gt_ref, nw_ref, wi_ref, vw_ref, vb_ref, ws_ref, bs_ref, wo_ref,
              *rest, decode):
    if decode:
        y_ref, vout_ref, vn_ref, gd_ref = rest
    else:
        y_ref, vn_ref, gd_ref = rest
    tm = x_ref.shape[0]
    x = x_ref[...]
    h = _norm_mod(x, nw_ref[...], sc_ref[...], sh_ref[...]).astype(bf16)
    v = jax.nn.gelu(jnp.dot(h, wi_ref[:, B_WIDTH:], preferred_element_type=f32))
    mu = jnp.mean(v, axis=-1, keepdims=True)
    vc = v - mu
    vn = vc * lax.rsqrt(jnp.mean(vc * vc, axis=-1, keepdims=True) + EPS) * vw_ref[...] + vb_ref[...]
    if decode:
        vout_ref[...] = vn
    else:
        vn_ref[...] = vn.astype(bf16)
        ri = lax.broadcasted_iota(jnp.int32, (CHUNK_B, CHUNK_B), 0)
        ci = lax.broadcasted_iota(jnp.int32, (CHUNK_B, CHUNK_B), 1)
        causal = ri >= ci
    for g in range(B_GROUPS):
        gs = slice(g * B_GROUP_DIM, (g + 1) * B_GROUP_DIM)
        u = jax.nn.gelu(jnp.dot(h, wi_ref[:, gs], preferred_element_type=f32))
        if decode:
            gd_ref[:, gs] = (u * (ws_ref[:, gs] * vn[:, gs] + bs_ref[:, gs])).astype(bf16)
        else:
            wsg = jnp.where(causal, ws_ref[g], 0.0).astype(bf16)
            bias = bs_ref[:, g:g + 1]
            for c in range(tm // CHUNK_B):
                rs = slice(c * CHUNK_B, (c + 1) * CHUNK_B)
                mixed = jnp.dot(wsg, vn_ref[rs, gs], preferred_element_type=f32) + bias
                gd_ref[rs, gs] = (u[rs] * mixed).astype(bf16)
    y = jnp.dot(gd_ref[...], wo_ref[...], preferred_element_type=f32)
    y_ref[...] = x + (1.0 + gt_ref[...]) * y


def _b_mixer(x, mods, l, nw3, wi, vw, vb, ws, bs, wo, decode):
    G, R, _ = x.shape
    tm = min(R, 256)
    sub = 1
    if decode:
        ws_spec = pl.BlockSpec((1, B_WIDTH), lambda g, i: (0, 0))
        bs_spec = pl.BlockSpec((1, B_WIDTH), lambda g, i: (0, 0))
        out_specs = [_tok_spec(tm, D_MODEL), _tok_spec(tm, B_WIDTH)]
        out_shape = [jax.ShapeDtypeStruct(x.shape, f32), jax.ShapeDtypeStruct((G, R, B_WIDTH), f32)]
    else:
        ws_spec = pl.BlockSpec((B_GROUPS, CHUNK_B, CHUNK_B), lambda g, i: (0, 0, 0))
        bs_spec = pl.BlockSpec((CHUNK_B, B_GROUPS), lambda g, i: (0, 0))
        out_specs = _tok_spec(tm, D_MODEL)
        out_shape = jax.ShapeDtypeStruct(x.shape, f32)
    return pl.pallas_call(
        functools.partial(_b_kernel, decode=decode),
        grid=(G, R // tm),
        in_specs=[
            _tok_spec(tm, D_MODEL),
            mods.spec(l, sub, 0, tm), mods.spec(l, sub, 1, tm), mods.spec(l, sub, 2, tm),
            pl.BlockSpec((None, 1, D_MODEL), lambda g, i: (l * N_SUB + sub, 0, 0)),
            _resident((D_MODEL, 2 * B_WIDTH), lambda g, i: (0, 0)),
            pl.BlockSpec((1, B_WIDTH), lambda g, i: (0, 0)),
            pl.BlockSpec((1, B_WIDTH), lambda g, i: (0, 0)),
            ws_spec, bs_spec,
            _resident((B_WIDTH, D_MODEL), lambda g, i: (0, 0)),
        ],
        out_specs=out_specs,
        out_shape=out_shape,
        scratch_shapes=[pltpu.VMEM((tm, B_WIDTH), bf16), pltpu.VMEM((tm, B_WIDTH), bf16)],
        compiler_params=_params(("arbitrary", "arbitrary")),
        name="b_mixer",
    )(x, mods.arr, mods.arr, mods.arr, nw3, wi, vw, vb, ws, bs, wo)


def _trunk(x, mods, decode, conv_state, s_state, W):
    new_conv, new_s, v_rows = [], [], []
    G, R, _ = x.shape
    for i in range(DEPTH):
        li = i // N_MIXERS
        x = _ffn(x, mods, i, 0, 0, W["nw3"], W["wgu"], W["wdn"])
        if i % N_MIXERS == 0:
            proj, gates = _a_in(x, mods, i, W["nw3"], W["a_wm"][li], W["a_ws"][li],
                                W["a_alog"][li], W["a_dtb"][li], cumsum=not decode)
            if decode:
                pre = proj[0, :, :A_CONV_DIM]
                o, s_new = _a_step(pre, conv_state[li], gates[0], W["a_wconv"][li], s_state[li])
                o = o[None]
                new_conv.append(jnp.concatenate([conv_state[li][:, 1:], pre[:, None, :]], axis=1))
            else:
                grow = gates[..., :GATE_COLS].reshape(G, R // CHUNK_A, CHUNK_A, GATE_COLS)
                grow = jnp.swapaxes(grow, 2, 3)
                o, s_new = _a_delta(proj, gates, grow, W["a_wconv"][li])
                new_conv.append(proj[:, R - (CONV_W - 1):, :A_CONV_DIM])
            new_s.append(s_new)
            x = _a_out(x, o, proj, mods, i, W["a_won"][li], W["a_wo"][li])
        else:
            if decode:
                x, v = _b_mixer(x, mods, i, W["nw3"], W["b_wi"][li], W["b_vw"][li], W["b_vb"][li],
                                W["b_ws00"][li], W["b_bs0"][li], W["b_wo"][li], decode=True)
                v_rows.append(v)
            else:
                x = _b_mixer(x, mods, i, W["nw3"], W["b_wi"][li], W["b_vw"][li], W["b_vb"][li],
                             W["b_ws"][li], W["b_bsT"][li], W["b_wo"][li], decode=False)
        fw = W["final_w"] if i == DEPTH - 1 else None
        x = _ffn(x, mods, i, 1, 2, W["nw3"], W["wgu"], W["wdn"], final_w=fw)
    return x, new_conv, new_s, v_rows


def kernel(x_prompt, x_sample, state_a_conv, state_a_S, c_prompt, c_sample, w_ada, b_ada, norm_w, ffn_w_gu, ffn_w_down, a_w_in, a_w_conv, a_log, a_dt_bias, a_w_onorm, a_w_out, b_w_in, b_vnorm_w, b_vnorm_b, b_w_s, b_b_s, b_w_out, final_norm_w):
    n_a = a_w_in.shape[0]
    n_b = b_w_in.shape[0]
    bp, seq, _ = x_prompt.shape
    bs = x_sample.shape[0]
    assert x_sample.shape[1] == 1 and seq % A_LT == 0 and seq % CHUNK_B == 0

    mod_p, mod_s = _ada(c_prompt, c_sample, w_ada, b_ada)
    mods_p = _Mods(mod_p.reshape(DEPTH, bp, 1, N_SUB * 3 * D_MODEL), per_row=False)
    mods_s = _Mods(mod_s.reshape(DEPTH, 1, bs, N_SUB * 3 * D_MODEL), per_row=True)

    pad = LANES - 2 * A_HEADS
    gate_pad = lambda a: jnp.pad(a, ((0, 0), (A_HEADS, pad)))[:, None, :]
    W = {
        "nw3": norm_w.reshape(DEPTH * N_SUB, 1, D_MODEL),
        "wgu": ffn_w_gu.astype(bf16),
        "wdn": ffn_w_down.astype(bf16),
        "a_wm": a_w_in[:, :, :A_MAIN_DIM].astype(bf16),
        "a_ws": jnp.pad(a_w_in[:, :, A_MAIN_DIM:], ((0, 0), (0, 0), (0, pad))),
        "a_alog": gate_pad(a_log),
        "a_dtb": gate_pad(a_dt_bias),
        "a_wconv": a_w_conv,
        "a_won": a_w_onorm[:, None, :],
        "a_wo": a_w_out.astype(bf16),
        "b_wi": b_w_in.astype(bf16),
        "b_vw": b_vnorm_w[:, None, :],
        "b_vb": b_vnorm_b[:, None, :],
        "b_ws": b_w_s,
        "b_bsT": jnp.swapaxes(b_b_s, 1, 2),
        "b_ws00": jnp.repeat(b_w_s[:, :, 0, 0], B_GROUP_DIM, axis=1)[:, None, :],
        "b_bs0": jnp.repeat(b_b_s[:, :, 0], B_GROUP_DIM, axis=1)[:, None, :],
        "b_wo": b_w_out.astype(bf16),
        "final_w": final_norm_w,
    }

    y_p, conv_p, s_p, _ = _trunk(x_prompt, mods_p, False, None, None, W)
    y_s, conv_s, s_s, v_s = _trunk(x_sample.reshape(1, bs, D_MODEL), mods_s, True,
                                   state_a_conv, state_a_S, W)
    return (y_p,
            y_s.reshape(bs, 1, D_MODEL),
            jnp.stack(conv_p),
            jnp.stack(s_p),
            jnp.stack(conv_s),
            jnp.stack(s_s),
            jnp.stack(v_s).reshape(n_b, bs, 1, B_WIDTH))
```

```python
import functools

import jax
import jax.numpy as jnp
from jax import lax
from jax.experimental import pallas as pl
from jax.experimental.pallas import tpu as pltpu

f32 = jnp.float32
bf16 = jnp.bfloat16

D_MODEL = 1024
DEPTH = 4
N_MIXERS = 2
A_HEADS = 8
A_DK = 128
A_DV = 128
A_QK_DIM = A_HEADS * A_DK
A_V_DIM = A_HEADS * A_DV
A_CONV_DIM = 2 * A_QK_DIM + A_V_DIM
A_MAIN_DIM = A_CONV_DIM + A_V_DIM
CONV_W = 4
CHUNK_A = 64
B_WIDTH = 2 * D_MODEL
B_GROUPS = 8
B_GROUP_DIM = B_WIDTH // B_GROUPS
CHUNK_B = 128
D_FF = 2816
FFN_RES = 0.5
N_SUB = 3
EPS = 1e-6

LANES = 128
FF_TILE = 256
GATE_COLS = 3 * A_HEADS
VMEM_LIMIT = 52 * 1024 * 1024


def _params(sem):
    return pltpu.CompilerParams(dimension_semantics=sem, vmem_limit_bytes=VMEM_LIMIT)


def _resident(shape, index_map):
    return pl.BlockSpec(shape, index_map, pipeline_mode=pl.Buffered(1))


def _bdot(a, b):
    return jnp.dot(a.astype(bf16), b.astype(bf16), preferred_element_type=f32)


def _bdot_nt(a, b):
    return lax.dot_general(a.astype(bf16), b.astype(bf16), (((1,), (1,)), ((), ())),
                           preferred_element_type=f32)


def _bdot_tn(a, b):
    return lax.dot_general(a.astype(bf16), b.astype(bf16), (((0,), (0,)), ((), ())),
                           preferred_element_type=f32)


def _split2(a):
    hi = a.astype(bf16)
    lo = (a - hi.astype(f32)).astype(bf16)
    return hi, lo


def _split3(a):
    hi = a.astype(bf16)
    r = a - hi.astype(f32)
    mid = r.astype(bf16)
    lo = (r - mid.astype(f32)).astype(bf16)
    return hi, mid, lo


def _dot3(a, b):
    ah, al = _split2(a)
    bh, bl = _split2(b)
    d = functools.partial(jnp.dot, preferred_element_type=f32)
    return d(ah, bh) + d(al, bh) + d(ah, bl)


def _silu(x):
    return x * jax.nn.sigmoid(x)


def _softplus(x):
    return jnp.maximum(x, 0.0) + jnp.log1p(jnp.exp(-jnp.abs(x)))


def _rms(x):
    return x * lax.rsqrt(jnp.mean(x * x, axis=-1, keepdims=True) + EPS)


def _norm_mod(x, nw, sc, sh):
    return (_rms(x) * nw) * (1.0 + sc) + sh


def _l2n(x):
    return x * lax.rsqrt(jnp.sum(x * x, axis=-1, keepdims=True) + EPS)


def _ada_kernel(cp_ref, cs_ref, w_ref, b_ref, op_ref, os_ref):
    w = w_ref[...].astype(bf16)
    b = b_ref[...]
    op_ref[...] = _bdot(_silu(cp_ref[...]), w) + b
    os_ref[...] = _bdot(_silu(cs_ref[...]), w) + b


def _ada(c_p, c_s, w_ada, b_ada):
    n = N_SUB * 3 * D_MODEL
    tn = 9 * LANES
    bp, bs = c_p.shape[0], c_s.shape[0]
    return pl.pallas_call(
        _ada_kernel,
        grid=(DEPTH, n // tn),
        in_specs=[
            pl.BlockSpec((bp, D_MODEL), lambda l, j: (0, 0)),
            pl.BlockSpec((bs, D_MODEL), lambda l, j: (0, 0)),
            pl.BlockSpec((None, D_MODEL, tn), lambda l, j: (l, 0, j)),
            pl.BlockSpec((None, 1, tn), lambda l, j: (l, 0, j)),
        ],
        out_specs=[
            pl.BlockSpec((None, bp, tn), lambda l, j: (l, 0, j)),
            pl.BlockSpec((None, bs, tn), lambda l, j: (l, 0, j)),
        ],
        out_shape=[jax.ShapeDtypeStruct((DEPTH, bp, n), f32),
                   jax.ShapeDtypeStruct((DEPTH, bs, n), f32)],
        compiler_params=_params(("arbitrary", "arbitrary")),
        name="ada",
    )(c_p, c_s, w_ada, b_ada.reshape(DEPTH, 1, n))


class _Mods:
    def __init__(self, arr, per_row):
        self.arr = arr
        self.per_row = per_row

    def spec(self, l, sub, kind, tm):
        col = sub * 3 + kind
        if self.per_row:
            return pl.BlockSpec((None, None, tm, D_MODEL), lambda g, i: (l, g, i, col))
        return pl.BlockSpec((None, None, 1, D_MODEL), lambda g, i: (l, g, 0, col))


def _tok_spec(tm, width, colblk=0):
    return pl.BlockSpec((None, tm, width), lambda g, i: (g, i, colblk))


def _ffn_kernel(x_ref, sh_ref, sc_ref, gt_ref, nw_ref, wg_ref, wu_ref, wd_ref, *rest, final):
    if final:
        fw_ref, o_ref, act_ref = rest
    else:
        o_ref, act_ref = rest
    x = x_ref[...]
    h = _norm_mod(x, nw_ref[...], sc_ref[...], sh_ref[...]).astype(bf16)
    for c in range(D_FF // FF_TILE):
        sl = slice(c * FF_TILE, (c + 1) * FF_TILE)
        g = jnp.dot(h, wg_ref[:, sl], preferred_element_type=f32)
        u = jnp.dot(h, wu_ref[:, sl], preferred_element_type=f32)
        act_ref[:, sl] = (_silu(g) * u).astype(bf16)
    y = jnp.dot(act_ref[...], wd_ref[...], preferred_element_type=f32)
    out = x + FFN_RES * (1.0 + gt_ref[...]) * y
    if final:
        out = _rms(out) * fw_ref[...]
    o_ref[...] = out


def _ffn(x, mods, l, j, sub, nw3, wgu, wdn, final_w=None):
    G, R, _ = x.shape
    tm = min(R, 512)
    final = final_w is not None
    in_specs = [
        _tok_spec(tm, D_MODEL),
        mods.spec(l, sub, 0, tm), mods.spec(l, sub, 1, tm), mods.spec(l, sub, 2, tm),
        pl.BlockSpec((None, 1, D_MODEL), lambda g, i: (l * N_SUB + sub, 0, 0)),
        _resident((None, None, D_MODEL, D_FF), lambda g, i: (l, j, 0, 0)),
        _resident((None, None, D_MODEL, D_FF), lambda g, i: (l, j, 0, 1)),
        _resident((None, None, D_FF, D_MODEL), lambda g, i: (l, j, 0, 0)),
    ]
    args = [x, mods.arr, mods.arr, mods.arr, nw3, wgu, wgu, wdn]
    if final:
        in_specs.append(pl.BlockSpec((1, D_MODEL), lambda g, i: (0, 0)))
        args.append(final_w.reshape(1, D_MODEL))
    return pl.pallas_call(
        functools.partial(_ffn_kernel, final=final),
        grid=(G, R // tm),
        in_specs=in_specs,
        out_specs=_tok_spec(tm, D_MODEL),
        out_shape=jax.ShapeDtypeStruct(x.shape, f32),
        scratch_shapes=[pltpu.VMEM((tm, D_FF), bf16)],
        compiler_params=_params(("arbitrary", "arbitrary")),
        name="ffn",
    )(*args)


def _a_in_kernel(x_ref, sh_ref, sc_ref, nw_ref, wm_ref, ws_ref, alog_ref, dtb_ref,
                 proj_ref, gates_ref, *, cumsum):
    tm = x_ref.shape[0]
    hf = _norm_mod(x_ref[...], nw_ref[...], sc_ref[...], sh_ref[...])
    h = hf.astype(bf16)
    for c in range(A_MAIN_DIM // D_MODEL):
        sl = slice(c * D_MODEL, (c + 1) * D_MODEL)
        proj_ref[:, sl] = jnp.dot(h, wm_ref[:, sl], preferred_element_type=f32)
    p = _dot3(hf, ws_ref[...])
    lane = lax.broadcasted_iota(jnp.int32, p.shape, 1)
    beta = jax.nn.sigmoid(p)
    g = -jnp.exp(alog_ref[...]) * _softplus(p + dtb_ref[...])
    is_g = (lane >= A_HEADS) & (lane < 2 * A_HEADS)
    gates = jnp.where(lane < A_HEADS, beta, jnp.where(is_g, g, 0.0))
    if cumsum:
        gsh = pltpu.roll(jnp.where(is_g, g, 0.0), A_HEADS, 1)
        ghi, gmid, glo = _split3(gsh)
        ri = lax.broadcasted_iota(jnp.int32, (CHUNK_A, CHUNK_A), 0)
        ci = lax.broadcasted_iota(jnp.int32, (CHUNK_A, CHUNK_A), 1)
        tri = jnp.where(ri >= ci, 1.0, 0.0).astype(bf16)
        d = functools.partial(jnp.dot, preferred_element_type=f32)
        for c in range(tm // CHUNK_A):
            rs = slice(c * CHUNK_A, (c + 1) * CHUNK_A)
            dc = d(tri, ghi[rs]) + d(tri, gmid[rs]) + d(tri, glo[rs])
            gates_ref[rs, :] = gates[rs] + dc
    else:
        gates_ref[...] = gates


def _a_in(x, mods, l, nw3, wm, ws, alog, dtb, cumsum):
    G, R, _ = x.shape
    tm = min(R, 512)
    sub = 1
    return pl.pallas_call(
        functools.partial(_a_in_kernel, cumsum=cumsum),
        grid=(G, R // tm),
        in_specs=[
            _tok_spec(tm, D_MODEL),
            mods.spec(l, sub, 0, tm), mods.spec(l, sub, 1, tm),
            pl.BlockSpec((None, 1, D_MODEL), lambda g, i: (l * N_SUB + sub, 0, 0)),
            _resident((D_MODEL, A_MAIN_DIM), lambda g, i: (0, 0)),
            _resident((D_MODEL, LANES), lambda g, i: (0, 0)),
            pl.BlockSpec((1, LANES), lambda g, i: (0, 0)),
            pl.BlockSpec((1, LANES), lambda g, i: (0, 0)),
        ],
        out_specs=[_tok_spec(tm, A_MAIN_DIM), _tok_spec(tm, LANES)],
        out_shape=[jax.ShapeDtypeStruct((G, R, A_MAIN_DIM), f32),
                   jax.ShapeDtypeStruct((G, R, LANES), f32)],
        compiler_params=_params(("arbitrary", "arbitrary")),
        name="a_in",
    )(x, mods.arr, mods.arr, nw3, wm, ws, alog, dtb)


A_LT = 128


def _a_delta_kernel(pre_ref, gcol_ref, grow_ref, wc_ref, o_ref, s_ref, cbuf):
    t = pl.program_id(1)
    lt = pre_ref.shape[0]
    nc = lt // CHUNK_A

    @pl.when(t == 0)
    def _():
        s_ref[...] = jnp.zeros(s_ref.shape, f32)
        cbuf[0:8, :] = jnp.zeros((8, A_CONV_DIM), f32)

    cbuf[8:8 + lt, :] = pre_ref[...]
    acc = cbuf[5:5 + lt, :] * wc_ref[0:1, :]
    for j in range(1, CONV_W):
        acc = acc + cbuf[5 + j:5 + j + lt, :] * wc_ref[j:j + 1, :]
    cbuf[0:8, :] = cbuf[lt:lt + 8, :]
    qkv = _silu(acc)
    gates = gcol_ref[...]

    ri = lax.broadcasted_iota(jnp.int32, (CHUNK_A, CHUNK_A), 0)
    ci = lax.broadcasted_iota(jnp.int32, (CHUNK_A, CHUNK_A), 1)
    causal = ri >= ci
    strict = ri > ci

    inst = []
    for h in range(A_HEADS):
        q = _l2n(qkv[:, h * A_DK:(h + 1) * A_DK]) * (A_DK ** -0.5)
        k = _l2n(qkv[:, A_QK_DIM + h * A_DK:A_QK_DIM + (h + 1) * A_DK])
        v = qkv[:, 2 * A_QK_DIM + h * A_DV:2 * A_QK_DIM + (h + 1) * A_DV]
        for c in range(nc):
            rs = slice(c * CHUNK_A, (c + 1) * CHUNK_A)
            qc, kc, vc = q[rs], k[rs], v[rs]
            bcol = gates[rs, h:h + 1]
            dcol = gates[rs, 2 * A_HEADS + h:2 * A_HEADS + h + 1]
            drow = grow_ref[c, 2 * A_HEADS + h:2 * A_HEADS + h + 1, :]
            dlast = dcol[CHUNK_A - 1:CHUNK_A, :]
            gam = jnp.where(causal, jnp.exp(jnp.where(causal, dcol - drow, 0.0)), 0.0)
            kb = kc * bcol
            ed = jnp.exp(dcol)
            kq = _bdot_nt(jnp.concatenate([kb, qc], axis=0), kc)
            m = jnp.where(strict, kq[:CHUNK_A] * gam, 0.0)
            attn = kq[CHUNK_A:] * gam
            kdec_t = (kc * jnp.exp(dlast - dcol)).T
            inst.append(dict(
                h=h, c=c, m=m,
                rhs=jnp.concatenate([vc * bcol, kb * ed], axis=1),
                qed=(qc * ed).astype(bf16),
                lhs2=jnp.concatenate([attn, kdec_t], axis=0).astype(bf16),
                edl=jnp.exp(dlast)))

    ms = [i["m"] for i in inst]
    ts = [-m for m in ms]
    n = 2
    while n < CHUNK_A:
        ms = [_bdot(m, m) for m in ms]
        ts = [t + m + _bdot(t, m) for t, m in zip(ts, ms)]
        n *= 2
    rs_ = [-(i["m"] + t + _dot3(i["m"], t)) for i, t in zip(inst, ts)]
    ts = [t + r + _bdot(t, r) for t, r in zip(ts, rs_)]

    for i, t in zip(inst, ts):
        uw = i["rhs"] + _bdot(t, i["rhs"])
        i["u"] = uw[:, :A_DV]
        i["lhs1"] = jnp.concatenate([uw[:, A_DV:].astype(bf16), i["qed"]], axis=0)

    states = [s_ref[h] for h in range(A_HEADS)]
    for c in range(nc):
        rs = slice(c * CHUNK_A, (c + 1) * CHUNK_A)
        for h in range(A_HEADS):
            i = inst[h * nc + c]
            ws_qs = jnp.dot(i["lhs1"], states[h].astype(bf16), preferred_element_type=f32)
            v_new = i["u"] - ws_qs[:CHUNK_A]
            r = jnp.dot(i["lhs2"], v_new.astype(bf16), preferred_element_type=f32)
            o_ref[rs, h * A_DV:(h + 1) * A_DV] = ws_qs[CHUNK_A:] + r[:CHUNK_A]
            states[h] = states[h] * i["edl"] + r[CHUNK_A:]
    for h in range(A_HEADS):
        s_ref[h] = states[h]


def _a_delta(proj, gates, grow, wconv):
    B, L, _ = proj.shape
    lt = A_LT
    return pl.pallas_call(
        _a_delta_kernel,
        grid=(B, L // lt),
        in_specs=[
            pl.BlockSpec((None, lt, A_CONV_DIM), lambda b, t: (b, t, 0)),
            pl.BlockSpec((None, lt, LANES), lambda b, t: (b, t, 0)),
            pl.BlockSpec((None, lt // CHUNK_A, GATE_COLS, CHUNK_A), lambda b, t: (b, t, 0, 0)),
            pl.BlockSpec((CONV_W, A_CONV_DIM), lambda b, t: (0, 0)),
        ],
        out_specs=[
            pl.BlockSpec((None, lt, A_V_DIM), lambda b, t: (b, t, 0)),
            pl.BlockSpec((None, A_HEADS, A_DK, A_DV), lambda b, t: (b, 0, 0, 0)),
        ],
        out_shape=[jax.ShapeDtypeStruct((B, L, A_V_DIM), f32),
                   jax.ShapeDtypeStruct((B, A_HEADS, A_DK, A_DV), f32)],
        scratch_shapes=[pltpu.VMEM((lt + 8, A_CONV_DIM), f32)],
        compiler_params=_params(("arbitrary", "arbitrary")),
        name="a_delta",
    )(proj, gates, grow, wconv)


A_BT = 8


def _a_step_kernel(pre_ref, conv_ref, gates_ref, wc_ref, s_ref, o_ref, so_ref):
    bt = pre_ref.shape[0]
    wc = wc_ref[...]
    acc = pre_ref[...] * wc[CONV_W - 1:CONV_W, :]
    for j in range(CONV_W - 1):
        acc = acc + conv_ref[:, j, :] * wc[j:j + 1, :]
    qkv = _silu(acc)
    gates = gates_ref[...]
    row = lax.broadcasted_iota(jnp.int32, (bt, 8, LANES), 1)
    for h in range(A_HEADS):
        q = _l2n(qkv[:, h * A_DK:(h + 1) * A_DK]) * (A_DK ** -0.5)
        k = _l2n(qkv[:, A_QK_DIM + h * A_DK:A_QK_DIM + (h + 1) * A_DK])
        v = qkv[:, 2 * A_QK_DIM + h * A_DV:2 * A_QK_DIM + (h + 1) * A_DV]
        beta = gates[:, h:h + 1]
        eg = jnp.exp(gates[:, A_HEADS + h:A_HEADS + h + 1])
        s = s_ref[:, h]
        kq = jnp.where(row == 0, k[:, None, :], jnp.where(row == 1, q[:, None, :], 0.0))
        ks = jnp.einsum('bmk,bkn->bmn', kq.astype(bf16), s.astype(bf16),
                        preferred_element_type=f32)
        v_new = beta * (v - eg * ks[:, 0, :])
        qk = jnp.sum(q * k, axis=-1, keepdims=True)
        o_ref[:, h * A_DV:(h + 1) * A_DV] = eg * ks[:, 1, :] + qk * v_new
        kh = k.astype(bf16).astype(f32)
        vh = v_new.astype(bf16).astype(f32)
        kl, vl = k - kh, v_new - vh
        lhs = jnp.where(row == 2, kl[:, None, :], jnp.where(row < 2, kh[:, None, :], 0.0))
        rhs = jnp.where(row == 1, vl[:, None, :], jnp.where(row < 3, vh[:, None, :], 0.0))
        outer = lax.dot_general(lhs.astype(bf16), rhs.astype(bf16), (((1,), (1,)), ((0,), (0,))),
                                preferred_element_type=f32)
        so_ref[:, h] = s * eg[:, :, None] + outer


def _a_step(proj, conv_state, gates, wconv, s0):
    nb = proj.shape[0]
    bt = A_BT
    return pl.pallas_call(
        _a_step_kernel,
        grid=(nb // bt,),
        in_specs=[
            pl.BlockSpec((bt, A_CONV_DIM), lambda i: (i, 0)),
            pl.BlockSpec((bt, CONV_W - 1, A_CONV_DIM), lambda i: (i, 0, 0)),
            pl.BlockSpec((bt, LANES), lambda i: (i, 0)),
            pl.BlockSpec((CONV_W, A_CONV_DIM), lambda i: (0, 0)),
            pl.BlockSpec((bt, A_HEADS, A_DK, A_DV), lambda i: (i, 0, 0, 0)),
        ],
        out_specs=[
            pl.BlockSpec((bt, A_V_DIM), lambda i: (i, 0)),
            pl.BlockSpec((bt, A_HEADS, A_DK, A_DV), lambda i: (i, 0, 0, 0)),
        ],
        out_shape=[jax.ShapeDtypeStruct((nb, A_V_DIM), f32),
                   jax.ShapeDtypeStruct(s0.shape, f32)],
        compiler_params=_params(("arbitrary",)),
        name="a_step",
    )(proj, conv_state, gates, wconv, s0)


def _a_out_kernel(x_ref, o_ref, z_ref, gt_ref, won_ref, wo_ref, y_ref, og_ref):
    won = won_ref[...]
    for h in range(A_HEADS):
        sl = slice(h * A_DV, (h + 1) * A_DV)
        og_ref[:, sl] = (_rms(o_ref[:, sl]) * won * _silu(z_ref[:, sl])).astype(bf16)
    y = jnp.dot(og_ref[...], wo_ref[...], preferred_element_type=f32)
    y_ref[...] = x_ref[...] + (1.0 + gt_ref[...]) * y


def _a_out(x, o, proj, mods, l, won, wo):
    G, R, _ = x.shape
    tm = min(R, 512)
    return pl.pallas_call(
        _a_out_kernel,
        grid=(G, R // tm),
        in_specs=[
            _tok_spec(tm, D_MODEL),
            _tok_spec(tm, A_V_DIM),
            _tok_spec(tm, A_V_DIM, A_CONV_DIM // A_V_DIM),
            mods.spec(l, 1, 2, tm),
            pl.BlockSpec((1, A_DV), lambda g, i: (0, 0)),
            _resident((A_V_DIM, D_MODEL), lambda g, i: (0, 0)),
        ],
        out_specs=_tok_spec(tm, D_MODEL),
        out_shape=jax.ShapeDtypeStruct(x.shape, f32),
        scratch_shapes=[pltpu.VMEM((tm, A_V_DIM), bf16)],
        compiler_params=_params(("arbitrary", "arbitrary")),
        name="a_out",
    )(x, o, proj, mods.arr, won, wo)


def _b_kernel(x_ref, sh_ref, sc_ref, gt_ref, nw_ref, wi_ref, vw_ref, vb_ref, ws_ref, bs_ref, wo_ref,
              *rest, decode):
    if decode:
        y_ref, vout_ref, vn_ref, gd_ref = rest
    else:
        y_ref, vn_ref, gd_ref = rest
    tm = x_ref.shape[0]
    x = x_ref[...]
    h = _norm_mod(x, nw_ref[...], sc_ref[...], sh_ref[...]).astype(bf16)
    v = jax.nn.gelu(jnp.dot(h, wi_ref[:, B_WIDTH:], preferred_element_type=f32))
    mu = jnp.mean(v, axis=-1, keepdims=True)
    vc = v - mu
    vn = vc * lax.rsqrt(jnp.mean(vc * vc, axis=-1, keepdims=True) + EPS) * vw_ref[...] + vb_ref[...]
    if decode:
        vout_ref[...] = vn
    else:
        vn_ref[...] = vn.astype(bf16)
        ri = lax.broadcasted_iota(jnp.int32, (CHUNK_B, CHUNK_B), 0)
        ci = lax.broadcasted_iota(jnp.int32, (CHUNK_B, CHUNK_B), 1)
        causal = ri >= ci
    for g in range(B_GROUPS):
        gs = slice(g * B_GROUP_DIM, (g + 1) * B_GROUP_DIM)
        u = jax.nn.gelu(jnp.dot(h, wi_ref[:, gs], preferred_element_type=f32))
        if decode:
            gd_ref[:, gs] = (u * (ws_ref[:, gs] * vn[:, gs] + bs_ref[:, gs])).astype(bf16)
        else:
            wsg = jnp.where(causal, ws_ref[g], 0.0).astype(bf16)
            bias = bs_ref[:, g:g + 1]
            for c in range(tm // CHUNK_B):
                rs = slice(c * CHUNK_B, (c + 1) * CHUNK_B)
                mixed = jnp.dot(wsg, vn_ref[rs, gs], preferred_element_type=f32) + bias
                gd_ref[rs, gs] = (u[rs] * mixed).astype(bf16)
    y = jnp.dot(gd_ref[...], wo_ref[...], preferred_element_type=f32)
    y_ref[...] = x + (1.0 + gt_ref[...]) * y


def _b_mixer(x, mods, l, nw3, wi, vw, vb, ws, bs, wo, decode):
    G, R, _ = x.shape
    tm = min(R, 256)
    sub = 1
    if decode:
        ws_spec = pl.BlockSpec((1, B_WIDTH), lambda g, i: (0, 0))
        bs_spec = pl.BlockSpec((1, B_WIDTH), lambda g, i: (0, 0))
        out_specs = [_tok_spec(tm, D_MODEL), _tok_spec(tm, B_WIDTH)]
        out_shape = [jax.ShapeDtypeStruct(x.shape, f32), jax.ShapeDtypeStruct((G, R, B_WIDTH), f32)]
    else:
        ws_spec = pl.BlockSpec((B_GROUPS, CHUNK_B, CHUNK_B), lambda g, i: (0, 0, 0))
        bs_spec = pl.BlockSpec((CHUNK_B, B_GROUPS), lambda g, i: (0, 0))
        out_specs = _tok_spec(tm, D_MODEL)
        out_shape = jax.ShapeDtypeStruct(x.shape, f32)
    return pl.pallas_call(
        functools.partial(_b_kernel, decode=decode),
        grid=(G, R // tm),
        in_specs=[
            _tok_spec(tm, D_MODEL),
            mods.spec(l, sub, 0, tm), mods.spec(l, sub, 1, tm), mods.spec(l, sub, 2, tm),
            pl.BlockSpec((None, 1, D_MODEL), lambda g, i: (l * N_SUB + sub, 0, 0)),
            _resident((D_MODEL, 2 * B_WIDTH), lambda g, i: (0, 0)),
            pl.BlockSpec((1, B_WIDTH), lambda g, i: (0, 0)),
            pl.BlockSpec((1, B_WIDTH), lambda g, i: (0, 0)),
            ws_spec, bs_spec,
            _resident((B_WIDTH, D_MODEL), lambda g, i: (0, 0)),
        ],
        out_specs=out_specs,
        out_shape=out_shape,
        scratch_shapes=[pltpu.VMEM((tm, B_WIDTH), bf16), pltpu.VMEM((tm, B_WIDTH), bf16)],
        compiler_params=_params(("arbitrary", "arbitrary")),
        name="b_mixer",
    )(x, mods.arr, mods.arr, mods.arr, nw3, wi, vw, vb, ws, bs, wo)


def _trunk(x, mods, decode, conv_state, s_state, W):
    new_conv, new_s, v_rows = [], [], []
    G, R, _ = x.shape
    for i in range(DEPTH):
        li = i // N_MIXERS
        x = _ffn(x, mods, i, 0, 0, W["nw3"], W["wgu"], W["wdn"])
        if i % N_MIXERS == 0:
            proj, gates = _a_in(x, mods, i, W["nw3"], W["a_wm"][li], W["a_ws"][li],
                                W["a_alog"][li], W["a_dtb"][li], cumsum=not decode)
            if decode:
                pre = proj[0, :, :A_CONV_DIM]
                o, s_new = _a_step(pre, conv_state[li], gates[0], W["a_wconv"][li], s_state[li])
                o = o[None]
                new_conv.append(jnp.concatenate([conv_state[li][:, 1:], pre[:, None, :]], axis=1))
            else:
                grow = gates[..., :GATE_COLS].reshape(G, R // CHUNK_A, CHUNK_A, GATE_COLS)
                grow = jnp.swapaxes(grow, 2, 3)
                o, s_new = _a_delta(proj, gates, grow, W["a_wconv"][li])
                new_conv.append(proj[:, R - (CONV_W - 1):, :A_CONV_DIM])
            new_s.append(s_new)
            x = _a_out(x, o, proj, mods, i, W["a_won"][li], W["a_wo"][li])
        else:
            if decode:
                x, v = _b_mixer(x, mods, i, W["nw3"], W["b_wi"][li], W["b_vw"][li], W["b_vb"][li],
                                W["b_ws00"][li], W["b_bs0"][li], W["b_wo"][li], decode=True)
                v_rows.append(v)
            else:
                x = _b_mixer(x, mods, i, W["nw3"], W["b_wi"][li], W["b_vw"][li], W["b_vb"][li],
                             W["b_ws"][li], W["b_bsT"][li], W["b_wo"][li], decode=False)
        fw = W["final_w"] if i == DEPTH - 1 else None
        x = _ffn(x, mods, i, 1, 2, W["nw3"], W["wgu"], W["wdn"], final_w=fw)
    return x, new_conv, new_s, v_rows


def kernel(x_prompt, x_sample, state_a_conv, state_a_S, c_prompt, c_sample, w_ada, b_ada, norm_w, ffn_w_gu, ffn_w_down, a_w_in, a_w_conv, a_log, a_dt_bias, a_w_onorm, a_w_out, b_w_in, b_vnorm_w, b_vnorm_b, b_w_s, b_b_s, b_w_out, final_norm_w):
    n_a = a_w_in.shape[0]
    n_b = b_w_in.shape[0]
    bp, seq, _ = x_prompt.shape
    bs = x_sample.shape[0]
    assert x_sample.shape[1] == 1 and seq % A_LT == 0 and seq % CHUNK_B == 0

    mod_p, mod_s = _ada(c_prompt, c_sample, w_ada, b_ada)
    mods_p = _Mods(mod_p.reshape(DEPTH, bp, 1, N_SUB * 3 * D_MODEL), per_row=False)
    mods_s = _Mods(mod_s.reshape(DEPTH, 1, bs, N_SUB * 3 * D_MODEL), per_row=True)

    pad = LANES - 2 * A_HEADS
    gate_pad = lambda a: jnp.pad(a, ((0, 0), (A_HEADS, pad)))[:, None, :]
    W = {
        "nw3": norm_w.reshape(DEPTH * N_SUB, 1, D_MODEL),
        "wgu": ffn_w_gu.astype(bf16),
        "wdn": ffn_w_down.astype(bf16),
        "a_wm": a_w_in[:, :, :A_MAIN_DIM].astype(bf16),
        "a_ws": jnp.pad(a_w_in[:, :, A_MAIN_DIM:], ((0, 0), (0, 0), (0, pad))),
        "a_alog": gate_pad(a_log),
        "a_dtb": gate_pad(a_dt_bias),
        "a_wconv": a_w_conv,
        "a_won": a_w_onorm[:, None, :],
        "a_wo": a_w_out.astype(bf16),
        "b_wi": b_w_in.astype(bf16),
        "b_vw": b_vnorm_w[:, None, :],
        "b_vb": b_vnorm_b[:, None, :],
        "b_ws": b_w_s,
        "b_bsT": jnp.swapaxes(b_b_s, 1, 2),
        "b_ws00": jnp.repeat(b_w_s[:, :, 0, 0], B_GROUP_DIM, axis=1)[:, None, :],
        "b_bs0": jnp.repeat(b_b_s[:, :, 0], B_GROUP_DIM, axis=1)[:, None, :],
        "b_wo": b_w_out.astype(bf16),
        "final_w": final_norm_w,
    }

    y_p, conv_p, s_p, _ = _trunk(x_prompt, mods_p, False, None, None, W)
    y_s, conv_s, s_s, v_s = _trunk(x_sample.reshape(1, bs, D_MODEL), mods_s, True,
                                   state_a_conv, state_a_S, W)
    return (y_p,
            y_s.reshape(bs, 1, D_MODEL),
            jnp.stack(conv_p),
            jnp.stack(s_p),
            jnp.stack(conv_s),
            jnp.stack(s_s),
            jnp.stack(v_s).reshape(n_b, bs, 1, B_WIDTH))
```

```python
import functools

import jax
import jax.numpy as jnp
from jax import lax
from jax.experimental import pallas as pl
from jax.experimental.pallas import tpu as pltpu

f32 = jnp.float32
bf16 = jnp.bfloat16

D_MODEL = 1024
DEPTH = 4
N_MIXERS = 2
A_HEADS = 8
A_DK = 128
A_DV = 128
A_QK_DIM = A_HEADS * A_DK
A_V_DIM = A_HEADS * A_DV
A_CONV_DIM = 2 * A_QK_DIM + A_V_DIM
A_MAIN_DIM = A_CONV_DIM + A_V_DIM
CONV_W = 4
CHUNK_A = 64
B_WIDTH = 2 * D_MODEL
B_GROUPS = 8
B_GROUP_DIM = B_WIDTH // B_GROUPS
CHUNK_B = 128
D_FF = 2816
FFN_RES = 0.5
N_SUB = 3
EPS = 1e-6

LANES = 128
FF_TILE = 256
GATE_COLS = 3 * A_HEADS
VMEM_LIMIT = 52 * 1024 * 1024


def _params(sem):
    return pltpu.CompilerParams(dimension_semantics=sem, vmem_limit_bytes=VMEM_LIMIT)


def _resident(shape, index_map):
    return pl.BlockSpec(shape, index_map, pipeline_mode=pl.Buffered(1))


def _bdot(a, b):
    return jnp.dot(a.astype(bf16), b.astype(bf16), preferred_element_type=f32)


def _bdot_nt(a, b):
    return lax.dot_general(a.astype(bf16), b.astype(bf16), (((1,), (1,)), ((), ())),
                           preferred_element_type=f32)


def _bdot_tn(a, b):
    return lax.dot_general(a.astype(bf16), b.astype(bf16), (((0,), (0,)), ((), ())),
                           preferred_element_type=f32)


def _split2(a):
    hi = a.astype(bf16)
    lo = (a - hi.astype(f32)).astype(bf16)
    return hi, lo


def _split3(a):
    hi = a.astype(bf16)
    r = a - hi.astype(f32)
    mid = r.astype(bf16)
    lo = (r - mid.astype(f32)).astype(bf16)
    return hi, mid, lo


def _dot3(a, b):
    ah, al = _split2(a)
    bh, bl = _split2(b)
    d = functools.partial(jnp.dot, preferred_element_type=f32)
    return d(ah, bh) + d(al, bh) + d(ah, bl)


def _silu(x):
    return x * jax.nn.sigmoid(x)


def _softplus(x):
    return jnp.maximum(x, 0.0) + jnp.log1p(jnp.exp(-jnp.abs(x)))


def _rms(x):
    return x * lax.rsqrt(jnp.mean(x * x, axis=-1, keepdims=True) + EPS)


def _norm_mod(x, nw, sc, sh):
    return (_rms(x) * nw) * (1.0 + sc) + sh


def _l2n(x):
    return x * lax.rsqrt(jnp.sum(x * x, axis=-1, keepdims=True) + EPS)


def _ada_kernel(cp_ref, cs_ref, w_ref, b_ref, op_ref, os_ref):
    w = w_ref[...].astype(bf16)
    b = b_ref[...]
    op_ref[...] = _bdot(_silu(cp_ref[...]), w) + b
    os_ref[...] = _bdot(_silu(cs_ref[...]), w) + b


def _ada(c_p, c_s, w_ada, b_ada):
    n = N_SUB * 3 * D_MODEL
    tn = 9 * LANES
    bp, bs = c_p.shape[0], c_s.shape[0]
    return pl.pallas_call(
        _ada_kernel,
        grid=(DEPTH, n // tn),
        in_specs=[
            pl.BlockSpec((bp, D_MODEL), lambda l, j: (0, 0)),
            pl.BlockSpec((bs, D_MODEL), lambda l, j: (0, 0)),
            pl.BlockSpec((None, D_MODEL, tn), lambda l, j: (l, 0, j)),
            pl.BlockSpec((None, 1, tn), lambda l, j: (l, 0, j)),
        ],
        out_specs=[
            pl.BlockSpec((None, bp, tn), lambda l, j: (l, 0, j)),
            pl.BlockSpec((None, bs, tn), lambda l, j: (l, 0, j)),
        ],
        out_shape=[jax.ShapeDtypeStruct((DEPTH, bp, n), f32),
                   jax.ShapeDtypeStruct((DEPTH, bs, n), f32)],
        compiler_params=_params(("arbitrary", "arbitrary")),
        name="ada",
    )(c_p, c_s, w_ada, b_ada.reshape(DEPTH, 1, n))


class _Mods:
    def __init__(self, arr, per_row):
        self.arr = arr
        self.per_row = per_row

    def spec(self, l, sub, kind, tm):
        col = sub * 3 + kind
        if self.per_row:
            return pl.BlockSpec((None, None, tm, D_MODEL), lambda g, i: (l, g, i, col))
        return pl.BlockSpec((None, None, 1, D_MODEL), lambda g, i: (l, g, 0, col))


def _tok_spec(tm, width, colblk=0):
    return pl.BlockSpec((None, tm, width), lambda g, i: (g, i, colblk))


def _ffn_kernel(x_ref, sh_ref, sc_ref, gt_ref, nw_ref, wg_ref, wu_ref, wd_ref, *rest, final):
    if final:
        fw_ref, o_ref, act_ref = rest
    else:
        o_ref, act_ref = rest
    x = x_ref[...]
    h = _norm_mod(x, nw_ref[...], sc_ref[...], sh_ref[...]).astype(bf16)
    for c in range(D_FF // FF_TILE):
        sl = slice(c * FF_TILE, (c + 1) * FF_TILE)
        g = jnp.dot(h, wg_ref[:, sl], preferred_element_type=f32)
        u = jnp.dot(h, wu_ref[:, sl], preferred_element_type=f32)
        act_ref[:, sl] = (_silu(g) * u).astype(bf16)
    y = jnp.dot(act_ref[...], wd_ref[...], preferred_element_type=f32)
    out = x + FFN_RES * (1.0 + gt_ref[...]) * y
    if final:
        out = _rms(out) * fw_ref[...]
    o_ref[...] = out


def _ffn(x, mods, l, j, sub, nw3, wgu, wdn, final_w=None):
    G, R, _ = x.shape
    tm = min(R, 512)
    final = final_w is not None
    in_specs = [
        _tok_spec(tm, D_MODEL),
        mods.spec(l, sub, 0, tm), mods.spec(l, sub, 1, tm), mods.spec(l, sub, 2, tm),
        pl.BlockSpec((None, 1, D_MODEL), lambda g, i: (l * N_SUB + sub, 0, 0)),
        _resident((None, None, D_MODEL, D_FF), lambda g, i: (l, j, 0, 0)),
        _resident((None, None, D_MODEL, D_FF), lambda g, i: (l, j, 0, 1)),
        _resident((None, None, D_FF, D_MODEL), lambda g, i: (l, j, 0, 0)),
    ]
    args = [x, mods.arr, mods.arr, mods.arr, nw3, wgu, wgu, wdn]
    if final:
        in_specs.append(pl.BlockSpec((1, D_MODEL), lambda g, i: (0, 0)))
        args.append(final_w.reshape(1, D_MODEL))
    return pl.pallas_call(
        functools.partial(_ffn_kernel, final=final),
        grid=(G, R // tm),
        in_specs=in_specs,
        out_specs=_tok_spec(tm, D_MODEL),
        out_shape=jax.ShapeDtypeStruct(x.shape, f32),
        scratch_shapes=[pltpu.VMEM((tm, D_FF), bf16)],
        compiler_params=_params(("arbitrary", "arbitrary")),
        name="ffn",
    )(*args)


def _a_in_kernel(x_ref, sh_ref, sc_ref, nw_ref, wm_ref, ws_ref, alog_ref, dtb_ref, *rest, cumsum):
    if cumsum:
        wc_ref, proj_ref, gates_ref, tail_ref, cbuf = rest
    else:
        proj_ref, gates_ref = rest
    tm = x_ref.shape[0]
    hf = _norm_mod(x_ref[...], nw_ref[...], sc_ref[...], sh_ref[...])
    h = hf.astype(bf16)
    nblk = A_MAIN_DIM // D_MODEL
    if cumsum:
        @pl.when(pl.program_id(1) == 0)
        def _():
            cbuf[0:8, :] = jnp.zeros((8, A_CONV_DIM), f32)

        for c in range(nblk - 1):
            sl = slice(c * D_MODEL, (c + 1) * D_MODEL)
            cbuf[8:8 + tm, sl] = jnp.dot(h, wm_ref[:, sl], preferred_element_type=f32)
        sl = slice((nblk - 1) * D_MODEL, nblk * D_MODEL)
        proj_ref[:, sl] = jnp.dot(h, wm_ref[:, sl], preferred_element_type=f32)
        for c in range(A_CONV_DIM // A_DK):
            sl = slice(c * A_DK, (c + 1) * A_DK)
            acc = cbuf[5:5 + tm, sl] * wc_ref[0:1, sl]
            for j in range(1, CONV_W):
                acc = acc + cbuf[5 + j:5 + j + tm, sl] * wc_ref[j:j + 1, sl]
            a = _silu(acc)
            if c < A_HEADS:
                a = _l2n(a) * (A_DK ** -0.5)
            elif c < 2 * A_HEADS:
                a = _l2n(a)
            proj_ref[:, sl] = a
        tail = cbuf[tm:tm + 8, :]
        tail_ref[...] = tail
        cbuf[0:8, :] = tail
    else:
        for c in range(nblk):
            sl = slice(c * D_MODEL, (c + 1) * D_MODEL)
            proj_ref[:, sl] = jnp.dot(h, wm_ref[:, sl], preferred_element_type=f32)
    p = _dot3(hf, ws_ref[...])
    lane = lax.broadcasted_iota(jnp.int32, p.shape, 1)
    beta = jax.nn.sigmoid(p)
    g = -jnp.exp(alog_ref[...]) * _softplus(p + dtb_ref[...])
    is_g = (lane >= A_HEADS) & (lane < 2 * A_HEADS)
    gates = jnp.where(lane < A_HEADS, beta, jnp.where(is_g, g, 0.0))
    if cumsum:
        gsh = pltpu.roll(jnp.where(is_g, g, 0.0), A_HEADS, 1)
        ghi, gmid, glo = _split3(gsh)
        ri = lax.broadcasted_iota(jnp.int32, (CHUNK_A, CHUNK_A), 0)
        ci = lax.broadcasted_iota(jnp.int32, (CHUNK_A, CHUNK_A), 1)
        tri = jnp.where(ri >= ci, 1.0, 0.0).astype(bf16)
        d = functools.partial(jnp.dot, preferred_element_type=f32)
        for c in range(tm // CHUNK_A):
            rs = slice(c * CHUNK_A, (c + 1) * CHUNK_A)
            dc = d(tri, ghi[rs]) + d(tri, gmid[rs]) + d(tri, glo[rs])
            gates_ref[rs, :] = gates[rs] + dc
    else:
        gates_ref[...] = gates


def _a_in(x, mods, l, li, nw3, wm, ws, alog, dtb, wconv=None):
    G, R, _ = x.shape
    tm = min(R, 512)
    sub = 1
    cumsum = wconv is not None
    in_specs = [
        _tok_spec(tm, D_MODEL),
        mods.spec(l, sub, 0, tm), mods.spec(l, sub, 1, tm),
        pl.BlockSpec((None, 1, D_MODEL), lambda g, i: (l * N_SUB + sub, 0, 0)),
        _resident((None, D_MODEL, A_MAIN_DIM), lambda g, i: (li, 0, 0)),
        _resident((None, D_MODEL, LANES), lambda g, i: (li, 0, 0)),
        pl.BlockSpec((None, 1, LANES), lambda g, i: (li, 0, 0)),
        pl.BlockSpec((None, 1, LANES), lambda g, i: (li, 0, 0)),
    ]
    args = [x, mods.arr, mods.arr, nw3, wm, ws, alog, dtb]
    out_specs = [_tok_spec(tm, A_MAIN_DIM), _tok_spec(tm, LANES)]
    out_shape = [jax.ShapeDtypeStruct((G, R, A_MAIN_DIM), f32),
                 jax.ShapeDtypeStruct((G, R, LANES), f32)]
    scratch = []
    if cumsum:
        in_specs.append(pl.BlockSpec((None, CONV_W, A_CONV_DIM), lambda g, i: (li, 0, 0)))
        args.append(wconv)
        out_specs.append(pl.BlockSpec((None, 8, A_CONV_DIM), lambda g, i: (g, 0, 0)))
        out_shape.append(jax.ShapeDtypeStruct((G, 8, A_CONV_DIM), f32))
        scratch.append(pltpu.VMEM((tm + 8, A_CONV_DIM), f32))
    return pl.pallas_call(
        functools.partial(_a_in_kernel, cumsum=cumsum),
        grid=(G, R // tm),
        in_specs=in_specs,
        out_specs=out_specs,
        out_shape=out_shape,
        scratch_shapes=scratch,
        compiler_params=_params(("arbitrary", "arbitrary")),
        name="a_in",
    )(*args)


A_LT = 128


def _a_delta_kernel(qkv_ref, gcol_ref, grow_ref, o_ref, s_ref):
    t = pl.program_id(1)
    lt = qkv_ref.shape[0]
    nc = lt // CHUNK_A

    @pl.when(t == 0)
    def _():
        s_ref[...] = jnp.zeros(s_ref.shape, f32)

    gates = gcol_ref[...]
    ri = lax.broadcasted_iota(jnp.int32, (CHUNK_A, CHUNK_A), 0)
    ci = lax.broadcasted_iota(jnp.int32, (CHUNK_A, CHUNK_A), 1)
    causal = ri >= ci
    strict = ri > ci

    inst = []
    for h in range(A_HEADS):
        for c in range(nc):
            rs = slice(c * CHUNK_A, (c + 1) * CHUNK_A)
            qc = qkv_ref[rs, h * A_DK:(h + 1) * A_DK]
            kc = qkv_ref[rs, A_QK_DIM + h * A_DK:A_QK_DIM + (h + 1) * A_DK]
            vc = qkv_ref[rs, 2 * A_QK_DIM + h * A_DV:2 * A_QK_DIM + (h + 1) * A_DV]
            bcol = gates[rs, h:h + 1]
            dcol = gates[rs, 2 * A_HEADS + h:2 * A_HEADS + h + 1]
            drow = grow_ref[c, 2 * A_HEADS + h:2 * A_HEADS + h + 1, :]
            dlast = dcol[CHUNK_A - 1:CHUNK_A, :]
            gam = jnp.where(causal, jnp.exp(jnp.where(causal, dcol - drow, 0.0)), 0.0)
            kb = kc * bcol
            ed = jnp.exp(dcol)
            kt = kc.T
            kq = _bdot(jnp.concatenate([kb, qc], axis=0), kt)
            m = jnp.where(strict, kq[:CHUNK_A] * gam, 0.0)
            attn = kq[CHUNK_A:] * gam
            kdec_t = kt * jnp.exp(dlast - drow)
            inst.append(dict(
                h=h, c=c, m=m,
                rhs=jnp.concatenate([vc * bcol, kb * ed], axis=1),
                qed=(qc * ed).astype(bf16),
                lhs2=jnp.concatenate([attn, kdec_t], axis=0).astype(bf16),
                edl=jnp.exp(dlast)))

    ms = [i["m"] for i in inst]
    ts = [-m for m in ms]
    n = 2
    while n < CHUNK_A:
        ms = [_bdot(m, m) for m in ms]
        ts = [t + m + _bdot(t, m) for t, m in zip(ts, ms)]
        n *= 2
    rs_ = [-(i["m"] + t + _dot3(i["m"], t)) for i, t in zip(inst, ts)]
    ts = [t + r + _bdot(t, r) for t, r in zip(ts, rs_)]

    for i, t in zip(inst, ts):
        uw = i["rhs"] + _bdot(t, i["rhs"])
        i["u"] = uw[:, :A_DV]
        i["lhs1"] = jnp.concatenate([uw[:, A_DV:].astype(bf16), i["qed"]], axis=0)

    states = [s_ref[h] for h in range(A_HEADS)]
    for c in range(nc):
        rs = slice(c * CHUNK_A, (c + 1) * CHUNK_A)
        for h in range(A_HEADS):
            i = inst[h * nc + c]
            ws_qs = jnp.dot(i["lhs1"], states[h].astype(bf16), preferred_element_type=f32)
            v_new = i["u"] - ws_qs[:CHUNK_A]
            r = jnp.dot(i["lhs2"], v_new.astype(bf16), preferred_element_type=f32)
            o_ref[rs, h * A_DV:(h + 1) * A_DV] = ws_qs[CHUNK_A:] + r[:CHUNK_A]
            states[h] = states[h] * i["edl"] + r[CHUNK_A:]
    for h in range(A_HEADS):
        s_ref[h] = states[h]


def _a_delta(proj, gates, grow):
    B, L, _ = proj.shape
    lt = A_LT
    return pl.pallas_call(
        _a_delta_kernel,
        grid=(B, L // lt),
        in_specs=[
            pl.BlockSpec((None, lt, A_CONV_DIM), lambda b, t: (b, t, 0)),
            pl.BlockSpec((None, lt, LANES), lambda b, t: (b, t, 0)),
            pl.BlockSpec((None, lt // CHUNK_A, GATE_COLS, CHUNK_A), lambda b, t: (b, t, 0, 0)),
        ],
        out_specs=[
            pl.BlockSpec((None, lt, A_V_DIM), lambda b, t: (b, t, 0)),
            pl.BlockSpec((None, A_HEADS, A_DK, A_DV), lambda b, t: (b, 0, 0, 0)),
        ],
        out_shape=[jax.ShapeDtypeStruct((B, L, A_V_DIM), f32),
                   jax.ShapeDtypeStruct((B, A_HEADS, A_DK, A_DV), f32)],
        compiler_params=_params(("arbitrary", "arbitrary")),
        name="a_delta",
    )(proj, gates, grow)


A_BT = 8


def _a_step_kernel(pre_ref, conv_ref, gates_ref, wc_ref, s_ref, o_ref, so_ref):
    bt = pre_ref.shape[0]
    wc = wc_ref[...]
    acc = pre_ref[...] * wc[CONV_W - 1:CONV_W, :]
    for j in range(CONV_W - 1):
        acc = acc + conv_ref[:, j, :] * wc[j:j + 1, :]
    qkv = _silu(acc)
    gates = gates_ref[...]
    row = lax.broadcasted_iota(jnp.int32, (bt, 8, LANES), 1)
    for h in range(A_HEADS):
        q = _l2n(qkv[:, h * A_DK:(h + 1) * A_DK]) * (A_DK ** -0.5)
        k = _l2n(qkv[:, A_QK_DIM + h * A_DK:A_QK_DIM + (h + 1) * A_DK])
        v = qkv[:, 2 * A_QK_DIM + h * A_DV:2 * A_QK_DIM + (h + 1) * A_DV]
        beta = gates[:, h:h + 1]
        eg = jnp.exp(gates[:, A_HEADS + h:A_HEADS + h + 1])
        s = s_ref[:, h]
        kq = jnp.where(row == 0, k[:, None, :], jnp.where(row == 1, q[:, None, :], 0.0))
        ks = jnp.einsum('bmk,bkn->bmn', kq.astype(bf16), s.astype(bf16),
                        preferred_element_type=f32)
        v_new = beta * (v - eg * ks[:, 0, :])
        qk = jnp.sum(q * k, axis=-1, keepdims=True)
        o_ref[:, h * A_DV:(h + 1) * A_DV] = eg * ks[:, 1, :] + qk * v_new
        kh = k.astype(bf16).astype(f32)
        vh = v_new.astype(bf16).astype(f32)
        kl, vl = k - kh, v_new - vh
        lhs = jnp.where(row == 2, kl[:, None, :], jnp.where(row < 2, kh[:, None, :], 0.0))
        rhs = jnp.where(row == 1, vl[:, None, :], jnp.where(row < 3, vh[:, None, :], 0.0))
        outer = lax.dot_general(lhs.astype(bf16), rhs.astype(bf16), (((1,), (1,)), ((0,), (0,))),
                                preferred_element_type=f32)
        so_ref[:, h] = s * eg[:, :, None] + outer


def _a_step(proj, conv_state, gates, wconv, s0, li):
    nb = proj.shape[1]
    bt = A_BT
    return pl.pallas_call(
        _a_step_kernel,
        grid=(nb // bt,),
        in_specs=[
            pl.BlockSpec((None, bt, A_CONV_DIM), lambda i: (0, i, 0)),
            pl.BlockSpec((None, bt, CONV_W - 1, A_CONV_DIM), lambda i: (li, i, 0, 0)),
            pl.BlockSpec((None, bt, LANES), lambda i: (0, i, 0)),
            pl.BlockSpec((None, CONV_W, A_CONV_DIM), lambda i: (li, 0, 0)),
            pl.BlockSpec((None, bt, A_HEADS, A_DK, A_DV), lambda i: (li, i, 0, 0, 0)),
        ],
        out_specs=[
            pl.BlockSpec((None, bt, A_V_DIM), lambda i: (0, i, 0)),
            pl.BlockSpec((bt, A_HEADS, A_DK, A_DV), lambda i: (i, 0, 0, 0)),
        ],
        out_shape=[jax.ShapeDtypeStruct((1, nb, A_V_DIM), f32),
                   jax.ShapeDtypeStruct(s0.shape[1:], f32)],
        compiler_params=_params(("arbitrary",)),
        name="a_step",
    )(proj, conv_state, gates, wconv, s0)


def _a_out_kernel(x_ref, o_ref, z_ref, gt_ref, won_ref, wo_ref, y_ref, og_ref):
    won = won_ref[...]
    for h in range(A_HEADS):
        sl = slice(h * A_DV, (h + 1) * A_DV)
        og_ref[:, sl] = (_rms(o_ref[:, sl]) * won * _silu(z_ref[:, sl])).astype(bf16)
    y = jnp.dot(og_ref[...], wo_ref[...], preferred_element_type=f32)
    y_ref[...] = x_ref[...] + (1.0 + gt_ref[...]) * y


def _a_out(x, o, proj, mods, l, li, won, wo):
    G, R, _ = x.shape
    tm = min(R, 512)
    return pl.pallas_call(
        _a_out_kernel,
        grid=(G, R // tm),
        in_specs=[
            _tok_spec(tm, D_MODEL),
            _tok_spec(tm, A_V_DIM),
            _tok_spec(tm, A_V_DIM, A_CONV_DIM // A_V_DIM),
            mods.spec(l, 1, 2, tm),
            pl.BlockSpec((None, 1, A_DV), lambda g, i: (li, 0, 0)),
            _resident((None, A_V_DIM, D_MODEL), lambda g, i: (li, 0, 0)),
        ],
        out_specs=_tok_spec(tm, D_MODEL),
        out_shape=jax.ShapeDtypeStruct(x.shape, f32),
        scratch_shapes=[pltpu.VMEM((tm, A_V_DIM), bf16)],
        compiler_params=_params(("arbitrary", "arbitrary")),
        name="a_out",
    )(x, o, proj, mods.arr, won, wo)


def _b_kernel(x_ref, sh_ref, sc_ref, gt_ref, nw_ref, wi_ref, vw_ref, vb_ref, ws_ref, bs_ref, wo_ref,
              *rest, decode):
    if decode:
        y_ref, vout_ref, vn_ref, gd_ref = rest
    else:
        y_ref, vn_ref, gd_ref = rest
    tm = x_ref.shape[0]
    x = x_ref[...]
    h = _norm_mod(x, nw_ref[...], sc_ref[...], sh_ref[...]).astype(bf16)
    v = jax.nn.gelu(jnp.dot(h, wi_ref[:, B_WIDTH:], preferred_element_type=f32))
    mu = jnp.mean(v, axis=-1, keepdims=True)
    vc = v - mu
    vn = vc * lax.rsqrt(jnp.mean(vc * vc, axis=-1, keepdims=True) + EPS) * vw_ref[...] + vb_ref[...]
    if decode:
        vout_ref[...] = vn
    else:
        vn_ref[...] = vn.astype(bf16)
        ri = lax.broadcasted_iota(jnp.int32, (CHUNK_B, CHUNK_B), 0)
        ci = lax.broadcasted_iota(jnp.int32, (CHUNK_B, CHUNK_B), 1)
        causal = ri >= ci
    for g in range(B_GROUPS):
        gs = slice(g * B_GROUP_DIM, (g + 1) * B_GROUP_DIM)
        u = jax.nn.gelu(jnp.dot(h, wi_ref[:, gs], preferred_element_type=f32))
        if decode:
            gd_ref[:, gs] = (u * (ws_ref[:, gs] * vn[:, gs] + bs_ref[:, gs])).astype(bf16)
        else:
            wsg = jnp.where(causal, ws_ref[g], 0.0).astype(bf16)
            bias = bs_ref[:, g:g + 1]
            for c in range(tm // CHUNK_B):
                rs = slice(c * CHUNK_B, (c + 1) * CHUNK_B)
                mixed = jnp.dot(wsg, vn_ref[rs, gs], preferred_element_type=f32) + bias
                gd_ref[rs, gs] = (u[rs] * mixed).astype(bf16)
    y = jnp.dot(gd_ref[...], wo_ref[...], preferred_element_type=f32)
    y_ref[...] = x + (1.0 + gt_ref[...]) * y


def _b_mixer(x, mods, l, li, nw3, wi, vw, vb, ws, bs, wo, decode):
    G, R, _ = x.shape
    tm = min(R, 256)
    sub = 1
    if decode:
        ws_spec = pl.BlockSpec((None, 1, B_WIDTH), lambda g, i: (li, 0, 0))
        bs_spec = pl.BlockSpec((None, 1, B_WIDTH), lambda g, i: (li, 0, 0))
        out_specs = [_tok_spec(tm, D_MODEL), _tok_spec(tm, B_WIDTH)]
        out_shape = [jax.ShapeDtypeStruct(x.shape, f32), jax.ShapeDtypeStruct((G, R, B_WIDTH), f32)]
    else:
        ws_spec = pl.BlockSpec((None, B_GROUPS, CHUNK_B, CHUNK_B), lambda g, i: (li, 0, 0, 0))
        bs_spec = pl.BlockSpec((None, CHUNK_B, B_GROUPS), lambda g, i: (li, 0, 0))
        out_specs = _tok_spec(tm, D_MODEL)
        out_shape = jax.ShapeDtypeStruct(x.shape, f32)
    return pl.pallas_call(
        functools.partial(_b_kernel, decode=decode),
        grid=(G, R // tm),
        in_specs=[
            _tok_spec(tm, D_MODEL),
            mods.spec(l, sub, 0, tm), mods.spec(l, sub, 1, tm), mods.spec(l, sub, 2, tm),
            pl.BlockSpec((None, 1, D_MODEL), lambda g, i: (l * N_SUB + sub, 0, 0)),
            _resident((None, D_MODEL, 2 * B_WIDTH), lambda g, i: (li, 0, 0)),
            pl.BlockSpec((None, 1, B_WIDTH), lambda g, i: (li, 0, 0)),
            pl.BlockSpec((None, 1, B_WIDTH), lambda g, i: (li, 0, 0)),
            ws_spec, bs_spec,
            _resident((None, B_WIDTH, D_MODEL), lambda g, i: (li, 0, 0)),
        ],
        out_specs=out_specs,
        out_shape=out_shape,
        scratch_shapes=[pltpu.VMEM((tm, B_WIDTH), bf16), pltpu.VMEM((tm, B_WIDTH), bf16)],
        compiler_params=_params(("arbitrary", "arbitrary")),
        name="b_mixer",
    )(x, mods.arr, mods.arr, mods.arr, nw3, wi, vw, vb, ws, bs, wo)


def _trunk(x, mods, decode, conv_state, s_state, W):
    new_conv, new_s, v_rows = [], [], []
    G, R, _ = x.shape
    for i in range(DEPTH):
        li = i // N_MIXERS
        x = _ffn(x, mods, i, 0, 0, W["nw3"], W["wgu"], W["wdn"])
        if i % N_MIXERS == 0:
            if decode:
                proj, gates = _a_in(x, mods, i, li, W["nw3"], W["a_wm"], W["a_ws"], W["a_alog"], W["a_dtb"])
                o, s_new = _a_step(proj, conv_state, gates, W["a_wconv"], s_state, li)
                pre = proj[0, :, None, :A_CONV_DIM]
                new_conv.append(jnp.concatenate([conv_state[li][:, 1:], pre], axis=1))
            else:
                proj, gates, tail = _a_in(x, mods, i, li, W["nw3"], W["a_wm"], W["a_ws"], W["a_alog"],
                                          W["a_dtb"], wconv=W["a_wconv"])
                grow = gates[..., :GATE_COLS].reshape(G, R // CHUNK_A, CHUNK_A, GATE_COLS)
                grow = jnp.swapaxes(grow, 2, 3)
                o, s_new = _a_delta(proj, gates, grow)
                new_conv.append(tail[:, 8 - (CONV_W - 1):])
            new_s.append(s_new)
            x = _a_out(x, o, proj, mods, i, li, W["a_won"], W["a_wo"])
        else:
            if decode:
                x, v = _b_mixer(x, mods, i, li, W["nw3"], W["b_wi"], W["b_vw"], W["b_vb"],
                                W["b_ws00"], W["b_bs0"], W["b_wo"], decode=True)
                v_rows.append(v)
            else:
                x = _b_mixer(x, mods, i, li, W["nw3"], W["b_wi"], W["b_vw"], W["b_vb"],
                             W["b_ws"], W["b_bsT"], W["b_wo"], decode=False)
        fw = W["final_w"] if i == DEPTH - 1 else None
        x = _ffn(x, mods, i, 1, 2, W["nw3"], W["wgu"], W["wdn"], final_w=fw)
    return x, new_conv, new_s, v_rows


def kernel(x_prompt, x_sample, state_a_conv, state_a_S, c_prompt, c_sample, w_ada, b_ada, norm_w, ffn_w_gu, ffn_w_down, a_w_in, a_w_conv, a_log, a_dt_bias, a_w_onorm, a_w_out, b_w_in, b_vnorm_w, b_vnorm_b, b_w_s, b_b_s, b_w_out, final_norm_w):
    n_a = a_w_in.shape[0]
    n_b = b_w_in.shape[0]
    bp, seq, _ = x_prompt.shape
    bs = x_sample.shape[0]
    assert x_sample.shape[1] == 1 and seq % A_LT == 0 and seq % CHUNK_B == 0

    mod_p, mod_s = _ada(c_prompt, c_sample, w_ada, b_ada)
    mods_p = _Mods(mod_p.reshape(DEPTH, bp, 1, N_SUB * 3 * D_MODEL), per_row=False)
    mods_s = _Mods(mod_s.reshape(DEPTH, 1, bs, N_SUB * 3 * D_MODEL), per_row=True)

    pad = LANES - 2 * A_HEADS
    gate_pad = lambda a: jnp.pad(a, ((0, 0), (A_HEADS, pad)))[:, None, :]
    W = {
        "nw3": norm_w.reshape(DEPTH * N_SUB, 1, D_MODEL),
        "wgu": ffn_w_gu.astype(bf16),
        "wdn": ffn_w_down.astype(bf16),
        "a_wm": a_w_in.astype(bf16),
        "a_ws": jnp.pad(a_w_in[:, :, A_MAIN_DIM:], ((0, 0), (0, 0), (0, pad))),
        "a_alog": gate_pad(a_log),
        "a_dtb": gate_pad(a_dt_bias),
        "a_wconv": a_w_conv,
        "a_won": a_w_onorm[:, None, :],
        "a_wo": a_w_out.astype(bf16),
        "b_wi": b_w_in.astype(bf16),
        "b_vw": b_vnorm_w[:, None, :],
        "b_vb": b_vnorm_b[:, None, :],
        "b_ws": b_w_s,
        "b_bsT": jnp.swapaxes(b_b_s, 1, 2),
        "b_ws00": jnp.repeat(b_w_s[:, :, 0, 0], B_GROUP_DIM, axis=1)[:, None, :],
        "b_bs0": jnp.repeat(b_b_s[:, :, 0], B_GROUP_DIM, axis=1)[:, None, :],
        "b_wo": b_w_out.astype(bf16),
        "final_w": final_norm_w,
    }

    y_p, conv_p, s_p, _ = _trunk(x_prompt, mods_p, False, None, None, W)
    y_s, conv_s, s_s, v_s = _trunk(x_sample.reshape(1, bs, D_MODEL), mods_s, True,
                                   state_a_conv, state_a_S, W)
    return (y_p,
            y_s.reshape(bs, 1, D_MODEL),
            jnp.stack(conv_p),
            jnp.stack(s_p),
            jnp.stack(conv_s),
            jnp.stack(s_s),
            jnp.stack(v_s).reshape(n_b, bs, 1, B_WIDTH))
```

```python
import functools

import jax
import jax.numpy as jnp
from jax import lax
from jax.experimental import pallas as pl
from jax.experimental.pallas import tpu as pltpu

f32 = jnp.float32
bf16 = jnp.bfloat16

D_MODEL = 1024
DEPTH = 4
N_MIXERS = 2
A_HEADS = 8
A_DK = 128
A_DV = 128
A_QK_DIM = A_HEADS * A_DK
A_V_DIM = A_HEADS * A_DV
A_CONV_DIM = 2 * A_QK_DIM + A_V_DIM
A_MAIN_DIM = A_CONV_DIM + A_V_DIM
CONV_W = 4
CHUNK_A = 64
B_WIDTH = 2 * D_MODEL
B_GROUPS = 8
B_GROUP_DIM = B_WIDTH // B_GROUPS
CHUNK_B = 128
D_FF = 2816
FFN_RES = 0.5
N_SUB = 3
EPS = 1e-6

LANES = 128
FF_TILE = 256
GATE_COLS = 3 * A_HEADS
VMEM_LIMIT = 52 * 1024 * 1024


def _params(sem):
    return pltpu.CompilerParams(dimension_semantics=sem, vmem_limit_bytes=VMEM_LIMIT)


def _resident(shape, index_map):
    return pl.BlockSpec(shape, index_map, pipeline_mode=pl.Buffered(1))


def _bdot(a, b):
    return jnp.dot(a.astype(bf16), b.astype(bf16), preferred_element_type=f32)


def _bdot_nt(a, b):
    return lax.dot_general(a.astype(bf16), b.astype(bf16), (((1,), (1,)), ((), ())),
                           preferred_element_type=f32)


def _bdot_tn(a, b):
    return lax.dot_general(a.astype(bf16), b.astype(bf16), (((0,), (0,)), ((), ())),
                           preferred_element_type=f32)


def _split2(a):
    hi = a.astype(bf16)
    lo = (a - hi.astype(f32)).astype(bf16)
    return hi, lo


def _split3(a):
    hi = a.astype(bf16)
    r = a - hi.astype(f32)
    mid = r.astype(bf16)
    lo = (r - mid.astype(f32)).astype(bf16)
    return hi, mid, lo


def _dot3(a, b):
    ah, al = _split2(a)
    bh, bl = _split2(b)
    d = functools.partial(jnp.dot, preferred_element_type=f32)
    return d(ah, bh) + d(al, bh) + d(ah, bl)


def _silu(x):
    return x * jax.nn.sigmoid(x)


def _softplus(x):
    return jnp.maximum(x, 0.0) + jnp.log1p(jnp.exp(-jnp.abs(x)))


GELU_A = 2.0 * (2.0 / 3.141592653589793) ** 0.5
GELU_B = GELU_A * 0.044715


def _gelu(x):
    return x * jax.nn.sigmoid(x * (GELU_A + GELU_B * (x * x)))


def _rms(x):
    return x * lax.rsqrt(jnp.mean(x * x, axis=-1, keepdims=True) + EPS)


def _norm_mod(x, nw, sc, sh):
    return (_rms(x) * nw) * (1.0 + sc) + sh


def _l2n(x):
    return x * lax.rsqrt(jnp.sum(x * x, axis=-1, keepdims=True) + EPS)


def _ada_kernel(cp_ref, cs_ref, w_ref, b_ref, op_ref, os_ref):
    w = w_ref[...].astype(bf16)
    b = b_ref[...]
    op_ref[...] = _bdot(_silu(cp_ref[...]), w) + b
    os_ref[...] = _bdot(_silu(cs_ref[...]), w) + b


def _ada(c_p, c_s, w_ada, b_ada):
    n = N_SUB * 3 * D_MODEL
    tn = 9 * LANES
    bp, bs = c_p.shape[0], c_s.shape[0]
    return pl.pallas_call(
        _ada_kernel,
        grid=(DEPTH, n // tn),
        in_specs=[
            pl.BlockSpec((bp, D_MODEL), lambda l, j: (0, 0)),
            pl.BlockSpec((bs, D_MODEL), lambda l, j: (0, 0)),
            pl.BlockSpec((None, D_MODEL, tn), lambda l, j: (l, 0, j)),
            pl.BlockSpec((None, 1, tn), lambda l, j: (l, 0, j)),
        ],
        out_specs=[
            pl.BlockSpec((None, bp, tn), lambda l, j: (l, 0, j)),
            pl.BlockSpec((None, bs, tn), lambda l, j: (l, 0, j)),
        ],
        out_shape=[jax.ShapeDtypeStruct((DEPTH, bp, n), f32),
                   jax.ShapeDtypeStruct((DEPTH, bs, n), f32)],
        compiler_params=_params(("arbitrary", "arbitrary")),
        name="ada",
    )(c_p, c_s, w_ada, b_ada.reshape(DEPTH, 1, n))


class _Mods:
    def __init__(self, arr, per_row):
        self.arr = arr
        self.per_row = per_row

    def spec(self, l, sub, kind, tm):
        col = sub * 3 + kind
        if self.per_row:
            return pl.BlockSpec((None, None, tm, D_MODEL), lambda g, i: (l, g, i, col))
        return pl.BlockSpec((None, None, 1, D_MODEL), lambda g, i: (l, g, 0, col))


def _tok_spec(tm, width, colblk=0):
    return pl.BlockSpec((None, tm, width), lambda g, i: (g, i, colblk))


def _ffn_kernel(x_ref, sh_ref, sc_ref, gt_ref, nw_ref, wg_ref, wu_ref, wd_ref, *rest, final):
    if final:
        fw_ref, o_ref, act_ref = rest
    else:
        o_ref, act_ref = rest
    x = x_ref[...]
    h = _norm_mod(x, nw_ref[...], sc_ref[...], sh_ref[...]).astype(bf16)
    for c in range(D_FF // FF_TILE):
        sl = slice(c * FF_TILE, (c + 1) * FF_TILE)
        g = jnp.dot(h, wg_ref[:, sl], preferred_element_type=f32)
        u = jnp.dot(h, wu_ref[:, sl], preferred_element_type=f32)
        act_ref[:, sl] = (_silu(g) * u).astype(bf16)
    y = jnp.dot(act_ref[...], wd_ref[...], preferred_element_type=f32)
    out = x + FFN_RES * (1.0 + gt_ref[...]) * y
    if final:
        out = _rms(out) * fw_ref[...]
    o_ref[...] = out


def _ffn(x, mods, l, j, sub, nw3, wgu, wdn, final_w=None):
    G, R, _ = x.shape
    tm = min(R, 512)
    final = final_w is not None
    in_specs = [
        _tok_spec(tm, D_MODEL),
        mods.spec(l, sub, 0, tm), mods.spec(l, sub, 1, tm), mods.spec(l, sub, 2, tm),
        pl.BlockSpec((None, 1, D_MODEL), lambda g, i: (l * N_SUB + sub, 0, 0)),
        _resident((None, None, D_MODEL, D_FF), lambda g, i: (l, j, 0, 0)),
        _resident((None, None, D_MODEL, D_FF), lambda g, i: (l, j, 0, 1)),
        _resident((None, None, D_FF, D_MODEL), lambda g, i: (l, j, 0, 0)),
    ]
    args = [x, mods.arr, mods.arr, mods.arr, nw3, wgu, wgu, wdn]
    if final:
        in_specs.append(pl.BlockSpec((1, D_MODEL), lambda g, i: (0, 0)))
        args.append(final_w.reshape(1, D_MODEL))
    return pl.pallas_call(
        functools.partial(_ffn_kernel, final=final),
        grid=(G, R // tm),
        in_specs=in_specs,
        out_specs=_tok_spec(tm, D_MODEL),
        out_shape=jax.ShapeDtypeStruct(x.shape, f32),
        scratch_shapes=[pltpu.VMEM((tm, D_FF), bf16)],
        compiler_params=_params(("arbitrary", "arbitrary")),
        name="ffn",
    )(*args)


def _a_in_kernel(x_ref, sh_ref, sc_ref, nw_ref, wm_ref, ws_ref, alog_ref, dtb_ref, *rest, cumsum):
    if cumsum:
        wc_ref, proj_ref, gates_ref, tail_ref, cbuf = rest
    else:
        proj_ref, gates_ref = rest
    tm = x_ref.shape[0]
    hf = _norm_mod(x_ref[...], nw_ref[...], sc_ref[...], sh_ref[...])
    h = hf.astype(bf16)
    nblk = A_MAIN_DIM // D_MODEL
    if cumsum:
        @pl.when(pl.program_id(1) == 0)
        def _():
            cbuf[0:8, :] = jnp.zeros((8, A_CONV_DIM), f32)

        for c in range(nblk - 1):
            sl = slice(c * D_MODEL, (c + 1) * D_MODEL)
            cbuf[8:8 + tm, sl] = jnp.dot(h, wm_ref[:, sl], preferred_element_type=f32)
        sl = slice((nblk - 1) * D_MODEL, nblk * D_MODEL)
        proj_ref[:, sl] = jnp.dot(h, wm_ref[:, sl], preferred_element_type=f32)
        for c in range(A_CONV_DIM // A_DK):
            sl = slice(c * A_DK, (c + 1) * A_DK)
            acc = cbuf[5:5 + tm, sl] * wc_ref[0:1, sl]
            for j in range(1, CONV_W):
                acc = acc + cbuf[5 + j:5 + j + tm, sl] * wc_ref[j:j + 1, sl]
            a = _silu(acc)
            if c < A_HEADS:
                a = _l2n(a) * (A_DK ** -0.5)
            elif c < 2 * A_HEADS:
                a = _l2n(a)
            proj_ref[:, sl] = a
        tail = cbuf[tm:tm + 8, :]
        tail_ref[...] = tail
        cbuf[0:8, :] = tail
    else:
        for c in range(nblk):
            sl = slice(c * D_MODEL, (c + 1) * D_MODEL)
            proj_ref[:, sl] = jnp.dot(h, wm_ref[:, sl], preferred_element_type=f32)
    p = _dot3(hf, ws_ref[...])
    lane = lax.broadcasted_iota(jnp.int32, p.shape, 1)
    beta = jax.nn.sigmoid(p)
    g = -jnp.exp(alog_ref[...]) * _softplus(p + dtb_ref[...])
    is_g = (lane >= A_HEADS) & (lane < 2 * A_HEADS)
    gates = jnp.where(lane < A_HEADS, beta, jnp.where(is_g, g, 0.0))
    if cumsum:
        gsh = pltpu.roll(jnp.where(is_g, g, 0.0), A_HEADS, 1)
        ghi, gmid, glo = _split3(gsh)
        ri = lax.broadcasted_iota(jnp.int32, (CHUNK_A, CHUNK_A), 0)
        ci = lax.broadcasted_iota(jnp.int32, (CHUNK_A, CHUNK_A), 1)
        tri = jnp.where(ri >= ci, 1.0, 0.0).astype(bf16)
        d = functools.partial(jnp.dot, preferred_element_type=f32)
        for c in range(tm // CHUNK_A):
            rs = slice(c * CHUNK_A, (c + 1) * CHUNK_A)
            dc = d(tri, ghi[rs]) + d(tri, gmid[rs]) + d(tri, glo[rs])
            gates_ref[rs, :] = gates[rs] + dc
    else:
        gates_ref[...] = gates


def _a_in(x, mods, l, li, nw3, wm, ws, alog, dtb, wconv=None):
    G, R, _ = x.shape
    tm = min(R, 512)
    sub = 1
    cumsum = wconv is not None
    in_specs = [
        _tok_spec(tm, D_MODEL),
        mods.spec(l, sub, 0, tm), mods.spec(l, sub, 1, tm),
        pl.BlockSpec((None, 1, D_MODEL), lambda g, i: (l * N_SUB + sub, 0, 0)),
        _resident((None, D_MODEL, A_MAIN_DIM), lambda g, i: (li, 0, 0)),
        _resident((None, D_MODEL, LANES), lambda g, i: (li, 0, 0)),
        pl.BlockSpec((None, 1, LANES), lambda g, i: (li, 0, 0)),
        pl.BlockSpec((None, 1, LANES), lambda g, i: (li, 0, 0)),
    ]
    args = [x, mods.arr, mods.arr, nw3, wm, ws, alog, dtb]
    out_specs = [_tok_spec(tm, A_MAIN_DIM), _tok_spec(tm, LANES)]
    out_shape = [jax.ShapeDtypeStruct((G, R, A_MAIN_DIM), f32),
                 jax.ShapeDtypeStruct((G, R, LANES), f32)]
    scratch = []
    if cumsum:
        in_specs.append(pl.BlockSpec((None, CONV_W, A_CONV_DIM), lambda g, i: (li, 0, 0)))
        args.append(wconv)
        out_specs.append(pl.BlockSpec((None, 8, A_CONV_DIM), lambda g, i: (g, 0, 0)))
        out_shape.append(jax.ShapeDtypeStruct((G, 8, A_CONV_DIM), f32))
        scratch.append(pltpu.VMEM((tm + 8, A_CONV_DIM), f32))
    return pl.pallas_call(
        functools.partial(_a_in_kernel, cumsum=cumsum),
        grid=(G, R // tm),
        in_specs=in_specs,
        out_specs=out_specs,
        out_shape=out_shape,
        scratch_shapes=scratch,
        compiler_params=_params(("arbitrary", "arbitrary")),
        name="a_in",
    )(*args)


A_LT = 256


def _a_delta_kernel(qkv_ref, gcol_ref, grow_ref, o_ref, s_ref):
    t = pl.program_id(1)
    lt = qkv_ref.shape[0]
    nc = lt // CHUNK_A

    @pl.when(t == 0)
    def _():
        s_ref[...] = jnp.zeros(s_ref.shape, f32)

    gates = gcol_ref[...]
    ri = lax.broadcasted_iota(jnp.int32, (CHUNK_A, CHUNK_A), 0)
    ci = lax.broadcasted_iota(jnp.int32, (CHUNK_A, CHUNK_A), 1)
    causal = ri >= ci
    strict = ri > ci

    inst = []
    for h in range(A_HEADS):
        for c in range(nc):
            rs = slice(c * CHUNK_A, (c + 1) * CHUNK_A)
            qc = qkv_ref[rs, h * A_DK:(h + 1) * A_DK]
            kc = qkv_ref[rs, A_QK_DIM + h * A_DK:A_QK_DIM + (h + 1) * A_DK]
            vc = qkv_ref[rs, 2 * A_QK_DIM + h * A_DV:2 * A_QK_DIM + (h + 1) * A_DV]
            bcol = gates[rs, h:h + 1]
            dcol = gates[rs, 2 * A_HEADS + h:2 * A_HEADS + h + 1]
            drow = grow_ref[c, 2 * A_HEADS + h:2 * A_HEADS + h + 1, :]
            dlast = dcol[CHUNK_A - 1:CHUNK_A, :]
            gam = jnp.where(causal, jnp.exp(jnp.where(causal, dcol - drow, 0.0)), 0.0)
            kb = kc * bcol
            ed = jnp.exp(dcol)
            kt = kc.T
            kq = _bdot(jnp.concatenate([kb, qc], axis=0), kt)
            m = jnp.where(strict, kq[:CHUNK_A] * gam, 0.0)
            attn = kq[CHUNK_A:] * gam
            kdec_t = kt * jnp.exp(dlast - drow)
            inst.append(dict(
                h=h, c=c, m=m,
                rhs=jnp.concatenate([vc * bcol, kb * ed], axis=1),
                qed=(qc * ed).astype(bf16),
                lhs2=jnp.concatenate([attn, kdec_t], axis=0).astype(bf16),
                edl=jnp.exp(dlast)))

    ms = [i["m"] for i in inst]
    ts = [-m for m in ms]
    n = 2
    while n < CHUNK_A:
        ms = [_bdot(m, m) for m in ms]
        ts = [t + m + _bdot(t, m) for t, m in zip(ts, ms)]
        n *= 2
    rs_ = [-(i["m"] + t + _dot3(i["m"], t)) for i, t in zip(inst, ts)]
    ts = [t + r + _bdot(t, r) for t, r in zip(ts, rs_)]

    for i, t in zip(inst, ts):
        uw = i["rhs"] + _bdot(t, i["rhs"])
        i["u"] = uw[:, :A_DV]
        i["lhs1"] = jnp.concatenate([uw[:, A_DV:].astype(bf16), i["qed"]], axis=0)

    states = [s_ref[h] for h in range(A_HEADS)]
    for c in range(nc):
        rs = slice(c * CHUNK_A, (c + 1) * CHUNK_A)
        for h in range(A_HEADS):
            i = inst[h * nc + c]
            ws_qs = jnp.dot(i["lhs1"], states[h].astype(bf16), preferred_element_type=f32)
            v_new = i["u"] - ws_qs[:CHUNK_A]
            r = jnp.dot(i["lhs2"], v_new.astype(bf16), preferred_element_type=f32)
            o_ref[rs, h * A_DV:(h + 1) * A_DV] = ws_qs[CHUNK_A:] + r[:CHUNK_A]
            states[h] = states[h] * i["edl"] + r[CHUNK_A:]
    for h in range(A_HEADS):
        s_ref[h] = states[h]


def _a_delta(proj, gates, grow):
    B, L, _ = proj.shape
    lt = A_LT
    return pl.pallas_call(
        _a_delta_kernel,
        grid=(B, L // lt),
        in_specs=[
            pl.BlockSpec((None, lt, A_CONV_DIM), lambda b, t: (b, t, 0)),
            pl.BlockSpec((None, lt, LANES), lambda b, t: (b, t, 0)),
            pl.BlockSpec((None, lt // CHUNK_A, GATE_COLS, CHUNK_A), lambda b, t: (b, t, 0, 0)),
        ],
        out_specs=[
            pl.BlockSpec((None, lt, A_V_DIM), lambda b, t: (b, t, 0)),
            pl.BlockSpec((None, A_HEADS, A_DK, A_DV), lambda b, t: (b, 0, 0, 0)),
        ],
        out_shape=[jax.ShapeDtypeStruct((B, L, A_V_DIM), f32),
                   jax.ShapeDtypeStruct((B, A_HEADS, A_DK, A_DV), f32)],
        compiler_params=_params(("arbitrary", "arbitrary")),
        name="a_delta",
    )(proj, gates, grow)


A_BT = 8


def _a_step_kernel(pre_ref, conv_ref, gates_ref, wc_ref, s_ref, *rest, li):
    if li:
        prev_ref, o_ref, so_all_ref = rest
        so_all_ref[0:li] = prev_ref[...]
    else:
        o_ref, so_all_ref = rest
    so_ref = so_all_ref.at[li]
    bt = pre_ref.shape[0]
    wc = wc_ref[...]
    acc = pre_ref[...] * wc[CONV_W - 1:CONV_W, :]
    for j in range(CONV_W - 1):
        acc = acc + conv_ref[:, j, :] * wc[j:j + 1, :]
    qkv = _silu(acc)
    gates = gates_ref[...]
    row = lax.broadcasted_iota(jnp.int32, (bt, 8, LANES), 1)
    for h in range(A_HEADS):
        q = _l2n(qkv[:, h * A_DK:(h + 1) * A_DK]) * (A_DK ** -0.5)
        k = _l2n(qkv[:, A_QK_DIM + h * A_DK:A_QK_DIM + (h + 1) * A_DK])
        v = qkv[:, 2 * A_QK_DIM + h * A_DV:2 * A_QK_DIM + (h + 1) * A_DV]
        beta = gates[:, h:h + 1]
        eg = jnp.exp(gates[:, A_HEADS + h:A_HEADS + h + 1])
        s = s_ref[:, h]
        kq = jnp.where(row == 0, k[:, None, :], jnp.where(row == 1, q[:, None, :], 0.0))
        ks = jnp.einsum('bmk,bkn->bmn', kq.astype(bf16), s.astype(bf16),
                        preferred_element_type=f32)
        v_new = beta * (v - eg * ks[:, 0, :])
        qk = jnp.sum(q * k, axis=-1, keepdims=True)
        o_ref[:, h * A_DV:(h + 1) * A_DV] = eg * ks[:, 1, :] + qk * v_new
        kh = k.astype(bf16).astype(f32)
        vh = v_new.astype(bf16).astype(f32)
        kl, vl = k - kh, v_new - vh
        lhs = jnp.where(row == 2, kl[:, None, :], jnp.where(row < 2, kh[:, None, :], 0.0))
        rhs = jnp.where(row == 1, vl[:, None, :], jnp.where(row < 3, vh[:, None, :], 0.0))
        outer = lax.dot_general(lhs.astype(bf16), rhs.astype(bf16), (((1,), (1,)), ((0,), (0,))),
                                preferred_element_type=f32)
        so_ref[:, h] = s * eg[:, :, None] + outer


def _a_step(proj, conv_state, gates, wconv, s0, li, s_prev):
    nb = proj.shape[1]
    bt = A_BT
    in_specs = [
        pl.BlockSpec((None, bt, A_CONV_DIM), lambda i: (0, i, 0)),
        pl.BlockSpec((None, bt, CONV_W - 1, A_CONV_DIM), lambda i: (li, i, 0, 0)),
        pl.BlockSpec((None, bt, LANES), lambda i: (0, i, 0)),
        pl.BlockSpec((None, CONV_W, A_CONV_DIM), lambda i: (li, 0, 0)),
        pl.BlockSpec((None, bt, A_HEADS, A_DK, A_DV), lambda i: (li, i, 0, 0, 0)),
    ]
    args = [proj, conv_state, gates, wconv, s0]
    if li:
        in_specs.append(pl.BlockSpec((li, bt, A_HEADS, A_DK, A_DV), lambda i: (0, i, 0, 0, 0)))
        args.append(s_prev)
    return pl.pallas_call(
        functools.partial(_a_step_kernel, li=li),
        grid=(nb // bt,),
        in_specs=in_specs,
        out_specs=[
            pl.BlockSpec((None, bt, A_V_DIM), lambda i: (0, i, 0)),
            pl.BlockSpec((li + 1, bt, A_HEADS, A_DK, A_DV), lambda i: (0, i, 0, 0, 0)),
        ],
        out_shape=[jax.ShapeDtypeStruct((1, nb, A_V_DIM), f32),
                   jax.ShapeDtypeStruct((li + 1,) + s0.shape[1:], f32)],
        compiler_params=_params(("arbitrary",)),
        name="a_step",
    )(*args)


def _a_out_kernel(x_ref, o_ref, z_ref, gt_ref, won_ref, wo_ref, y_ref, og_ref):
    won = won_ref[...]
    for h in range(A_HEADS):
        sl = slice(h * A_DV, (h + 1) * A_DV)
        og_ref[:, sl] = (_rms(o_ref[:, sl]) * won * _silu(z_ref[:, sl])).astype(bf16)
    y = jnp.dot(og_ref[...], wo_ref[...], preferred_element_type=f32)
    y_ref[...] = x_ref[...] + (1.0 + gt_ref[...]) * y


def _a_out(x, o, proj, mods, l, li, won, wo):
    G, R, _ = x.shape
    tm = min(R, 512)
    return pl.pallas_call(
        _a_out_kernel,
        grid=(G, R // tm),
        in_specs=[
            _tok_spec(tm, D_MODEL),
            _tok_spec(tm, A_V_DIM),
            _tok_spec(tm, A_V_DIM, A_CONV_DIM // A_V_DIM),
            mods.spec(l, 1, 2, tm),
            pl.BlockSpec((None, 1, A_DV), lambda g, i: (li, 0, 0)),
            _resident((None, A_V_DIM, D_MODEL), lambda g, i: (li, 0, 0)),
        ],
        out_specs=_tok_spec(tm, D_MODEL),
        out_shape=jax.ShapeDtypeStruct(x.shape, f32),
        scratch_shapes=[pltpu.VMEM((tm, A_V_DIM), bf16)],
        compiler_params=_params(("arbitrary", "arbitrary")),
        name="a_out",
    )(x, o, proj, mods.arr, won, wo)


def _b_kernel(x_ref, sh_ref, sc_ref, gt_ref, nw_ref, wi_ref, vw_ref, vb_ref, ws_ref, bs_ref, wo_ref,
              *rest, decode):
    if decode:
        y_ref, vout_ref, vn_ref, gd_ref = rest
    else:
        y_ref, vn_ref, gd_ref = rest
    tm = x_ref.shape[0]
    x = x_ref[...]
    h = _norm_mod(x, nw_ref[...], sc_ref[...], sh_ref[...]).astype(bf16)
    v = _gelu(jnp.dot(h, wi_ref[:, B_WIDTH:], preferred_element_type=f32))
    mu = jnp.mean(v, axis=-1, keepdims=True)
    vc = v - mu
    vn = vc * lax.rsqrt(jnp.mean(vc * vc, axis=-1, keepdims=True) + EPS) * vw_ref[...] + vb_ref[...]
    if decode:
        vout_ref[...] = vn
    else:
        vn_ref[...] = vn.astype(bf16)
        ri = lax.broadcasted_iota(jnp.int32, (CHUNK_B, CHUNK_B), 0)
        ci = lax.broadcasted_iota(jnp.int32, (CHUNK_B, CHUNK_B), 1)
        causal = ri >= ci
    for g in range(B_GROUPS):
        gs = slice(g * B_GROUP_DIM, (g + 1) * B_GROUP_DIM)
        u = _gelu(jnp.dot(h, wi_ref[:, gs], preferred_element_type=f32))
        if decode:
            gd_ref[:, gs] = (u * (ws_ref[:, gs] * vn[:, gs] + bs_ref[:, gs])).astype(bf16)
        else:
            wsg = jnp.where(causal, ws_ref[g], 0.0).astype(bf16)
            bias = bs_ref[:, g:g + 1]
            for c in range(tm // CHUNK_B):
                rs = slice(c * CHUNK_B, (c + 1) * CHUNK_B)
                mixed = jnp.dot(wsg, vn_ref[rs, gs], preferred_element_type=f32) + bias
                gd_ref[rs, gs] = (u[rs] * mixed).astype(bf16)
    y = jnp.dot(gd_ref[...], wo_ref[...], preferred_element_type=f32)
    y_ref[...] = x + (1.0 + gt_ref[...]) * y


def _b_mixer(x, mods, l, li, nw3, wi, vw, vb, ws, bs, wo, decode):
    G, R, _ = x.shape
    tm = min(R, 512)
    sub = 1
    if decode:
        ws_spec = pl.BlockSpec((None, 1, B_WIDTH), lambda g, i: (li, 0, 0))
        bs_spec = pl.BlockSpec((None, 1, B_WIDTH), lambda g, i: (li, 0, 0))
        out_specs = [_tok_spec(tm, D_MODEL), _tok_spec(tm, B_WIDTH)]
        out_shape = [jax.ShapeDtypeStruct(x.shape, f32), jax.ShapeDtypeStruct((G, R, B_WIDTH), f32)]
    else:
        ws_spec = pl.BlockSpec((None, B_GROUPS, CHUNK_B, CHUNK_B), lambda g, i: (li, 0, 0, 0))
        bs_spec = pl.BlockSpec((None, CHUNK_B, B_GROUPS), lambda g, i: (li, 0, 0))
        out_specs = _tok_spec(tm, D_MODEL)
        out_shape = jax.ShapeDtypeStruct(x.shape, f32)
    return pl.pallas_call(
        functools.partial(_b_kernel, decode=decode),
        grid=(G, R // tm),
        in_specs=[
            _tok_spec(tm, D_MODEL),
            mods.spec(l, sub, 0, tm), mods.spec(l, sub, 1, tm), mods.spec(l, sub, 2, tm),
            pl.BlockSpec((None, 1, D_MODEL), lambda g, i: (l * N_SUB + sub, 0, 0)),
            _resident((None, D_MODEL, 2 * B_WIDTH), lambda g, i: (li, 0, 0)),
            pl.BlockSpec((None, 1, B_WIDTH), lambda g, i: (li, 0, 0)),
            pl.BlockSpec((None, 1, B_WIDTH), lambda g, i: (li, 0, 0)),
            ws_spec, bs_spec,
            _resident((None, B_WIDTH, D_MODEL), lambda g, i: (li, 0, 0)),
        ],
        out_specs=out_specs,
        out_shape=out_shape,
        scratch_shapes=[pltpu.VMEM((tm, B_WIDTH), bf16), pltpu.VMEM((tm, B_WIDTH), bf16)],
        compiler_params=_params(("arbitrary", "arbitrary")),
        name="b_mixer",
    )(x, mods.arr, mods.arr, mods.arr, nw3, wi, vw, vb, ws, bs, wo)


def _trunk(x, mods, decode, conv_state, s_state, W):
    new_conv, new_s, v_rows = [], [], []
    G, R, _ = x.shape
    for i in range(DEPTH):
        li = i // N_MIXERS
        x = _ffn(x, mods, i, 0, 0, W["nw3"], W["wgu"], W["wdn"])
        if i % N_MIXERS == 0:
            if decode:
                proj, gates = _a_in(x, mods, i, li, W["nw3"], W["a_wm"], W["a_ws"], W["a_alog"], W["a_dtb"])
                o, s_new = _a_step(proj, conv_state, gates, W["a_wconv"], s_state, li,
                                   new_s[-1] if new_s else None)
                pre = proj[0, :, None, :A_CONV_DIM]
                new_conv.append(jnp.concatenate([conv_state[li][:, 1:], pre], axis=1))
            else:
                proj, gates, tail = _a_in(x, mods, i, li, W["nw3"], W["a_wm"], W["a_ws"], W["a_alog"],
                                          W["a_dtb"], wconv=W["a_wconv"])
                grow = gates[..., :GATE_COLS].reshape(G, R // CHUNK_A, CHUNK_A, GATE_COLS)
                grow = jnp.swapaxes(grow, 2, 3)
                o, s_new = _a_delta(proj, gates, grow)
                new_conv.append(tail[:, 8 - (CONV_W - 1):])
            new_s.append(s_new)
            x = _a_out(x, o, proj, mods, i, li, W["a_won"], W["a_wo"])
        else:
            if decode:
                x, v = _b_mixer(x, mods, i, li, W["nw3"], W["b_wi"], W["b_vw"], W["b_vb"],
                                W["b_ws00"], W["b_bs0"], W["b_wo"], decode=True)
                v_rows.append(v)
            else:
                x = _b_mixer(x, mods, i, li, W["nw3"], W["b_wi"], W["b_vw"], W["b_vb"],
                             W["b_ws"], W["b_bsT"], W["b_wo"], decode=False)
        fw = W["final_w"] if i == DEPTH - 1 else None
        x = _ffn(x, mods, i, 1, 2, W["nw3"], W["wgu"], W["wdn"], final_w=fw)
    return x, new_conv, new_s, v_rows


def kernel(x_prompt, x_sample, state_a_conv, state_a_S, c_prompt, c_sample, w_ada, b_ada, norm_w, ffn_w_gu, ffn_w_down, a_w_in, a_w_conv, a_log, a_dt_bias, a_w_onorm, a_w_out, b_w_in, b_vnorm_w, b_vnorm_b, b_w_s, b_b_s, b_w_out, final_norm_w):
    n_a = a_w_in.shape[0]
    n_b = b_w_in.shape[0]
    bp, seq, _ = x_prompt.shape
    bs = x_sample.shape[0]
    assert x_sample.shape[1] == 1 and seq % A_LT == 0 and seq % CHUNK_B == 0

    mod_p, mod_s = _ada(c_prompt, c_sample, w_ada, b_ada)
    mods_p = _Mods(mod_p.reshape(DEPTH, bp, 1, N_SUB * 3 * D_MODEL), per_row=False)
    mods_s = _Mods(mod_s.reshape(DEPTH, 1, bs, N_SUB * 3 * D_MODEL), per_row=True)

    pad = LANES - 2 * A_HEADS
    gate_pad = lambda a: jnp.pad(a, ((0, 0), (A_HEADS, pad)))[:, None, :]
    W = {
        "nw3": norm_w.reshape(DEPTH * N_SUB, 1, D_MODEL),
        "wgu": ffn_w_gu.astype(bf16),
        "wdn": ffn_w_down.astype(bf16),
        "a_wm": a_w_in.astype(bf16),
        "a_ws": jnp.pad(a_w_in[:, :, A_MAIN_DIM:], ((0, 0), (0, 0), (0, pad))),
        "a_alog": gate_pad(a_log),
        "a_dtb": gate_pad(a_dt_bias),
        "a_wconv": a_w_conv,
        "a_won": a_w_onorm[:, None, :],
        "a_wo": a_w_out.astype(bf16),
        "b_wi": b_w_in.astype(bf16),
        "b_vw": b_vnorm_w[:, None, :],
        "b_vb": b_vnorm_b[:, None, :],
        "b_ws": b_w_s,
        "b_bsT": jnp.swapaxes(b_b_s, 1, 2),
        "b_ws00": jnp.repeat(b_w_s[:, :, 0, 0], B_GROUP_DIM, axis=1)[:, None, :],
        "b_bs0": jnp.repeat(b_b_s[:, :, 0], B_GROUP_DIM, axis=1)[:, None, :],
        "b_wo": b_w_out.astype(bf16),
        "final_w": final_norm_w,
    }

    y_p, conv_p, s_p, _ = _trunk(x_prompt, mods_p, False, None, None, W)
    y_s, conv_s, s_s, v_s = _trunk(x_sample.reshape(1, bs, D_MODEL), mods_s, True,
                                   state_a_conv, state_a_S, W)
    return (y_p,
            y_s.reshape(bs, 1, D_MODEL),
            jnp.stack(conv_p),
            jnp.stack(s_p),
            jnp.stack(conv_s),
            s_s[-1],
            jnp.stack(v_s).reshape(n_b, bs, 1, B_WIDTH))
```

```python
import functools

import jax
import jax.numpy as jnp
from jax import lax
from jax.experimental import pallas as pl
from jax.experimental.pallas import tpu as pltpu

f32 = jnp.float32
bf16 = jnp.bfloat16

D_MODEL = 1024
DEPTH = 4
N_MIXERS = 2
A_HEADS = 8
A_DK = 128
A_DV = 128
A_QK_DIM = A_HEADS * A_DK
A_V_DIM = A_HEADS * A_DV
A_CONV_DIM = 2 * A_QK_DIM + A_V_DIM
A_MAIN_DIM = A_CONV_DIM + A_V_DIM
CONV_W = 4
CHUNK_A = 64
B_WIDTH = 2 * D_MODEL
B_GROUPS = 8
B_GROUP_DIM = B_WIDTH // B_GROUPS
CHUNK_B = 128
D_FF = 2816
FFN_RES = 0.5
N_SUB = 3
EPS = 1e-6

LANES = 128
FF_TILE = 256
GATE_COLS = 3 * A_HEADS
VMEM_LIMIT = 52 * 1024 * 1024


def _params(sem):
    return pltpu.CompilerParams(dimension_semantics=sem, vmem_limit_bytes=VMEM_LIMIT)


def _resident(shape, index_map):
    return pl.BlockSpec(shape, index_map, pipeline_mode=pl.Buffered(1))


def _bdot(a, b):
    return jnp.dot(a.astype(bf16), b.astype(bf16), preferred_element_type=f32)


def _bdot_nt(a, b):
    return lax.dot_general(a.astype(bf16), b.astype(bf16), (((1,), (1,)), ((), ())),
                           preferred_element_type=f32)


def _bdot_tn(a, b):
    return lax.dot_general(a.astype(bf16), b.astype(bf16), (((0,), (0,)), ((), ())),
                           preferred_element_type=f32)


def _split2(a):
    hi = a.astype(bf16)
    lo = (a - hi.astype(f32)).astype(bf16)
    return hi, lo


def _split3(a):
    hi = a.astype(bf16)
    r = a - hi.astype(f32)
    mid = r.astype(bf16)
    lo = (r - mid.astype(f32)).astype(bf16)
    return hi, mid, lo


def _dot3(a, b):
    ah, al = _split2(a)
    bh, bl = _split2(b)
    d = functools.partial(jnp.dot, preferred_element_type=f32)
    return d(ah, bh) + d(al, bh) + d(ah, bl)


def _silu(x):
    return x * jax.nn.sigmoid(x)


def _softplus(x):
    return jnp.maximum(x, 0.0) + jnp.log1p(jnp.exp(-jnp.abs(x)))


GELU_A = 2.0 * (2.0 / 3.141592653589793) ** 0.5
GELU_B = GELU_A * 0.044715


def _gelu(x):
    return x * jax.nn.sigmoid(x * (GELU_A + GELU_B * (x * x)))


def _rms(x):
    return x * lax.rsqrt(jnp.mean(x * x, axis=-1, keepdims=True) + EPS)


def _norm_mod(x, nw, sc, sh):
    return (_rms(x) * nw) * (1.0 + sc) + sh


def _l2n(x):
    return x * lax.rsqrt(jnp.sum(x * x, axis=-1, keepdims=True) + EPS)


def _ada_kernel(cp_ref, cs_ref, w_ref, b_ref, op_ref, os_ref):
    w = w_ref[...].astype(bf16)
    b = b_ref[...]
    op_ref[...] = _bdot(_silu(cp_ref[...]), w) + b
    os_ref[...] = _bdot(_silu(cs_ref[...]), w) + b


def _ada(c_p, c_s, w_ada, b_ada):
    n = N_SUB * 3 * D_MODEL
    tn = 9 * LANES
    bp, bs = c_p.shape[0], c_s.shape[0]
    return pl.pallas_call(
        _ada_kernel,
        grid=(DEPTH, n // tn),
        in_specs=[
            pl.BlockSpec((bp, D_MODEL), lambda l, j: (0, 0)),
            pl.BlockSpec((bs, D_MODEL), lambda l, j: (0, 0)),
            pl.BlockSpec((None, D_MODEL, tn), lambda l, j: (l, 0, j)),
            pl.BlockSpec((None, 1, tn), lambda l, j: (l, 0, j)),
        ],
        out_specs=[
            pl.BlockSpec((None, bp, tn), lambda l, j: (l, 0, j)),
            pl.BlockSpec((None, bs, tn), lambda l, j: (l, 0, j)),
        ],
        out_shape=[jax.ShapeDtypeStruct((DEPTH, bp, n), f32),
                   jax.ShapeDtypeStruct((DEPTH, bs, n), f32)],
        compiler_params=_params(("arbitrary", "arbitrary")),
        name="ada",
    )(c_p, c_s, w_ada, b_ada.reshape(DEPTH, 1, n))


class _Mods:
    def __init__(self, arr, per_row):
        self.arr = arr
        self.per_row = per_row

    def spec(self, l, sub, kind, tm):
        col = sub * 3 + kind
        if self.per_row:
            return pl.BlockSpec((None, None, tm, D_MODEL), lambda g, i: (l, g, i, col))
        return pl.BlockSpec((None, None, 1, D_MODEL), lambda g, i: (l, g, 0, col))


def _tok_spec(tm, width, colblk=0):
    return pl.BlockSpec((None, tm, width), lambda g, i: (g, i, colblk))


def _ffn_kernel(x_ref, sh_ref, sc_ref, gt_ref, nw_ref, wg_ref, wu_ref, wd_ref, *rest, final):
    if final:
        fw_ref, o_ref, act_ref = rest
    else:
        o_ref, act_ref = rest
    x = x_ref[...]
    h = _norm_mod(x, nw_ref[...], sc_ref[...], sh_ref[...]).astype(bf16)
    for c in range(D_FF // FF_TILE):
        sl = slice(c * FF_TILE, (c + 1) * FF_TILE)
        g = jnp.dot(h, wg_ref[:, sl], preferred_element_type=f32)
        u = jnp.dot(h, wu_ref[:, sl], preferred_element_type=f32)
        act_ref[:, sl] = (_silu(g) * u).astype(bf16)
    y = jnp.dot(act_ref[...], wd_ref[...], preferred_element_type=f32)
    out = x + FFN_RES * (1.0 + gt_ref[...]) * y
    if final:
        out = _rms(out) * fw_ref[...]
    o_ref[...] = out


def _ffn(x, mods, l, j, sub, nw3, wgu, wdn, final_w=None):
    G, R, _ = x.shape
    tm = min(R, 1024)
    final = final_w is not None
    in_specs = [
        _tok_spec(tm, D_MODEL),
        mods.spec(l, sub, 0, tm), mods.spec(l, sub, 1, tm), mods.spec(l, sub, 2, tm),
        pl.BlockSpec((None, 1, D_MODEL), lambda g, i: (l * N_SUB + sub, 0, 0)),
        _resident((None, None, D_MODEL, D_FF), lambda g, i: (l, j, 0, 0)),
        _resident((None, None, D_MODEL, D_FF), lambda g, i: (l, j, 0, 1)),
        _resident((None, None, D_FF, D_MODEL), lambda g, i: (l, j, 0, 0)),
    ]
    args = [x, mods.arr, mods.arr, mods.arr, nw3, wgu, wgu, wdn]
    if final:
        in_specs.append(pl.BlockSpec((1, D_MODEL), lambda g, i: (0, 0)))
        args.append(final_w.reshape(1, D_MODEL))
    return pl.pallas_call(
        functools.partial(_ffn_kernel, final=final),
        grid=(G, R // tm),
        in_specs=in_specs,
        out_specs=_tok_spec(tm, D_MODEL),
        out_shape=jax.ShapeDtypeStruct(x.shape, f32),
        scratch_shapes=[pltpu.VMEM((tm, D_FF), bf16)],
        compiler_params=_params(("arbitrary", "arbitrary")),
        name="ffn",
    )(*args)


def _gate_values(hf, ws_ref, alog_ref, dtb_ref):
    p = _dot3(hf, ws_ref[...])
    lane = lax.broadcasted_iota(jnp.int32, p.shape, 1)
    beta = jax.nn.sigmoid(p)
    g = -jnp.exp(alog_ref[...]) * _softplus(p + dtb_ref[...])
    is_g = (lane >= A_HEADS) & (lane < 2 * A_HEADS)
    return jnp.where(lane < A_HEADS, beta, jnp.where(is_g, g, 0.0))


def _a_in_kernel(x_ref, sh_ref, sc_ref, nw_ref, wm_ref, ws_ref, alog_ref, dtb_ref, proj_ref, gates_ref):
    hf = _norm_mod(x_ref[...], nw_ref[...], sc_ref[...], sh_ref[...])
    h = hf.astype(bf16)
    for c in range(A_MAIN_DIM // D_MODEL):
        sl = slice(c * D_MODEL, (c + 1) * D_MODEL)
        proj_ref[:, sl] = jnp.dot(h, wm_ref[:, sl], preferred_element_type=f32)
    gates_ref[...] = _gate_values(hf, ws_ref, alog_ref, dtb_ref)


def _a_in(x, mods, l, li, nw3, wm, ws, alog, dtb):
    G, R, _ = x.shape
    tm = min(R, 512)
    sub = 1
    return pl.pallas_call(
        _a_in_kernel,
        grid=(G, R // tm),
        in_specs=[
            _tok_spec(tm, D_MODEL),
            mods.spec(l, sub, 0, tm), mods.spec(l, sub, 1, tm),
            pl.BlockSpec((None, 1, D_MODEL), lambda g, i: (l * N_SUB + sub, 0, 0)),
            _resident((None, D_MODEL, A_MAIN_DIM), lambda g, i: (li, 0, 0)),
            _resident((None, D_MODEL, LANES), lambda g, i: (li, 0, 0)),
            pl.BlockSpec((None, 1, LANES), lambda g, i: (li, 0, 0)),
            pl.BlockSpec((None, 1, LANES), lambda g, i: (li, 0, 0)),
        ],
        out_specs=[_tok_spec(tm, A_MAIN_DIM), _tok_spec(tm, LANES)],
        out_shape=[jax.ShapeDtypeStruct((G, R, A_MAIN_DIM), f32),
                   jax.ShapeDtypeStruct((G, R, LANES), f32)],
        compiler_params=_params(("arbitrary", "arbitrary")),
        name="a_in",
    )(x, mods.arr, mods.arr, nw3, wm, ws, alog, dtb)


A_LT = 256


def _a_prompt_kernel(x_ref, sh_ref, sc_ref, gt_ref, nw_ref, wm_ref, ws_ref, alog_ref, dtb_ref, wc_ref,
                     won_ref, wo_ref, y_ref, s_ref, tail_ref, cbuf, act_ref, o_scr, og_ref):
    t = pl.program_id(1)
    lt = x_ref.shape[0]
    nc = lt // CHUNK_A

    @pl.when(t == 0)
    def _():
        s_ref[...] = jnp.zeros(s_ref.shape, f32)
        cbuf[0:8, :] = jnp.zeros((8, A_CONV_DIM), f32)

    x = x_ref[...]
    hf = _norm_mod(x, nw_ref[...], sc_ref[...], sh_ref[...])
    h = hf.astype(bf16)

    ri = lax.broadcasted_iota(jnp.int32, (CHUNK_A, CHUNK_A), 0)
    ci = lax.broadcasted_iota(jnp.int32, (CHUNK_A, CHUNK_A), 1)
    causal = ri >= ci
    strict = ri > ci

    gates = _gate_values(hf, ws_ref, alog_ref, dtb_ref)
    lane = lax.broadcasted_iota(jnp.int32, gates.shape, 1)
    is_g = (lane >= A_HEADS) & (lane < 2 * A_HEADS)
    ghi, gmid, glo = _split3(pltpu.roll(jnp.where(is_g, gates, 0.0), A_HEADS, 1))
    tri = jnp.where(causal, 1.0, 0.0).astype(bf16)
    d = functools.partial(jnp.dot, preferred_element_type=f32)
    gcols, grows = [], []
    for c in range(nc):
        rs = slice(c * CHUNK_A, (c + 1) * CHUNK_A)
        gc = gates[rs] + d(tri, ghi[rs]) + d(tri, gmid[rs]) + d(tri, glo[rs])
        gcols.append(gc)
        grows.append(gc.T)

    pair = 2 * A_DK
    for p in range(A_HEADS // 2):
        for part in range(3):
            col = part * A_QK_DIM + p * pair
            cbuf[8:8 + lt, col:col + pair] = jnp.dot(h, wm_ref[:, col:col + pair],
                                                     preferred_element_type=f32)
            for hh in range(2):
                sl = slice(col + hh * A_DK, col + (hh + 1) * A_DK)
                acc = cbuf[5:5 + lt, sl] * wc_ref[0:1, sl]
                for j in range(1, CONV_W):
                    acc = acc + cbuf[5 + j:5 + j + lt, sl] * wc_ref[j:j + 1, sl]
                a = _silu(acc)
                if part == 0:
                    a = a * (lax.rsqrt(jnp.sum(a * a, axis=-1, keepdims=True) + EPS) * (A_DK ** -0.5))
                elif part == 1:
                    a = _l2n(a)
                act_ref[:, sl] = a
    tail = cbuf[lt:lt + 8, :]
    tail_ref[...] = tail
    cbuf[0:8, :] = tail

    inst = []
    for h_ in range(A_HEADS):
        for c in range(nc):
            rs = slice(c * CHUNK_A, (c + 1) * CHUNK_A)
            qc = act_ref[rs, h_ * A_DK:(h_ + 1) * A_DK]
            kc = act_ref[rs, A_QK_DIM + h_ * A_DK:A_QK_DIM + (h_ + 1) * A_DK]
            vc = act_ref[rs, 2 * A_QK_DIM + h_ * A_DV:2 * A_QK_DIM + (h_ + 1) * A_DV]
            bcol = gcols[c][:, h_:h_ + 1]
            dcol = gcols[c][:, 2 * A_HEADS + h_:2 * A_HEADS + h_ + 1]
            drow = grows[c][2 * A_HEADS + h_:2 * A_HEADS + h_ + 1, :]
            dlast = dcol[CHUNK_A - 1:CHUNK_A, :]
            gam = jnp.where(causal, jnp.exp(jnp.where(causal, dcol - drow, 0.0)), 0.0)
            kb = kc * bcol
            ed = jnp.exp(dcol)
            kt = kc.T
            kq = _bdot(jnp.concatenate([kb, qc], axis=0), kt)
            m = jnp.where(strict, kq[:CHUNK_A] * gam, 0.0)
            attn = kq[CHUNK_A:] * gam
            kdec_t = kt * jnp.exp(dlast - drow)
            inst.append(dict(
                m=m,
                rhs=jnp.concatenate([vc * bcol, kb * ed], axis=1),
                qed=(qc * ed).astype(bf16),
                lhs2=jnp.concatenate([attn, kdec_t], axis=0).astype(bf16),
                edl=jnp.exp(dlast)))

    ms = [i["m"] for i in inst]
    ts = [-m for m in ms]
    n = 2
    while n < CHUNK_A:
        ms = [_bdot(m, m) for m in ms]
        ts = [t + m + _bdot(t, m) for t, m in zip(ts, ms)]
        n *= 2
    rs_ = [-(i["m"] + t + _dot3(i["m"], t)) for i, t in zip(inst, ts)]
    ts = [t + r + _bdot(t, r) for t, r in zip(ts, rs_)]

    for i, t in zip(inst, ts):
        uw = i["rhs"] + _bdot(t, i["rhs"])
        i["u"] = uw[:, :A_DV]
        i["lhs1"] = jnp.concatenate([uw[:, A_DV:].astype(bf16), i["qed"]], axis=0)

    states = [s_ref[h_] for h_ in range(A_HEADS)]
    for c in range(nc):
        rs = slice(c * CHUNK_A, (c + 1) * CHUNK_A)
        for h_ in range(A_HEADS):
            i = inst[h_ * nc + c]
            ws_qs = jnp.dot(i["lhs1"], states[h_].astype(bf16), preferred_element_type=f32)
            v_new = i["u"] - ws_qs[:CHUNK_A]
            r = jnp.dot(i["lhs2"], v_new.astype(bf16), preferred_element_type=f32)
            o_scr[rs, h_ * A_DV:(h_ + 1) * A_DV] = ws_qs[CHUNK_A:] + r[:CHUNK_A]
            states[h_] = states[h_] * i["edl"] + r[CHUNK_A:]
    for h_ in range(A_HEADS):
        s_ref[h_] = states[h_]

    z = jnp.dot(h, wm_ref[:, A_CONV_DIM:A_MAIN_DIM], preferred_element_type=f32)
    won = won_ref[...]
    for h_ in range(A_HEADS):
        sl = slice(h_ * A_DV, (h_ + 1) * A_DV)
        og_ref[:, sl] = (_rms(o_scr[:, sl]) * won * _silu(z[:, sl])).astype(bf16)
    y = jnp.dot(og_ref[...], wo_ref[...], preferred_element_type=f32)
    y_ref[...] = x + (1.0 + gt_ref[...]) * y


def _a_prompt(x, mods, l, li, nw3, wm, ws, alog, dtb, wconv, won, wo):
    B, L, _ = x.shape
    lt = A_LT
    sub = 1
    const = lambda shape: pl.BlockSpec((None,) + shape, lambda b, t: (li,) + (0,) * len(shape))
    return pl.pallas_call(
        _a_prompt_kernel,
        grid=(B, L // lt),
        in_specs=[
            _tok_spec(lt, D_MODEL),
            mods.spec(l, sub, 0, lt), mods.spec(l, sub, 1, lt), mods.spec(l, sub, 2, lt),
            pl.BlockSpec((None, 1, D_MODEL), lambda b, t: (l * N_SUB + sub, 0, 0)),
            _resident((None, D_MODEL, A_MAIN_DIM), lambda b, t: (li, 0, 0)),
            _resident((None, D_MODEL, LANES), lambda b, t: (li, 0, 0)),
            const((1, LANES)), const((1, LANES)),
            const((CONV_W, A_CONV_DIM)),
            const((1, A_DV)),
            _resident((None, A_V_DIM, D_MODEL), lambda b, t: (li, 0, 0)),
        ],
        out_specs=[
            _tok_spec(lt, D_MODEL),
            pl.BlockSpec((None, A_HEADS, A_DK, A_DV), lambda b, t: (b, 0, 0, 0)),
            pl.BlockSpec((None, 8, A_CONV_DIM), lambda b, t: (b, 0, 0)),
        ],
        out_shape=[jax.ShapeDtypeStruct(x.shape, f32),
                   jax.ShapeDtypeStruct((B, A_HEADS, A_DK, A_DV), f32),
                   jax.ShapeDtypeStruct((B, 8, A_CONV_DIM), f32)],
        scratch_shapes=[pltpu.VMEM((lt + 8, A_CONV_DIM), f32),
                        pltpu.VMEM((lt, A_CONV_DIM), f32),
                        pltpu.VMEM((lt, A_V_DIM), f32),
                        pltpu.VMEM((lt, A_V_DIM), bf16)],
        compiler_params=_params(("arbitrary", "arbitrary")),
        name="a_prompt",
    )(x, mods.arr, mods.arr, mods.arr, nw3, wm, ws, alog, dtb, wconv, won, wo)


A_BT = 8


def _a_step_kernel(pre_ref, conv_ref, gates_ref, wc_ref, s_ref, *rest, li):
    if li:
        prev_ref, o_ref, so_all_ref = rest
        so_all_ref[0:li] = prev_ref[...]
    else:
        o_ref, so_all_ref = rest
    so_ref = so_all_ref.at[li]
    bt = pre_ref.shape[0]
    wc = wc_ref[...]
    acc = pre_ref[...] * wc[CONV_W - 1:CONV_W, :]
    for j in range(CONV_W - 1):
        acc = acc + conv_ref[:, j, :] * wc[j:j + 1, :]
    qkv = _silu(acc)
    gates = gates_ref[...]
    row = lax.broadcasted_iota(jnp.int32, (bt, 8, LANES), 1)
    for h in range(A_HEADS):
        q = _l2n(qkv[:, h * A_DK:(h + 1) * A_DK]) * (A_DK ** -0.5)
        k = _l2n(qkv[:, A_QK_DIM + h * A_DK:A_QK_DIM + (h + 1) * A_DK])
        v = qkv[:, 2 * A_QK_DIM + h * A_DV:2 * A_QK_DIM + (h + 1) * A_DV]
        beta = gates[:, h:h + 1]
        eg = jnp.exp(gates[:, A_HEADS + h:A_HEADS + h + 1])
        s = s_ref[:, h]
        kq = jnp.where(row == 0, k[:, None, :], jnp.where(row == 1, q[:, None, :], 0.0))
        ks = jnp.einsum('bmk,bkn->bmn', kq.astype(bf16), s.astype(bf16),
                        preferred_element_type=f32)
        v_new = beta * (v - eg * ks[:, 0, :])
        qk = jnp.sum(q * k, axis=-1, keepdims=True)
        o_ref[:, h * A_DV:(h + 1) * A_DV] = eg * ks[:, 1, :] + qk * v_new
        kh = k.astype(bf16).astype(f32)
        vh = v_new.astype(bf16).astype(f32)
        kl, vl = k - kh, v_new - vh
        lhs = jnp.where(row == 2, kl[:, None, :], jnp.where(row < 2, kh[:, None, :], 0.0))
        rhs = jnp.where(row == 1, vl[:, None, :], jnp.where(row < 3, vh[:, None, :], 0.0))
        outer = lax.dot_general(lhs.astype(bf16), rhs.astype(bf16), (((1,), (1,)), ((0,), (0,))),
                                preferred_element_type=f32)
        so_ref[:, h] = s * eg[:, :, None] + outer


def _a_step(proj, conv_state, gates, wconv, s0, li, s_prev):
    nb = proj.shape[1]
    bt = A_BT
    in_specs = [
        pl.BlockSpec((None, bt, A_CONV_DIM), lambda i: (0, i, 0)),
        pl.BlockSpec((None, bt, CONV_W - 1, A_CONV_DIM), lambda i: (li, i, 0, 0)),
        pl.BlockSpec((None, bt, LANES), lambda i: (0, i, 0)),
        pl.BlockSpec((None, CONV_W, A_CONV_DIM), lambda i: (li, 0, 0)),
        pl.BlockSpec((None, bt, A_HEADS, A_DK, A_DV), lambda i: (li, i, 0, 0, 0)),
    ]
    args = [proj, conv_state, gates, wconv, s0]
    if li:
        in_specs.append(pl.BlockSpec((li, bt, A_HEADS, A_DK, A_DV), lambda i: (0, i, 0, 0, 0)))
        args.append(s_prev)
    return pl.pallas_call(
        functools.partial(_a_step_kernel, li=li),
        grid=(nb // bt,),
        in_specs=in_specs,
        out_specs=[
            pl.BlockSpec((None, bt, A_V_DIM), lambda i: (0, i, 0)),
            pl.BlockSpec((li + 1, bt, A_HEADS, A_DK, A_DV), lambda i: (0, i, 0, 0, 0)),
        ],
        out_shape=[jax.ShapeDtypeStruct((1, nb, A_V_DIM), f32),
                   jax.ShapeDtypeStruct((li + 1,) + s0.shape[1:], f32)],
        compiler_params=_params(("arbitrary",)),
        name="a_step",
    )(*args)


def _a_out_kernel(x_ref, o_ref, z_ref, gt_ref, won_ref, wo_ref, y_ref, og_ref):
    won = won_ref[...]
    for h in range(A_HEADS):
        sl = slice(h * A_DV, (h + 1) * A_DV)
        og_ref[:, sl] = (_rms(o_ref[:, sl]) * won * _silu(z_ref[:, sl])).astype(bf16)
    y = jnp.dot(og_ref[...], wo_ref[...], preferred_element_type=f32)
    y_ref[...] = x_ref[...] + (1.0 + gt_ref[...]) * y


def _a_out(x, o, proj, mods, l, li, won, wo):
    G, R, _ = x.shape
    tm = min(R, 512)
    return pl.pallas_call(
        _a_out_kernel,
        grid=(G, R // tm),
        in_specs=[
            _tok_spec(tm, D_MODEL),
            _tok_spec(tm, A_V_DIM),
            _tok_spec(tm, A_V_DIM, A_CONV_DIM // A_V_DIM),
            mods.spec(l, 1, 2, tm),
            pl.BlockSpec((None, 1, A_DV), lambda g, i: (li, 0, 0)),
            _resident((None, A_V_DIM, D_MODEL), lambda g, i: (li, 0, 0)),
        ],
        out_specs=_tok_spec(tm, D_MODEL),
        out_shape=jax.ShapeDtypeStruct(x.shape, f32),
        scratch_shapes=[pltpu.VMEM((tm, A_V_DIM), bf16)],
        compiler_params=_params(("arbitrary", "arbitrary")),
        name="a_out",
    )(x, o, proj, mods.arr, won, wo)


def _b_kernel(x_ref, sh_ref, sc_ref, gt_ref, nw_ref, wi_ref, vw_ref, vb_ref, ws_ref, bs_ref, wo_ref,
              *rest, decode):
    if decode:
        y_ref, vout_ref, vn_ref, gd_ref = rest
    else:
        y_ref, vn_ref, gd_ref = rest
    tm = x_ref.shape[0]
    x = x_ref[...]
    h = _norm_mod(x, nw_ref[...], sc_ref[...], sh_ref[...]).astype(bf16)
    v = _gelu(jnp.dot(h, wi_ref[:, B_WIDTH:], preferred_element_type=f32))
    mu = jnp.mean(v, axis=-1, keepdims=True)
    vc = v - mu
    vn = vc * lax.rsqrt(jnp.mean(vc * vc, axis=-1, keepdims=True) + EPS) * vw_ref[...] + vb_ref[...]
    if decode:
        vout_ref[...] = vn
    else:
        vn_ref[...] = vn.astype(bf16)
        ri = lax.broadcasted_iota(jnp.int32, (CHUNK_B, CHUNK_B), 0)
        ci = lax.broadcasted_iota(jnp.int32, (CHUNK_B, CHUNK_B), 1)
        causal = ri >= ci
    for g in range(B_GROUPS):
        gs = slice(g * B_GROUP_DIM, (g + 1) * B_GROUP_DIM)
        u = _gelu(jnp.dot(h, wi_ref[:, gs], preferred_element_type=f32))
        if decode:
            gd_ref[:, gs] = (u * (ws_ref[:, gs] * vn[:, gs] + bs_ref[:, gs])).astype(bf16)
        else:
            wsg = jnp.where(causal, ws_ref[g], 0.0).astype(bf16)
            bias = bs_ref[:, g:g + 1]
            for c in range(tm // CHUNK_B):
                rs = slice(c * CHUNK_B, (c + 1) * CHUNK_B)
                mixed = jnp.dot(wsg, vn_ref[rs, gs], preferred_element_type=f32) + bias
                gd_ref[rs, gs] = (u[rs] * mixed).astype(bf16)
    y = jnp.dot(gd_ref[...], wo_ref[...], preferred_element_type=f32)
    y_ref[...] = x + (1.0 + gt_ref[...]) * y


def _b_mixer(x, mods, l, li, nw3, wi, vw, vb, ws, bs, wo, decode):
    G, R, _ = x.shape
    tm = min(R, 512)
    sub = 1
    if decode:
        ws_spec = pl.BlockSpec((None, 1, B_WIDTH), lambda g, i: (li, 0, 0))
        bs_spec = pl.BlockSpec((None, 1, B_WIDTH), lambda g, i: (li, 0, 0))
        out_specs = [_tok_spec(tm, D_MODEL), _tok_spec(tm, B_WIDTH)]
        out_shape = [jax.ShapeDtypeStruct(x.shape, f32), jax.ShapeDtypeStruct((G, R, B_WIDTH), f32)]
    else:
        ws_spec = pl.BlockSpec((None, B_GROUPS, CHUNK_B, CHUNK_B), lambda g, i: (li, 0, 0, 0))
        bs_spec = pl.BlockSpec((None, CHUNK_B, B_GROUPS), lambda g, i: (li, 0, 0))
        out_specs = _tok_spec(tm, D_MODEL)
        out_shape = jax.ShapeDtypeStruct(x.shape, f32)
    return pl.pallas_call(
        functools.partial(_b_kernel, decode=decode),
        grid=(G, R // tm),
        in_specs=[
            _tok_spec(tm, D_MODEL),
            mods.spec(l, sub, 0, tm), mods.spec(l, sub, 1, tm), mods.spec(l, sub, 2, tm),
            pl.BlockSpec((None, 1, D_MODEL), lambda g, i: (l * N_SUB + sub, 0, 0)),
            _resident((None, D_MODEL, 2 * B_WIDTH), lambda g, i: (li, 0, 0)),
            pl.BlockSpec((None, 1, B_WIDTH), lambda g, i: (li, 0, 0)),
            pl.BlockSpec((None, 1, B_WIDTH), lambda g, i: (li, 0, 0)),
            ws_spec, bs_spec,
            _resident((None, B_WIDTH, D_MODEL), lambda g, i: (li, 0, 0)),
        ],
        out_specs=out_specs,
        out_shape=out_shape,
        scratch_shapes=[pltpu.VMEM((tm, B_WIDTH), bf16), pltpu.VMEM((tm, B_WIDTH), bf16)],
        compiler_params=_params(("arbitrary", "arbitrary")),
        name="b_mixer",
    )(x, mods.arr, mods.arr, mods.arr, nw3, wi, vw, vb, ws, bs, wo)


def _trunk(x, mods, decode, conv_state, s_state, W):
    new_conv, new_s, v_rows = [], [], []
    G, R, _ = x.shape
    for i in range(DEPTH):
        li = i // N_MIXERS
        x = _ffn(x, mods, i, 0, 0, W["nw3"], W["wgu"], W["wdn"])
        if i % N_MIXERS == 0:
            if decode:
                proj, gates = _a_in(x, mods, i, li, W["nw3"], W["a_wm"], W["a_ws"], W["a_alog"], W["a_dtb"])
                o, s_new = _a_step(proj, conv_state, gates, W["a_wconv"], s_state, li,
                                   new_s[-1] if new_s else None)
                pre = proj[0, :, None, :A_CONV_DIM]
                new_conv.append(jnp.concatenate([conv_state[li][:, 1:], pre], axis=1))
                x = _a_out(x, o, proj, mods, i, li, W["a_won"], W["a_wo"])
            else:
                x, s_new, tail = _a_prompt(x, mods, i, li, W["nw3"], W["a_wm"], W["a_ws"], W["a_alog"],
                                           W["a_dtb"], W["a_wconv"], W["a_won"], W["a_wo"])
                new_conv.append(tail[:, 8 - (CONV_W - 1):])
            new_s.append(s_new)
        else:
            if decode:
                x, v = _b_mixer(x, mods, i, li, W["nw3"], W["b_wi"], W["b_vw"], W["b_vb"],
                                W["b_ws00"], W["b_bs0"], W["b_wo"], decode=True)
                v_rows.append(v)
            else:
                x = _b_mixer(x, mods, i, li, W["nw3"], W["b_wi"], W["b_vw"], W["b_vb"],
                             W["b_ws"], W["b_bsT"], W["b_wo"], decode=False)
        fw = W["final_w"] if i == DEPTH - 1 else None
        x = _ffn(x, mods, i, 1, 2, W["nw3"], W["wgu"], W["wdn"], final_w=fw)
    return x, new_conv, new_s, v_rows


def kernel(x_prompt, x_sample, state_a_conv, state_a_S, c_prompt, c_sample, w_ada, b_ada, norm_w, ffn_w_gu, ffn_w_down, a_w_in, a_w_conv, a_log, a_dt_bias, a_w_onorm, a_w_out, b_w_in, b_vnorm_w, b_vnorm_b, b_w_s, b_b_s, b_w_out, final_norm_w):
    n_a = a_w_in.shape[0]
    n_b = b_w_in.shape[0]
    bp, seq, _ = x_prompt.shape
    bs = x_sample.shape[0]
    assert x_sample.shape[1] == 1 and seq % A_LT == 0 and seq % CHUNK_B == 0

    mod_p, mod_s = _ada(c_prompt, c_sample, w_ada, b_ada)
    mods_p = _Mods(mod_p.reshape(DEPTH, bp, 1, N_SUB * 3 * D_MODEL), per_row=False)
    mods_s = _Mods(mod_s.reshape(DEPTH, 1, bs, N_SUB * 3 * D_MODEL), per_row=True)

    pad = LANES - 2 * A_HEADS
    gate_pad = lambda a: jnp.pad(a, ((0, 0), (A_HEADS, pad)))[:, None, :]
    W = {
        "nw3": norm_w.reshape(DEPTH * N_SUB, 1, D_MODEL),
        "wgu": ffn_w_gu.astype(bf16),
        "wdn": ffn_w_down.astype(bf16),
        "a_wm": a_w_in.astype(bf16),
        "a_ws": jnp.pad(a_w_in[:, :, A_MAIN_DIM:], ((0, 0), (0, 0), (0, pad))),
        "a_alog": gate_pad(a_log),
        "a_dtb": gate_pad(a_dt_bias),
        "a_wconv": a_w_conv,
        "a_won": a_w_onorm[:, None, :],
        "a_wo": a_w_out.astype(bf16),
        "b_wi": b_w_in.astype(bf16),
        "b_vw": b_vnorm_w[:, None, :],
        "b_vb": b_vnorm_b[:, None, :],
        "b_ws": b_w_s,
        "b_bsT": jnp.swapaxes(b_b_s, 1, 2),
        "b_ws00": jnp.repeat(b_w_s[:, :, 0, 0], B_GROUP_DIM, axis=1)[:, None, :],
        "b_bs0": jnp.repeat(b_b_s[:, :, 0], B_GROUP_DIM, axis=1)[:, None, :],
        "b_wo": b_w_out.astype(bf16),
        "final_w": final_norm_w,
    }

    y_p, conv_p, s_p, _ = _trunk(x_prompt, mods_p, False, None, None, W)
    y_s, conv_s, s_s, v_s = _trunk(x_sample.reshape(1, bs, D_MODEL), mods_s, True,
                                   state_a_conv, state_a_S, W)
    return (y_p,
            y_s.reshape(bs, 1, D_MODEL),
            jnp.stack(conv_p),
            jnp.stack(s_p),
            jnp.stack(conv_s),
            s_s[-1],
            jnp.stack(v_s).reshape(n_b, bs, 1, B_WIDTH))
```

```python
import functools

import jax
import jax.numpy as jnp
import numpy as np
from jax import lax
from jax.experimental import pallas as pl
from jax.experimental.pallas import tpu as pltpu

f32 = jnp.float32
bf16 = jnp.bfloat16

D_MODEL = 1024
DEPTH = 4
N_MIXERS = 2
A_HEADS = 8
A_DK = 128
A_DV = 128
A_QK_DIM = A_HEADS * A_DK
A_V_DIM = A_HEADS * A_DV
A_CONV_DIM = 2 * A_QK_DIM + A_V_DIM
A_MAIN_DIM = A_CONV_DIM + A_V_DIM
CONV_W = 4
CHUNK_A = 64
B_WIDTH = 2 * D_MODEL
B_GROUPS = 8
B_GROUP_DIM = B_WIDTH // B_GROUPS
CHUNK_B = 128
D_FF = 2816
FFN_RES = 0.5
N_SUB = 3
EPS = 1e-6

LANES = 128
FF_TILE = 256
GATE_COLS = 3 * A_HEADS
VMEM_LIMIT = 52 * 1024 * 1024


def _params(sem):
    return pltpu.CompilerParams(dimension_semantics=sem, vmem_limit_bytes=VMEM_LIMIT)


def _resident(shape, index_map):
    return pl.BlockSpec(shape, index_map, pipeline_mode=pl.Buffered(1))


def _bdot(a, b):
    return jnp.dot(a.astype(bf16), b.astype(bf16), preferred_element_type=f32)


def _bdot_nt(a, b):
    return lax.dot_general(a.astype(bf16), b.astype(bf16), (((1,), (1,)), ((), ())),
                           preferred_element_type=f32)


def _bdot_tn(a, b):
    return lax.dot_general(a.astype(bf16), b.astype(bf16), (((0,), (0,)), ((), ())),
                           preferred_element_type=f32)


def _split2(a):
    hi = a.astype(bf16)
    lo = (a - hi.astype(f32)).astype(bf16)
    return hi, lo


def _split3(a):
    hi = a.astype(bf16)
    r = a - hi.astype(f32)
    mid = r.astype(bf16)
    lo = (r - mid.astype(f32)).astype(bf16)
    return hi, mid, lo


def _dot3(a, b):
    ah, al = _split2(a)
    bh, bl = _split2(b)
    d = functools.partial(jnp.dot, preferred_element_type=f32)
    return d(ah, bh) + d(al, bh) + d(ah, bl)


def _silu(x):
    return x * jax.nn.sigmoid(x)


def _softplus(x):
    return jnp.maximum(x, 0.0) + jnp.log1p(jnp.exp(-jnp.abs(x)))


GELU_A = 2.0 * (2.0 / 3.141592653589793) ** 0.5
GELU_B = GELU_A * 0.044715


def _gelu(x):
    return x * jax.nn.sigmoid(x * (GELU_A + GELU_B * (x * x)))


def _rms(x):
    return x * lax.rsqrt(jnp.mean(x * x, axis=-1, keepdims=True) + EPS)


def _norm_mod(x, nw, sc, sh):
    return (_rms(x) * nw) * (1.0 + sc) + sh


def _l2n(x):
    return x * lax.rsqrt(jnp.sum(x * x, axis=-1, keepdims=True) + EPS)


def _ada_kernel(cp_ref, cs_ref, w_ref, b_ref, op_ref, os_ref):
    w = w_ref[...].astype(bf16)
    b = b_ref[...]
    op_ref[...] = _bdot(_silu(cp_ref[...]), w) + b
    os_ref[...] = _bdot(_silu(cs_ref[...]), w) + b


def _ada(c_p, c_s, w_ada, b_ada):
    n = N_SUB * 3 * D_MODEL
    tn = 9 * LANES
    bp, bs = c_p.shape[0], c_s.shape[0]
    return pl.pallas_call(
        _ada_kernel,
        grid=(DEPTH, n // tn),
        in_specs=[
            pl.BlockSpec((bp, D_MODEL), lambda l, j: (0, 0)),
            pl.BlockSpec((bs, D_MODEL), lambda l, j: (0, 0)),
            pl.BlockSpec((None, D_MODEL, tn), lambda l, j: (l, 0, j)),
            pl.BlockSpec((None, 1, tn), lambda l, j: (l, 0, j)),
        ],
        out_specs=[
            pl.BlockSpec((None, bp, tn), lambda l, j: (l, 0, j)),
            pl.BlockSpec((None, bs, tn), lambda l, j: (l, 0, j)),
        ],
        out_shape=[jax.ShapeDtypeStruct((DEPTH, bp, n), f32),
                   jax.ShapeDtypeStruct((DEPTH, bs, n), f32)],
        compiler_params=_params(("arbitrary", "arbitrary")),
        name="ada",
    )(c_p, c_s, w_ada, b_ada.reshape(DEPTH, 1, n))


class _Mods:
    def __init__(self, arr, per_row):
        self.arr = arr
        self.per_row = per_row

    def spec(self, l, sub, kind, tm):
        col = sub * 3 + kind
        if self.per_row:
            return pl.BlockSpec((None, None, tm, D_MODEL), lambda g, i: (l, g, i, col))
        return pl.BlockSpec((None, None, 1, D_MODEL), lambda g, i: (l, g, 0, col))


def _tok_spec(tm, width, colblk=0):
    return pl.BlockSpec((None, tm, width), lambda g, i: (g, i, colblk))


def _ffn_kernel(x_ref, sh_ref, sc_ref, gt_ref, nw_ref, wg_ref, wu_ref, wd_ref, *rest, final):
    if final:
        fw_ref, o_ref, act_ref = rest
    else:
        o_ref, act_ref = rest
    x = x_ref[...]
    h = _norm_mod(x, nw_ref[...], sc_ref[...], sh_ref[...]).astype(bf16)
    for c in range(D_FF // FF_TILE):
        sl = slice(c * FF_TILE, (c + 1) * FF_TILE)
        g = jnp.dot(h, wg_ref[:, sl], preferred_element_type=f32)
        u = jnp.dot(h, wu_ref[:, sl], preferred_element_type=f32)
        act_ref[:, sl] = (_silu(g) * u).astype(bf16)
    y = jnp.dot(act_ref[...], wd_ref[...], preferred_element_type=f32)
    out = x + FFN_RES * (1.0 + gt_ref[...]) * y
    if final:
        out = _rms(out) * fw_ref[...]
    o_ref[...] = out


def _ffn(x, mods, l, j, sub, nw3, wgu, wdn, final_w=None):
    G, R, _ = x.shape
    tm = min(R, 1024)
    final = final_w is not None
    in_specs = [
        _tok_spec(tm, D_MODEL),
        mods.spec(l, sub, 0, tm), mods.spec(l, sub, 1, tm), mods.spec(l, sub, 2, tm),
        pl.BlockSpec((None, 1, D_MODEL), lambda g, i: (l * N_SUB + sub, 0, 0)),
        _resident((None, None, D_MODEL, D_FF), lambda g, i: (l, j, 0, 0)),
        _resident((None, None, D_MODEL, D_FF), lambda g, i: (l, j, 0, 1)),
        _resident((None, None, D_FF, D_MODEL), lambda g, i: (l, j, 0, 0)),
    ]
    args = [x, mods.arr, mods.arr, mods.arr, nw3, wgu, wgu, wdn]
    if final:
        in_specs.append(pl.BlockSpec((1, D_MODEL), lambda g, i: (0, 0)))
        args.append(final_w.reshape(1, D_MODEL))
    return pl.pallas_call(
        functools.partial(_ffn_kernel, final=final),
        grid=(G, R // tm),
        in_specs=in_specs,
        out_specs=_tok_spec(tm, D_MODEL),
        out_shape=jax.ShapeDtypeStruct(x.shape, f32),
        scratch_shapes=[pltpu.VMEM((tm, D_FF), bf16)],
        compiler_params=_params(("arbitrary", "arbitrary")),
        name="ffn",
    )(*args)


def _gate_values(hf, ws_ref, alog_ref, dtb_ref):
    p = _dot3(hf, ws_ref[...])
    lane = lax.broadcasted_iota(jnp.int32, p.shape, 1)
    beta = jax.nn.sigmoid(p)
    g = -jnp.exp(alog_ref[...]) * _softplus(p + dtb_ref[...])
    is_g = (lane >= A_HEADS) & (lane < 2 * A_HEADS)
    return jnp.where(lane < A_HEADS, beta, jnp.where(is_g, g, 0.0))


def _a_in_kernel(x_ref, sh_ref, sc_ref, nw_ref, wm_ref, ws_ref, alog_ref, dtb_ref, proj_ref, gates_ref):
    hf = _norm_mod(x_ref[...], nw_ref[...], sc_ref[...], sh_ref[...])
    h = hf.astype(bf16)
    for c in range(A_MAIN_DIM // D_MODEL):
        sl = slice(c * D_MODEL, (c + 1) * D_MODEL)
        proj_ref[:, sl] = jnp.dot(h, wm_ref[:, sl], preferred_element_type=f32)
    gates_ref[...] = _gate_values(hf, ws_ref, alog_ref, dtb_ref)


def _a_in(x, mods, l, li, nw3, wm, ws, alog, dtb):
    G, R, _ = x.shape
    tm = min(R, 512)
    sub = 1
    return pl.pallas_call(
        _a_in_kernel,
        grid=(G, R // tm),
        in_specs=[
            _tok_spec(tm, D_MODEL),
            mods.spec(l, sub, 0, tm), mods.spec(l, sub, 1, tm),
            pl.BlockSpec((None, 1, D_MODEL), lambda g, i: (l * N_SUB + sub, 0, 0)),
            _resident((None, D_MODEL, A_MAIN_DIM), lambda g, i: (li, 0, 0)),
            _resident((None, D_MODEL, LANES), lambda g, i: (li, 0, 0)),
            pl.BlockSpec((None, 1, LANES), lambda g, i: (li, 0, 0)),
            pl.BlockSpec((None, 1, LANES), lambda g, i: (li, 0, 0)),
        ],
        out_specs=[_tok_spec(tm, A_MAIN_DIM), _tok_spec(tm, LANES)],
        out_shape=[jax.ShapeDtypeStruct((G, R, A_MAIN_DIM), f32),
                   jax.ShapeDtypeStruct((G, R, LANES), f32)],
        compiler_params=_params(("arbitrary", "arbitrary")),
        name="a_in",
    )(x, mods.arr, mods.arr, nw3, wm, ws, alog, dtb)


A_LT = 256
A_GROUP = 2


def _a_prompt_kernel(x_ref, sh_ref, sc_ref, gt_ref, nw_ref, wm_ref, ws_ref, alog_ref, dtb_ref, wc_ref,
                     won_ref, wo_ref, perm_ref, permt_ref, ptri_ref,
                     y_ref, s_ref, tail_ref, hist_ref, act_ref, o_scr, og_ref):
    t = pl.program_id(1)
    lt = x_ref.shape[0]
    nc = lt // CHUNK_A

    @pl.when(t == 0)
    def _():
        s_ref[...] = jnp.zeros(s_ref.shape, f32)
        hist_ref[...] = jnp.zeros(hist_ref.shape, f32)

    x = x_ref[...]
    hf = _norm_mod(x, nw_ref[...], sc_ref[...], sh_ref[...])
    h = hf.astype(bf16)

    ri = lax.broadcasted_iota(jnp.int32, (CHUNK_A, CHUNK_A), 0)
    ci = lax.broadcasted_iota(jnp.int32, (CHUNK_A, CHUNK_A), 1)
    pos_r = (ri % 8) * (CHUNK_A // 8) + ri // 8
    pos_c = (ci % 8) * (CHUNK_A // 8) + ci // 8
    causal = pos_r >= pos_c
    strict = pos_r > pos_c
    hp = jnp.dot(perm_ref[...], h, preferred_element_type=f32).astype(bf16)

    gates = _gate_values(hf, ws_ref, alog_ref, dtb_ref)
    lane = lax.broadcasted_iota(jnp.int32, gates.shape, 1)
    is_g = (lane >= A_HEADS) & (lane < 2 * A_HEADS)
    gparts = _split3(gates)
    sparts = _split3(pltpu.roll(jnp.where(is_g, gates, 0.0), A_HEADS, 1))
    ptri = ptri_ref[...]
    gcols, grows = [], []
    for c in range(nc):
        rs = slice(c * CHUNK_A, (c + 1) * CHUNK_A)
        gc = None
        for gp, sp in zip(gparts, sparts):
            term = jnp.dot(ptri, jnp.concatenate([gp[rs], sp[rs]], axis=0), preferred_element_type=f32)
            gc = term if gc is None else gc + term
        gcols.append(gc)
        grows.append(gc.T)

    ng = CHUNK_A // 8
    sub = lax.broadcasted_iota(jnp.int32, (8, 2 * A_DK), 0)
    pair = 2 * A_DK
    for p in range(A_HEADS // 2):
        for part in range(3):
            col = part * A_QK_DIM + p * pair
            cs = slice(col, col + pair)
            pre = jnp.dot(hp, wm_ref[:, cs], preferred_element_type=f32)
            grp = lambda c, i: pre[c * CHUNK_A + 8 * i:c * CHUNK_A + 8 * i + 8]
            acc = pre * wc_ref[CONV_W - 1:CONV_W, cs]
            shifted = [[] for _ in range(CONV_W)]
            for c in range(nc):
                w = []
                for j in range(CONV_W - 1):
                    i = ng - (CONV_W - 1) + j
                    prev = hist_ref[8 * j:8 * j + 8, cs] if c == 0 else grp(c - 1, i)
                    w.append(pltpu.roll(jnp.where(sub == 7, prev, grp(c, i)), 1, 0))
                for k_ in range(1, CONV_W):
                    shifted[k_] += w[CONV_W - 1 - k_:] + [pre[c * CHUNK_A:(c + 1) * CHUNK_A - 8 * k_]]
            hist_ref[:, cs] = pre[lt - 8 * (CONV_W - 1):]
            for k_ in range(1, CONV_W):
                acc = acc + jnp.concatenate(shifted[k_], axis=0) * wc_ref[CONV_W - 1 - k_:CONV_W - k_, cs]
            a2 = _silu(acc)
            for hh in range(2):
                a = a2[:, hh * A_DK:(hh + 1) * A_DK]
                if part == 0:
                    a = a * (lax.rsqrt(jnp.sum(a * a, axis=-1, keepdims=True) + EPS) * (A_DK ** -0.5))
                elif part == 1:
                    a = _l2n(a)
                act_ref[part * A_HEADS + 2 * p + hh] = a
    tail_ref[...] = hist_ref[...]

    z = jnp.dot(hp, wm_ref[:, A_CONV_DIM:A_MAIN_DIM], preferred_element_type=f32)
    won = won_ref[...]
    zg = [_silu(z[:, h_ * A_DV:(h_ + 1) * A_DV]) * won for h_ in range(A_HEADS)]

    states = [s_ref[h_] for h_ in range(A_HEADS)]

    def recurrence_pieces(c, insts):
        rs = slice(c * CHUNK_A, (c + 1) * CHUNK_A)
        tmp = {}

        def first():
            tmp["ws_qs"] = [jnp.dot(i["lhs1"], states[h_].astype(bf16), preferred_element_type=f32)
                            for h_, i in enumerate(insts)]

        def second():
            tmp["r"] = [jnp.dot(i["lhs2"], (i["u"] - w[:CHUNK_A]).astype(bf16), preferred_element_type=f32)
                        for i, w in zip(insts, tmp["ws_qs"])]

        def third():
            for h_, (i, w, r) in enumerate(zip(insts, tmp["ws_qs"], tmp["r"])):
                o_scr[rs, h_ * A_DV:(h_ + 1) * A_DV] = w[CHUNK_A:] + r[:CHUNK_A]
                states[h_] = states[h_] * i["edl"] + r[CHUNK_A:]

        return [first, second, third]

    pending = []
    for c0 in range(0, nc, A_GROUP):
        chunks = range(c0, min(c0 + A_GROUP, nc))

        inst = []
        for c in chunks:
            rs = slice(c * CHUNK_A, (c + 1) * CHUNK_A)
            for h_ in range(A_HEADS):
                qc = act_ref[h_, rs, :]
                kc = act_ref[A_HEADS + h_, rs, :]
                vc = act_ref[2 * A_HEADS + h_, rs, :]
                bcol = gcols[c][:, h_:h_ + 1]
                dcol = gcols[c][:, 2 * A_HEADS + h_:2 * A_HEADS + h_ + 1]
                drow = grows[c][2 * A_HEADS + h_:2 * A_HEADS + h_ + 1, :]
                dlast = dcol[CHUNK_A - 1:CHUNK_A, :]
                gam = jnp.where(causal, jnp.exp(jnp.where(causal, dcol - drow, 0.0)), 0.0)
                kb = kc * bcol
                ed = jnp.exp(dcol)
                kt = kc.T
                kq = _bdot(jnp.concatenate([kb, qc], axis=0), kt)
                m = jnp.where(strict, kq[:CHUNK_A] * gam, 0.0)
                attn = kq[CHUNK_A:] * gam
                kdec_t = kt * jnp.exp(dlast - drow)
                inst.append(dict(
                    m=m,
                    rhs=jnp.concatenate([vc * bcol, kb * ed], axis=1),
                    qed=(qc * ed).astype(bf16),
                    lhs2=jnp.concatenate([attn, kdec_t], axis=0).astype(bf16),
                    edl=jnp.exp(dlast)))

        ms = [i["m"] for i in inst]
        ts = [-m for m in ms]
        n = 2
        while n < CHUNK_A:
            ms = [_bdot(m, m) for m in ms]
            ts = [t + m + _bdot(t, m) for t, m in zip(ts, ms)]
            n *= 2
            if pending:
                pending.pop(0)()
        rs_ = [-(i["m"] + t + _dot3(i["m"], t)) for i, t in zip(inst, ts)]
        if pending:
            pending.pop(0)()
        ts = [t + r + _bdot(t, r) for t, r in zip(ts, rs_)]
        while pending:
            pending.pop(0)()

        for i, t in zip(inst, ts):
            uw = i["rhs"] + _bdot(t, i["rhs"])
            i["u"] = uw[:, :A_DV]
            i["lhs1"] = jnp.concatenate([uw[:, A_DV:].astype(bf16), i["qed"]], axis=0)
        for k_, c in enumerate(chunks):
            pending += recurrence_pieces(c, inst[k_ * A_HEADS:(k_ + 1) * A_HEADS])
    while pending:
        pending.pop(0)()
    for h_ in range(A_HEADS):
        s_ref[h_] = states[h_]

    for h_ in range(A_HEADS):
        sl = slice(h_ * A_DV, (h_ + 1) * A_DV)
        og_ref[:, sl] = (_rms(o_scr[:, sl]) * zg[h_]).astype(bf16)
    og = jnp.dot(permt_ref[...], og_ref[...], preferred_element_type=f32).astype(bf16)
    y = jnp.dot(og, wo_ref[...], preferred_element_type=f32)
    y_ref[...] = x + (1.0 + gt_ref[...]) * y


def _a_prompt(x, mods, l, li, nw3, wm, ws, alog, dtb, wconv, won, wo):
    B, L, _ = x.shape
    lt = A_LT
    sub = 1
    const = lambda shape: pl.BlockSpec((None,) + shape, lambda b, t: (li,) + (0,) * len(shape))
    r = np.arange(CHUNK_A)
    pc = (np.arange(CHUNK_A)[None, :] == ((r % 8) * (CHUNK_A // 8) + r // 8)[:, None]).astype(np.float32)
    perm_np = np.kron(np.eye(lt // CHUNK_A, dtype=np.float32), pc)
    perm = jnp.asarray(perm_np, dtype=bf16)
    permt = jnp.asarray(perm_np.T, dtype=bf16)
    ptri = jnp.asarray(np.concatenate([pc, pc @ np.tril(np.ones((CHUNK_A, CHUNK_A), np.float32))], axis=1),
                       dtype=bf16)
    hist_rows = 8 * (CONV_W - 1)
    return pl.pallas_call(
        _a_prompt_kernel,
        grid=(B, L // lt),
        in_specs=[
            _tok_spec(lt, D_MODEL),
            mods.spec(l, sub, 0, lt), mods.spec(l, sub, 1, lt), mods.spec(l, sub, 2, lt),
            pl.BlockSpec((None, 1, D_MODEL), lambda b, t: (l * N_SUB + sub, 0, 0)),
            _resident((None, D_MODEL, A_MAIN_DIM), lambda b, t: (li, 0, 0)),
            _resident((None, D_MODEL, LANES), lambda b, t: (li, 0, 0)),
            const((1, LANES)), const((1, LANES)),
            const((CONV_W, A_CONV_DIM)),
            const((1, A_DV)),
            _resident((None, A_V_DIM, D_MODEL), lambda b, t: (li, 0, 0)),
            _resident((lt, lt), lambda b, t: (0, 0)),
            _resident((lt, lt), lambda b, t: (0, 0)),
            _resident((CHUNK_A, 2 * CHUNK_A), lambda b, t: (0, 0)),
        ],
        out_specs=[
            _tok_spec(lt, D_MODEL),
            pl.BlockSpec((None, A_HEADS, A_DK, A_DV), lambda b, t: (b, 0, 0, 0)),
            pl.BlockSpec((None, hist_rows, A_CONV_DIM), lambda b, t: (b, 0, 0)),
        ],
        out_shape=[jax.ShapeDtypeStruct(x.shape, f32),
                   jax.ShapeDtypeStruct((B, A_HEADS, A_DK, A_DV), f32),
                   jax.ShapeDtypeStruct((B, hist_rows, A_CONV_DIM), f32)],
        scratch_shapes=[pltpu.VMEM((hist_rows, A_CONV_DIM), f32),
                        pltpu.VMEM((3 * A_HEADS, lt, A_DK), f32),
                        pltpu.VMEM((lt, A_V_DIM), f32),
                        pltpu.VMEM((lt, A_V_DIM), bf16)],
        compiler_params=_params(("arbitrary", "arbitrary")),
        name="a_prompt",
    )(x, mods.arr, mods.arr, mods.arr, nw3, wm, ws, alog, dtb, wconv, won, wo, perm, permt, ptri)


A_BT = 8


def _a_step_kernel(pre_ref, conv_ref, gates_ref, wc_ref, s_ref, *rest, li):
    if li:
        prev_ref, o_ref, so_all_ref = rest
        so_all_ref[0:li] = prev_ref[...]
    else:
        o_ref, so_all_ref = rest
    so_ref = so_all_ref.at[li]
    bt = pre_ref.shape[0]
    wc = wc_ref[...]
    acc = pre_ref[...] * wc[CONV_W - 1:CONV_W, :]
    for j in range(CONV_W - 1):
        acc = acc + conv_ref[:, j, :] * wc[j:j + 1, :]
    qkv = _silu(acc)
    gates = gates_ref[...]
    row = lax.broadcasted_iota(jnp.int32, (bt, 8, LANES), 1)
    for h in range(A_HEADS):
        q = _l2n(qkv[:, h * A_DK:(h + 1) * A_DK]) * (A_DK ** -0.5)
        k = _l2n(qkv[:, A_QK_DIM + h * A_DK:A_QK_DIM + (h + 1) * A_DK])
        v = qkv[:, 2 * A_QK_DIM + h * A_DV:2 * A_QK_DIM + (h + 1) * A_DV]
        beta = gates[:, h:h + 1]
        eg = jnp.exp(gates[:, A_HEADS + h:A_HEADS + h + 1])
        s = s_ref[:, h]
        kq = jnp.where(row == 0, k[:, None, :], jnp.where(row == 1, q[:, None, :], 0.0))
        ks = jnp.einsum('bmk,bkn->bmn', kq.astype(bf16), s.astype(bf16),
                        preferred_element_type=f32)
        v_new = beta * (v - eg * ks[:, 0, :])
        qk = jnp.sum(q * k, axis=-1, keepdims=True)
        o_ref[:, h * A_DV:(h + 1) * A_DV] = eg * ks[:, 1, :] + qk * v_new
        kh = k.astype(bf16).astype(f32)
        vh = v_new.astype(bf16).astype(f32)
        kl, vl = k - kh, v_new - vh
        lhs = jnp.where(row == 2, kl[:, None, :], jnp.where(row < 2, kh[:, None, :], 0.0))
        rhs = jnp.where(row == 1, vl[:, None, :], jnp.where(row < 3, vh[:, None, :], 0.0))
        outer = lax.dot_general(lhs.astype(bf16), rhs.astype(bf16), (((1,), (1,)), ((0,), (0,))),
                                preferred_element_type=f32)
        so_ref[:, h] = s * eg[:, :, None] + outer


def _a_step(proj, conv_state, gates, wconv, s0, li, s_prev):
    nb = proj.shape[1]
    bt = A_BT
    in_specs = [
        pl.BlockSpec((None, bt, A_CONV_DIM), lambda i: (0, i, 0)),
        pl.BlockSpec((None, bt, CONV_W - 1, A_CONV_DIM), lambda i: (li, i, 0, 0)),
        pl.BlockSpec((None, bt, LANES), lambda i: (0, i, 0)),
        pl.BlockSpec((None, CONV_W, A_CONV_DIM), lambda i: (li, 0, 0)),
        pl.BlockSpec((None, bt, A_HEADS, A_DK, A_DV), lambda i: (li, i, 0, 0, 0)),
    ]
    args = [proj, conv_state, gates, wconv, s0]
    if li:
        in_specs.append(pl.BlockSpec((li, bt, A_HEADS, A_DK, A_DV), lambda i: (0, i, 0, 0, 0)))
        args.append(s_prev)
    return pl.pallas_call(
        functools.partial(_a_step_kernel, li=li),
        grid=(nb // bt,),
        in_specs=in_specs,
        out_specs=[
            pl.BlockSpec((None, bt, A_V_DIM), lambda i: (0, i, 0)),
            pl.BlockSpec((li + 1, bt, A_HEADS, A_DK, A_DV), lambda i: (0, i, 0, 0, 0)),
        ],
        out_shape=[jax.ShapeDtypeStruct((1, nb, A_V_DIM), f32),
                   jax.ShapeDtypeStruct((li + 1,) + s0.shape[1:], f32)],
        compiler_params=_params(("arbitrary",)),
        name="a_step",
    )(*args)


def _a_out_kernel(x_ref, o_ref, z_ref, gt_ref, won_ref, wo_ref, y_ref, og_ref):
    won = won_ref[...]
    for h in range(A_HEADS):
        sl = slice(h * A_DV, (h + 1) * A_DV)
        og_ref[:, sl] = (_rms(o_ref[:, sl]) * won * _silu(z_ref[:, sl])).astype(bf16)
    y = jnp.dot(og_ref[...], wo_ref[...], preferred_element_type=f32)
    y_ref[...] = x_ref[...] + (1.0 + gt_ref[...]) * y


def _a_out(x, o, proj, mods, l, li, won, wo):
    G, R, _ = x.shape
    tm = min(R, 512)
    return pl.pallas_call(
        _a_out_kernel,
        grid=(G, R // tm),
        in_specs=[
            _tok_spec(tm, D_MODEL),
            _tok_spec(tm, A_V_DIM),
            _tok_spec(tm, A_V_DIM, A_CONV_DIM // A_V_DIM),
            mods.spec(l, 1, 2, tm),
            pl.BlockSpec((None, 1, A_DV), lambda g, i: (li, 0, 0)),
            _resident((None, A_V_DIM, D_MODEL), lambda g, i: (li, 0, 0)),
        ],
        out_specs=_tok_spec(tm, D_MODEL),
        out_shape=jax.ShapeDtypeStruct(x.shape, f32),
        scratch_shapes=[pltpu.VMEM((tm, A_V_DIM), bf16)],
        compiler_params=_params(("arbitrary", "arbitrary")),
        name="a_out",
    )(x, o, proj, mods.arr, won, wo)


def _b_kernel(x_ref, sh_ref, sc_ref, gt_ref, nw_ref, wi_ref, vw_ref, vb_ref, ws_ref, bs_ref, wo_ref,
              *rest, decode):
    if decode:
        y_ref, vout_ref, vn_ref, gd_ref = rest
    else:
        y_ref, vn_ref, gd_ref = rest
    tm = x_ref.shape[0]
    x = x_ref[...]
    h = _norm_mod(x, nw_ref[...], sc_ref[...], sh_ref[...]).astype(bf16)
    v = _gelu(jnp.dot(h, wi_ref[:, B_WIDTH:], preferred_element_type=f32))
    mu = jnp.mean(v, axis=-1, keepdims=True)
    vc = v - mu
    vn = vc * lax.rsqrt(jnp.mean(vc * vc, axis=-1, keepdims=True) + EPS) * vw_ref[...] + vb_ref[...]
    if decode:
        vout_ref[...] = vn
    else:
        vn_ref[...] = vn.astype(bf16)
        ri = lax.broadcasted_iota(jnp.int32, (CHUNK_B, CHUNK_B), 0)
        ci = lax.broadcasted_iota(jnp.int32, (CHUNK_B, CHUNK_B), 1)
        causal = ri >= ci
    for g in range(B_GROUPS):
        gs = slice(g * B_GROUP_DIM, (g + 1) * B_GROUP_DIM)
        u = _gelu(jnp.dot(h, wi_ref[:, gs], preferred_element_type=f32))
        if decode:
            gd_ref[:, gs] = (u * (ws_ref[:, gs] * vn[:, gs] + bs_ref[:, gs])).astype(bf16)
        else:
            wsg = jnp.where(causal, ws_ref[g], 0.0).astype(bf16)
            bias = bs_ref[:, g:g + 1]
            for c in range(tm // CHUNK_B):
                rs = slice(c * CHUNK_B, (c + 1) * CHUNK_B)
                mixed = jnp.dot(wsg, vn_ref[rs, gs], preferred_element_type=f32) + bias
                gd_ref[rs, gs] = (u[rs] * mixed).astype(bf16)
    y = jnp.dot(gd_ref[...], wo_ref[...], preferred_element_type=f32)
    y_ref[...] = x + (1.0 + gt_ref[...]) * y


def _b_mixer(x, mods, l, li, nw3, wi, vw, vb, ws, bs, wo, decode):
    G, R, _ = x.shape
    tm = min(R, 512)
    sub = 1
    if decode:
        ws_spec = pl.BlockSpec((None, 1, B_WIDTH), lambda g, i: (li, 0, 0))
        bs_spec = pl.BlockSpec((None, 1, B_WIDTH), lambda g, i: (li, 0, 0))
        out_specs = [_tok_spec(tm, D_MODEL), _tok_spec(tm, B_WIDTH)]
        out_shape = [jax.ShapeDtypeStruct(x.shape, f32), jax.ShapeDtypeStruct((G, R, B_WIDTH), f32)]
    else:
        ws_spec = pl.BlockSpec((None, B_GROUPS, CHUNK_B, CHUNK_B), lambda g, i: (li, 0, 0, 0))
        bs_spec = pl.BlockSpec((None, CHUNK_B, B_GROUPS), lambda g, i: (li, 0, 0))
        out_specs = _tok_spec(tm, D_MODEL)
        out_shape = jax.ShapeDtypeStruct(x.shape, f32)
    return pl.pallas_call(
        functools.partial(_b_kernel, decode=decode),
        grid=(G, R // tm),
        in_specs=[
            _tok_spec(tm, D_MODEL),
            mods.spec(l, sub, 0, tm), mods.spec(l, sub, 1, tm), mods.spec(l, sub, 2, tm),
            pl.BlockSpec((None, 1, D_MODEL), lambda g, i: (l * N_SUB + sub, 0, 0)),
            _resident((None, D_MODEL, 2 * B_WIDTH), lambda g, i: (li, 0, 0)),
            pl.BlockSpec((None, 1, B_WIDTH), lambda g, i: (li, 0, 0)),
            pl.BlockSpec((None, 1, B_WIDTH), lambda g, i: (li, 0, 0)),
            ws_spec, bs_spec,
            _resident((None, B_WIDTH, D_MODEL), lambda g, i: (li, 0, 0)),
        ],
        out_specs=out_specs,
        out_shape=out_shape,
        scratch_shapes=[pltpu.VMEM((tm, B_WIDTH), bf16), pltpu.VMEM((tm, B_WIDTH), bf16)],
        compiler_params=_params(("arbitrary", "arbitrary")),
        name="b_mixer",
    )(x, mods.arr, mods.arr, mods.arr, nw3, wi, vw, vb, ws, bs, wo)


def _trunk(x, mods, decode, conv_state, s_state, W):
    new_conv, new_s, v_rows = [], [], []
    G, R, _ = x.shape
    for i in range(DEPTH):
        li = i // N_MIXERS
        x = _ffn(x, mods, i, 0, 0, W["nw3"], W["wgu"], W["wdn"])
        if i % N_MIXERS == 0:
            if decode:
                proj, gates = _a_in(x, mods, i, li, W["nw3"], W["a_wm"], W["a_ws"], W["a_alog"], W["a_dtb"])
                o, s_new = _a_step(proj, conv_state, gates, W["a_wconv"], s_state, li,
                                   new_s[-1] if new_s else None)
                pre = proj[0, :, None, :A_CONV_DIM]
                new_conv.append(jnp.concatenate([conv_state[li][:, 1:], pre], axis=1))
                x = _a_out(x, o, proj, mods, i, li, W["a_won"], W["a_wo"])
            else:
                x, s_new, tail = _a_prompt(x, mods, i, li, W["nw3"], W["a_wm"], W["a_ws"], W["a_alog"],
                                           W["a_dtb"], W["a_wconv"], W["a_won"], W["a_wo"])
                new_conv.append(tail[:, 7::8])
            new_s.append(s_new)
        else:
            if decode:
                x, v = _b_mixer(x, mods, i, li, W["nw3"], W["b_wi"], W["b_vw"], W["b_vb"],
                                W["b_ws00"], W["b_bs0"], W["b_wo"], decode=True)
                v_rows.append(v)
            else:
                x = _b_mixer(x, mods, i, li, W["nw3"], W["b_wi"], W["b_vw"], W["b_vb"],
                             W["b_ws"], W["b_bsT"], W["b_wo"], decode=False)
        fw = W["final_w"] if i == DEPTH - 1 else None
        x = _ffn(x, mods, i, 1, 2, W["nw3"], W["wgu"], W["wdn"], final_w=fw)
    return x, new_conv, new_s, v_rows


def kernel(x_prompt, x_sample, state_a_conv, state_a_S, c_prompt, c_sample, w_ada, b_ada, norm_w, ffn_w_gu, ffn_w_down, a_w_in, a_w_conv, a_log, a_dt_bias, a_w_onorm, a_w_out, b_w_in, b_vnorm_w, b_vnorm_b, b_w_s, b_b_s, b_w_out, final_norm_w):
    n_a = a_w_in.shape[0]
    n_b = b_w_in.shape[0]
    bp, seq, _ = x_prompt.shape
    bs = x_sample.shape[0]
    assert x_sample.shape[1] == 1 and seq % A_LT == 0 and seq % CHUNK_B == 0

    mod_p, mod_s = _ada(c_prompt, c_sample, w_ada, b_ada)
    mods_p = _Mods(mod_p.reshape(DEPTH, bp, 1, N_SUB * 3 * D_MODEL), per_row=False)
    mods_s = _Mods(mod_s.reshape(DEPTH, 1, bs, N_SUB * 3 * D_MODEL), per_row=True)

    pad = LANES - 2 * A_HEADS
    gate_pad = lambda a: jnp.pad(a, ((0, 0), (A_HEADS, pad)))[:, None, :]
    W = {
        "nw3": norm_w.reshape(DEPTH * N_SUB, 1, D_MODEL),
        "wgu": ffn_w_gu.astype(bf16),
        "wdn": ffn_w_down.astype(bf16),
        "a_wm": a_w_in.astype(bf16),
        "a_ws": jnp.pad(a_w_in[:, :, A_MAIN_DIM:], ((0, 0), (0, 0), (0, pad))),
        "a_alog": gate_pad(a_log),
        "a_dtb": gate_pad(a_dt_bias),
        "a_wconv": a_w_conv,
        "a_won": a_w_onorm[:, None, :],
        "a_wo": a_w_out.astype(bf16),
        "b_wi": b_w_in.astype(bf16),
        "b_vw": b_vnorm_w[:, None, :],
        "b_vb": b_vnorm_b[:, None, :],
        "b_ws": b_w_s,
        "b_bsT": jnp.swapaxes(b_b_s, 1, 2),
        "b_ws00": jnp.repeat(b_w_s[:, :, 0, 0], B_GROUP_DIM, axis=1)[:, None, :],
        "b_bs0": jnp.repeat(b_b_s[:, :, 0], B_GROUP_DIM, axis=1)[:, None, :],
        "b_wo": b_w_out.astype(bf16),
        "final_w": final_norm_w,
    }

    y_p, conv_p, s_p, _ = _trunk(x_prompt, mods_p, False, None, None, W)
    y_s, conv_s, s_s, v_s = _trunk(x_sample.reshape(1, bs, D_MODEL), mods_s, True,
                                   state_a_conv, state_a_S, W)
    return (y_p,
            y_s.reshape(bs, 1, D_MODEL),
            jnp.stack(conv_p),
            jnp.stack(s_p),
            jnp.stack(conv_s),
            s_s[-1],
            jnp.stack(v_s).reshape(n_b, bs, 1, B_WIDTH))
```

```python
import functools

import jax
import jax.numpy as jnp
import numpy as np
from jax import lax
from jax.experimental import pallas as pl
from jax.experimental.pallas import tpu as pltpu

f32 = jnp.float32
bf16 = jnp.bfloat16

D_MODEL = 1024
DEPTH = 4
N_MIXERS = 2
A_HEADS = 8
A_DK = 128
A_DV = 128
A_QK_DIM = A_HEADS * A_DK
A_V_DIM = A_HEADS * A_DV
A_CONV_DIM = 2 * A_QK_DIM + A_V_DIM
A_MAIN_DIM = A_CONV_DIM + A_V_DIM
CONV_W = 4
CHUNK_A = 64
B_WIDTH = 2 * D_MODEL
B_GROUPS = 8
B_GROUP_DIM = B_WIDTH // B_GROUPS
CHUNK_B = 128
D_FF = 2816
FFN_RES = 0.5
N_SUB = 3
EPS = 1e-6

LANES = 128
FF_TILE = 256
GATE_COLS = 3 * A_HEADS
VMEM_LIMIT = 52 * 1024 * 1024


def _params(sem):
    return pltpu.CompilerParams(dimension_semantics=sem, vmem_limit_bytes=VMEM_LIMIT)


def _resident(shape, index_map):
    return pl.BlockSpec(shape, index_map, pipeline_mode=pl.Buffered(1))


def _bdot(a, b):
    return jnp.dot(a.astype(bf16), b.astype(bf16), preferred_element_type=f32)


def _bdot_nt(a, b):
    return lax.dot_general(a.astype(bf16), b.astype(bf16), (((1,), (1,)), ((), ())),
                           preferred_element_type=f32)


def _bdot_tn(a, b):
    return lax.dot_general(a.astype(bf16), b.astype(bf16), (((0,), (0,)), ((), ())),
                           preferred_element_type=f32)


def _split2(a):
    hi = a.astype(bf16)
    lo = (a - hi.astype(f32)).astype(bf16)
    return hi, lo


def _split3(a):
    hi = a.astype(bf16)
    r = a - hi.astype(f32)
    mid = r.astype(bf16)
    lo = (r - mid.astype(f32)).astype(bf16)
    return hi, mid, lo


def _dot3(a, b):
    ah, al = _split2(a)
    bh, bl = _split2(b)
    d = functools.partial(jnp.dot, preferred_element_type=f32)
    return d(ah, bh) + d(al, bh) + d(ah, bl)


LOG2E = 1.4426950408889634


def _silu(x):
    return x / (1.0 + jnp.exp2(x * (-LOG2E)))


def _softplus(x):
    return jnp.maximum(x, 0.0) + jnp.log1p(jnp.exp(-jnp.abs(x)))


GELU_A = -LOG2E * 2.0 * (2.0 / 3.141592653589793) ** 0.5
GELU_B = GELU_A * 0.044715


def _gelu(x):
    return x / (1.0 + jnp.exp2(x * (GELU_A + GELU_B * (x * x))))


def _rms(x):
    return x * lax.rsqrt(jnp.mean(x * x, axis=-1, keepdims=True) + EPS)


def _norm_mod(x, nw, sc, sh):
    return (_rms(x) * nw) * (1.0 + sc) + sh


def _l2n(x):
    return x * lax.rsqrt(jnp.sum(x * x, axis=-1, keepdims=True) + EPS)


def _ada_kernel(cp_ref, cs_ref, w_ref, b_ref, op_ref, os_ref):
    w = w_ref[...].astype(bf16)
    b = b_ref[...]
    op_ref[...] = _bdot(_silu(cp_ref[...]), w) + b
    os_ref[...] = _bdot(_silu(cs_ref[...]), w) + b


def _ada(c_p, c_s, w_ada, b_ada):
    n = N_SUB * 3 * D_MODEL
    tn = 9 * LANES
    bp, bs = c_p.shape[0], c_s.shape[0]
    return pl.pallas_call(
        _ada_kernel,
        grid=(DEPTH, n // tn),
        in_specs=[
            pl.BlockSpec((bp, D_MODEL), lambda l, j: (0, 0)),
            pl.BlockSpec((bs, D_MODEL), lambda l, j: (0, 0)),
            pl.BlockSpec((None, D_MODEL, tn), lambda l, j: (l, 0, j)),
            pl.BlockSpec((None, 1, tn), lambda l, j: (l, 0, j)),
        ],
        out_specs=[
            pl.BlockSpec((None, bp, tn), lambda l, j: (l, 0, j)),
            pl.BlockSpec((None, bs, tn), lambda l, j: (l, 0, j)),
        ],
        out_shape=[jax.ShapeDtypeStruct((DEPTH, bp, n), f32),
                   jax.ShapeDtypeStruct((DEPTH, bs, n), f32)],
        compiler_params=_params(("arbitrary", "arbitrary")),
        name="ada",
    )(c_p, c_s, w_ada, b_ada.reshape(DEPTH, 1, n))


class _Mods:
    def __init__(self, arr, per_row):
        self.arr = arr
        self.per_row = per_row

    def spec(self, l, sub, kind, tm):
        col = sub * 3 + kind
        if self.per_row:
            return pl.BlockSpec((None, None, tm, D_MODEL), lambda g, i: (l, g, i, col))
        return pl.BlockSpec((None, None, 1, D_MODEL), lambda g, i: (l, g, 0, col))


def _tok_spec(tm, width, colblk=0):
    return pl.BlockSpec((None, tm, width), lambda g, i: (g, i, colblk))


def _ffn_kernel(xp_ref, shp_ref, scp_ref, gtp_ref, xs_ref, shs_ref, scs_ref, gts_ref,
                nw_ref, wg_ref, wu_ref, wd_ref, *rest, final, n_prompt):
    if final:
        fw_ref, op_ref, os_ref, act_ref = rest
    else:
        op_ref, os_ref, act_ref = rest

    def body(x_ref, sh_ref, sc_ref, gt_ref, o_ref):
        rows = x_ref.shape[0]
        x = x_ref[...]
        h = _norm_mod(x, nw_ref[...], sc_ref[...], sh_ref[...]).astype(bf16)
        for c in range(D_FF // FF_TILE):
            sl = slice(c * FF_TILE, (c + 1) * FF_TILE)
            g = jnp.dot(h, wg_ref[:, sl], preferred_element_type=f32)
            u = jnp.dot(h, wu_ref[:, sl], preferred_element_type=f32)
            act_ref[0:rows, sl] = (_silu(g) * u).astype(bf16)
        y = jnp.dot(act_ref[0:rows, :], wd_ref[...], preferred_element_type=f32)
        out = x + FFN_RES * (1.0 + gt_ref[...]) * y
        if final:
            out = _rms(out) * fw_ref[...]
        o_ref[...] = out

    i = pl.program_id(0)
    pl.when(i < n_prompt)(lambda: body(xp_ref, shp_ref, scp_ref, gtp_ref, op_ref))
    pl.when(i == n_prompt)(lambda: body(xs_ref, shs_ref, scs_ref, gts_ref, os_ref))


def _ffn(xp, mods_p, xs, mods_s, l, j, sub, nw3, wgu, wdn, final_w=None):
    B, L, _ = xp.shape
    bs = xs.shape[1]
    tm = min(L, 1024)
    tps = L // tm
    n_prompt = B * tps
    final = final_w is not None
    tile = lambda i: jnp.minimum(i, n_prompt - 1)
    pmod = lambda kind: pl.BlockSpec((None, None, 1, D_MODEL),
                                     lambda i: (l, tile(i) // tps, 0, sub * 3 + kind))
    smod = lambda kind: pl.BlockSpec((None, None, bs, D_MODEL), lambda i: (l, 0, 0, sub * 3 + kind))
    in_specs = [
        pl.BlockSpec((None, tm, D_MODEL), lambda i: (tile(i) // tps, tile(i) % tps, 0)),
        pmod(0), pmod(1), pmod(2),
        pl.BlockSpec((None, bs, D_MODEL), lambda i: (0, 0, 0)),
        smod(0), smod(1), smod(2),
        pl.BlockSpec((None, 1, D_MODEL), lambda i: (l * N_SUB + sub, 0, 0)),
        _resident((None, None, D_MODEL, D_FF), lambda i: (l, j, 0, 0)),
        _resident((None, None, D_MODEL, D_FF), lambda i: (l, j, 0, 1)),
        _resident((None, None, D_FF, D_MODEL), lambda i: (l, j, 0, 0)),
    ]
    args = [xp, mods_p.arr, mods_p.arr, mods_p.arr, xs, mods_s.arr, mods_s.arr, mods_s.arr,
            nw3, wgu, wgu, wdn]
    if final:
        in_specs.append(pl.BlockSpec((1, D_MODEL), lambda i: (0, 0)))
        args.append(final_w.reshape(1, D_MODEL))
    return pl.pallas_call(
        functools.partial(_ffn_kernel, final=final, n_prompt=n_prompt),
        grid=(n_prompt + 1,),
        in_specs=in_specs,
        out_specs=[
            pl.BlockSpec((None, tm, D_MODEL), lambda i: (tile(i) // tps, tile(i) % tps, 0)),
            pl.BlockSpec((None, bs, D_MODEL), lambda i: (0, 0, 0)),
        ],
        out_shape=[jax.ShapeDtypeStruct(xp.shape, f32), jax.ShapeDtypeStruct(xs.shape, f32)],
        scratch_shapes=[pltpu.VMEM((tm, D_FF), bf16)],
        compiler_params=_params(("arbitrary",)),
        name="ffn",
    )(*args)


def _gate_values(hf, ws_ref, alog_ref, dtb_ref):
    p = _dot3(hf, ws_ref[...])
    lane = lax.broadcasted_iota(jnp.int32, p.shape, 1)
    beta = jax.nn.sigmoid(p)
    g = -jnp.exp(alog_ref[...]) * _softplus(p + dtb_ref[...])
    is_g = (lane >= A_HEADS) & (lane < 2 * A_HEADS)
    return jnp.where(lane < A_HEADS, beta, jnp.where(is_g, g, 0.0))


def _a_in_kernel(x_ref, sh_ref, sc_ref, nw_ref, wm_ref, ws_ref, alog_ref, dtb_ref, proj_ref, gates_ref):
    hf = _norm_mod(x_ref[...], nw_ref[...], sc_ref[...], sh_ref[...])
    h = hf.astype(bf16)
    for c in range(A_MAIN_DIM // D_MODEL):
        sl = slice(c * D_MODEL, (c + 1) * D_MODEL)
        proj_ref[:, sl] = jnp.dot(h, wm_ref[:, sl], preferred_element_type=f32)
    gates_ref[...] = _gate_values(hf, ws_ref, alog_ref, dtb_ref)


def _a_in(x, mods, l, li, nw3, wm, ws, alog, dtb):
    G, R, _ = x.shape
    tm = min(R, 512)
    sub = 1
    return pl.pallas_call(
        _a_in_kernel,
        grid=(G, R // tm),
        in_specs=[
            _tok_spec(tm, D_MODEL),
            mods.spec(l, sub, 0, tm), mods.spec(l, sub, 1, tm),
            pl.BlockSpec((None, 1, D_MODEL), lambda g, i: (l * N_SUB + sub, 0, 0)),
            _resident((None, D_MODEL, A_MAIN_DIM), lambda g, i: (li, 0, 0)),
            _resident((None, D_MODEL, LANES), lambda g, i: (li, 0, 0)),
            pl.BlockSpec((None, 1, LANES), lambda g, i: (li, 0, 0)),
            pl.BlockSpec((None, 1, LANES), lambda g, i: (li, 0, 0)),
        ],
        out_specs=[_tok_spec(tm, A_MAIN_DIM), _tok_spec(tm, LANES)],
        out_shape=[jax.ShapeDtypeStruct((G, R, A_MAIN_DIM), f32),
                   jax.ShapeDtypeStruct((G, R, LANES), f32)],
        compiler_params=_params(("arbitrary", "arbitrary")),
        name="a_in",
    )(x, mods.arr, mods.arr, nw3, wm, ws, alog, dtb)


A_LT = 256
A_GROUP = 2


def _a_prompt_kernel(x_ref, sh_ref, sc_ref, gt_ref, nw_ref, wm_ref, ws_ref, alog_ref, dtb_ref, wc_ref,
                     won_ref, wo_ref, perm_ref, permt_ref, ptri_ref,
                     y_ref, s_ref, tail_ref, hist_ref, act_ref, o_scr, og_ref):
    t = pl.program_id(1)
    lt = x_ref.shape[0]
    nc = lt // CHUNK_A

    @pl.when(t == 0)
    def _():
        s_ref[...] = jnp.zeros(s_ref.shape, f32)
        hist_ref[...] = jnp.zeros(hist_ref.shape, f32)

    x = x_ref[...]
    hf = _norm_mod(x, nw_ref[...], sc_ref[...], sh_ref[...])
    h = hf.astype(bf16)

    ri = lax.broadcasted_iota(jnp.int32, (CHUNK_A, CHUNK_A), 0)
    ci = lax.broadcasted_iota(jnp.int32, (CHUNK_A, CHUNK_A), 1)
    pos_r = (ri % 8) * (CHUNK_A // 8) + ri // 8
    pos_c = (ci % 8) * (CHUNK_A // 8) + ci // 8
    causal = pos_r >= pos_c
    strict = pos_r > pos_c
    hp = jnp.dot(perm_ref[...], h, preferred_element_type=f32).astype(bf16)

    gates = _gate_values(hf, ws_ref, alog_ref, dtb_ref)
    lane = lax.broadcasted_iota(jnp.int32, gates.shape, 1)
    is_g = (lane >= A_HEADS) & (lane < 2 * A_HEADS)
    gparts = _split3(gates)
    sparts = _split3(pltpu.roll(jnp.where(is_g, gates, 0.0), A_HEADS, 1))
    ptri = ptri_ref[...]
    gcols, grows = [], []
    for c in range(nc):
        rs = slice(c * CHUNK_A, (c + 1) * CHUNK_A)
        gc = None
        for gp, sp in zip(gparts, sparts):
            term = jnp.dot(ptri, jnp.concatenate([gp[rs], sp[rs]], axis=0), preferred_element_type=f32)
            gc = term if gc is None else gc + term
        gcols.append(gc)
        grows.append(gc.T)

    ng = CHUNK_A // 8
    sub = lax.broadcasted_iota(jnp.int32, (8, 2 * A_DK), 0)
    pair = 2 * A_DK
    for p in range(A_HEADS // 2):
        for part in range(3):
            col = part * A_QK_DIM + p * pair
            cs = slice(col, col + pair)
            pre = jnp.dot(hp, wm_ref[:, cs], preferred_element_type=f32)
            grp = lambda c, i: pre[c * CHUNK_A + 8 * i:c * CHUNK_A + 8 * i + 8]
            acc = pre * wc_ref[CONV_W - 1:CONV_W, cs]
            shifted = [[] for _ in range(CONV_W)]
            for c in range(nc):
                w = []
                for j in range(CONV_W - 1):
                    i = ng - (CONV_W - 1) + j
                    prev = hist_ref[8 * j:8 * j + 8, cs] if c == 0 else grp(c - 1, i)
                    w.append(pltpu.roll(jnp.where(sub == 7, prev, grp(c, i)), 1, 0))
                for k_ in range(1, CONV_W):
                    shifted[k_] += w[CONV_W - 1 - k_:] + [pre[c * CHUNK_A:(c + 1) * CHUNK_A - 8 * k_]]
            hist_ref[:, cs] = pre[lt - 8 * (CONV_W - 1):]
            for k_ in range(1, CONV_W):
                acc = acc + jnp.concatenate(shifted[k_], axis=0) * wc_ref[CONV_W - 1 - k_:CONV_W - k_, cs]
            a2 = _silu(acc)
            for hh in range(2):
                a = a2[:, hh * A_DK:(hh + 1) * A_DK]
                if part == 0:
                    a = a * (lax.rsqrt(jnp.sum(a * a, axis=-1, keepdims=True) + EPS) * (A_DK ** -0.5))
                elif part == 1:
                    a = _l2n(a)
                act_ref[part * A_HEADS + 2 * p + hh] = a
    tail_ref[...] = hist_ref[...]

    z = jnp.dot(hp, wm_ref[:, A_CONV_DIM:A_MAIN_DIM], preferred_element_type=f32)
    won = won_ref[...]
    zg = [_silu(z[:, h_ * A_DV:(h_ + 1) * A_DV]) * won for h_ in range(A_HEADS)]

    states = [s_ref[h_] for h_ in range(A_HEADS)]

    def recurrence_pieces(c, insts):
        rs = slice(c * CHUNK_A, (c + 1) * CHUNK_A)
        tmp = {}

        def first():
            tmp["ws_qs"] = [jnp.dot(i["lhs1"], states[h_].astype(bf16), preferred_element_type=f32)
                            for h_, i in enumerate(insts)]

        def second():
            tmp["r"] = [jnp.dot(i["lhs2"], (i["u"] - w[:CHUNK_A]).astype(bf16), preferred_element_type=f32)
                        for i, w in zip(insts, tmp["ws_qs"])]

        def third():
            for h_, (i, w, r) in enumerate(zip(insts, tmp["ws_qs"], tmp["r"])):
                o_scr[rs, h_ * A_DV:(h_ + 1) * A_DV] = w[CHUNK_A:] + r[:CHUNK_A]
                states[h_] = states[h_] * i["edl"] + r[CHUNK_A:]

        return [first, second, third]

    pending = []
    for c0 in range(0, nc, A_GROUP):
        chunks = range(c0, min(c0 + A_GROUP, nc))

        inst = []
        for c in chunks:
            rs = slice(c * CHUNK_A, (c + 1) * CHUNK_A)
            for h_ in range(A_HEADS):
                qc = act_ref[h_, rs, :]
                kc = act_ref[A_HEADS + h_, rs, :]
                vc = act_ref[2 * A_HEADS + h_, rs, :]
                bcol = gcols[c][:, h_:h_ + 1]
                dcol = gcols[c][:, 2 * A_HEADS + h_:2 * A_HEADS + h_ + 1]
                drow = grows[c][2 * A_HEADS + h_:2 * A_HEADS + h_ + 1, :]
                dlast = dcol[CHUNK_A - 1:CHUNK_A, :]
                gam = jnp.where(causal, jnp.exp(jnp.where(causal, dcol - drow, 0.0)), 0.0)
                kb = kc * bcol
                ed = jnp.exp(dcol)
                kt = kc.T
                kq = _bdot(jnp.concatenate([kb, qc], axis=0), kt)
                m = jnp.where(strict, kq[:CHUNK_A] * gam, 0.0)
                attn = kq[CHUNK_A:] * gam
                kdec_t = kt * jnp.exp(dlast - drow)
                inst.append(dict(
                    m=m,
                    rhs=jnp.concatenate([vc * bcol, kb * ed], axis=1),
                    qed=(qc * ed).astype(bf16),
                    lhs2=jnp.concatenate([attn, kdec_t], axis=0).astype(bf16),
                    edl=jnp.exp(dlast)))

        ms = [i["m"] for i in inst]
        ts = [-m for m in ms]
        n = 2
        while n < CHUNK_A:
            ms = [_bdot(m, m) for m in ms]
            ts = [t + m + _bdot(t, m) for t, m in zip(ts, ms)]
            n *= 2
            if pending:
                pending.pop(0)()
        rs_ = [-(i["m"] + t + _dot3(i["m"], t)) for i, t in zip(inst, ts)]
        if pending:
            pending.pop(0)()
        ts = [t + r + _bdot(t, r) for t, r in zip(ts, rs_)]
        while pending:
            pending.pop(0)()

        for i, t in zip(inst, ts):
            uw = i["rhs"] + _bdot(t, i["rhs"])
            i["u"] = uw[:, :A_DV]
            i["lhs1"] = jnp.concatenate([uw[:, A_DV:].astype(bf16), i["qed"]], axis=0)
        for k_, c in enumerate(chunks):
            pending += recurrence_pieces(c, inst[k_ * A_HEADS:(k_ + 1) * A_HEADS])
    while pending:
        pending.pop(0)()
    for h_ in range(A_HEADS):
        s_ref[h_] = states[h_]

    for h_ in range(A_HEADS):
        sl = slice(h_ * A_DV, (h_ + 1) * A_DV)
        og_ref[:, sl] = (_rms(o_scr[:, sl]) * zg[h_]).astype(bf16)
    og = jnp.dot(permt_ref[...], og_ref[...], preferred_element_type=f32).astype(bf16)
    y = jnp.dot(og, wo_ref[...], preferred_element_type=f32)
    y_ref[...] = x + (1.0 + gt_ref[...]) * y


def _a_prompt(x, mods, l, li, nw3, wm, ws, alog, dtb, wconv, won, wo):
    B, L, _ = x.shape
    lt = A_LT
    sub = 1
    const = lambda shape: pl.BlockSpec((None,) + shape, lambda b, t: (li,) + (0,) * len(shape))
    r = np.arange(CHUNK_A)
    pc = (np.arange(CHUNK_A)[None, :] == ((r % 8) * (CHUNK_A // 8) + r // 8)[:, None]).astype(np.float32)
    perm_np = np.kron(np.eye(lt // CHUNK_A, dtype=np.float32), pc)
    perm = jnp.asarray(perm_np, dtype=bf16)
    permt = jnp.asarray(perm_np.T, dtype=bf16)
    ptri = jnp.asarray(np.concatenate([pc, pc @ np.tril(np.ones((CHUNK_A, CHUNK_A), np.float32))], axis=1),
                       dtype=bf16)
    hist_rows = 8 * (CONV_W - 1)
    return pl.pallas_call(
        _a_prompt_kernel,
        grid=(B, L // lt),
        in_specs=[
            _tok_spec(lt, D_MODEL),
            mods.spec(l, sub, 0, lt), mods.spec(l, sub, 1, lt), mods.spec(l, sub, 2, lt),
            pl.BlockSpec((None, 1, D_MODEL), lambda b, t: (l * N_SUB + sub, 0, 0)),
            _resident((None, D_MODEL, A_MAIN_DIM), lambda b, t: (li, 0, 0)),
            _resident((None, D_MODEL, LANES), lambda b, t: (li, 0, 0)),
            const((1, LANES)), const((1, LANES)),
            const((CONV_W, A_CONV_DIM)),
            const((1, A_DV)),
            _resident((None, A_V_DIM, D_MODEL), lambda b, t: (li, 0, 0)),
            _resident((lt, lt), lambda b, t: (0, 0)),
            _resident((lt, lt), lambda b, t: (0, 0)),
            _resident((CHUNK_A, 2 * CHUNK_A), lambda b, t: (0, 0)),
        ],
        out_specs=[
            _tok_spec(lt, D_MODEL),
            pl.BlockSpec((None, A_HEADS, A_DK, A_DV), lambda b, t: (b, 0, 0, 0)),
            pl.BlockSpec((None, hist_rows, A_CONV_DIM), lambda b, t: (b, 0, 0)),
        ],
        out_shape=[jax.ShapeDtypeStruct(x.shape, f32),
                   jax.ShapeDtypeStruct((B, A_HEADS, A_DK, A_DV), f32),
                   jax.ShapeDtypeStruct((B, hist_rows, A_CONV_DIM), f32)],
        scratch_shapes=[pltpu.VMEM((hist_rows, A_CONV_DIM), f32),
                        pltpu.VMEM((3 * A_HEADS, lt, A_DK), f32),
                        pltpu.VMEM((lt, A_V_DIM), f32),
                        pltpu.VMEM((lt, A_V_DIM), bf16)],
        compiler_params=_params(("arbitrary", "arbitrary")),
        name="a_prompt",
    )(x, mods.arr, mods.arr, mods.arr, nw3, wm, ws, alog, dtb, wconv, won, wo, perm, permt, ptri)


A_BT = 8


def _a_step_kernel(pre_ref, conv_ref, gates_ref, wc_ref, s_ref, *rest, li):
    if li:
        prev_ref, o_ref, so_all_ref = rest
        so_all_ref[0:li] = prev_ref[...]
    else:
        o_ref, so_all_ref = rest
    so_ref = so_all_ref.at[li]
    bt = pre_ref.shape[0]
    wc = wc_ref[...]
    acc = pre_ref[...] * wc[CONV_W - 1:CONV_W, :]
    for j in range(CONV_W - 1):
        acc = acc + conv_ref[:, j, :] * wc[j:j + 1, :]
    qkv = _silu(acc)
    gates = gates_ref[...]
    row = lax.broadcasted_iota(jnp.int32, (bt, 8, LANES), 1)
    for h in range(A_HEADS):
        q = _l2n(qkv[:, h * A_DK:(h + 1) * A_DK]) * (A_DK ** -0.5)
        k = _l2n(qkv[:, A_QK_DIM + h * A_DK:A_QK_DIM + (h + 1) * A_DK])
        v = qkv[:, 2 * A_QK_DIM + h * A_DV:2 * A_QK_DIM + (h + 1) * A_DV]
        beta = gates[:, h:h + 1]
        eg = jnp.exp(gates[:, A_HEADS + h:A_HEADS + h + 1])
        s = s_ref[:, h]
        kq = jnp.where(row == 0, k[:, None, :], jnp.where(row == 1, q[:, None, :], 0.0))
        ks = jnp.einsum('bmk,bkn->bmn', kq.astype(bf16), s.astype(bf16),
                        preferred_element_type=f32)
        v_new = beta * (v - eg * ks[:, 0, :])
        qk = jnp.sum(q * k, axis=-1, keepdims=True)
        o_ref[:, h * A_DV:(h + 1) * A_DV] = eg * ks[:, 1, :] + qk * v_new
        kh = k.astype(bf16).astype(f32)
        vh = v_new.astype(bf16).astype(f32)
        kl, vl = k - kh, v_new - vh
        lhs = jnp.where(row == 2, kl[:, None, :], jnp.where(row < 2, kh[:, None, :], 0.0))
        rhs = jnp.where(row == 1, vl[:, None, :], jnp.where(row < 3, vh[:, None, :], 0.0))
        outer = lax.dot_general(lhs.astype(bf16), rhs.astype(bf16), (((1,), (1,)), ((0,), (0,))),
                                preferred_element_type=f32)
        so_ref[:, h] = s * eg[:, :, None] + outer


def _a_step(proj, conv_state, gates, wconv, s0, li, s_prev):
    nb = proj.shape[1]
    bt = A_BT
    in_specs = [
        pl.BlockSpec((None, bt, A_CONV_DIM), lambda i: (0, i, 0)),
        pl.BlockSpec((None, bt, CONV_W - 1, A_CONV_DIM), lambda i: (li, i, 0, 0)),
        pl.BlockSpec((None, bt, LANES), lambda i: (0, i, 0)),
        pl.BlockSpec((None, CONV_W, A_CONV_DIM), lambda i: (li, 0, 0)),
        pl.BlockSpec((None, bt, A_HEADS, A_DK, A_DV), lambda i: (li, i, 0, 0, 0)),
    ]
    args = [proj, conv_state, gates, wconv, s0]
    if li:
        in_specs.append(pl.BlockSpec((li, bt, A_HEADS, A_DK, A_DV), lambda i: (0, i, 0, 0, 0)))
        args.append(s_prev)
    return pl.pallas_call(
        functools.partial(_a_step_kernel, li=li),
        grid=(nb // bt,),
        in_specs=in_specs,
        out_specs=[
            pl.BlockSpec((None, bt, A_V_DIM), lambda i: (0, i, 0)),
            pl.BlockSpec((li + 1, bt, A_HEADS, A_DK, A_DV), lambda i: (0, i, 0, 0, 0)),
        ],
        out_shape=[jax.ShapeDtypeStruct((1, nb, A_V_DIM), f32),
                   jax.ShapeDtypeStruct((li + 1,) + s0.shape[1:], f32)],
        compiler_params=_params(("arbitrary",)),
        name="a_step",
    )(*args)


def _a_out_kernel(x_ref, o_ref, z_ref, gt_ref, won_ref, wo_ref, y_ref, og_ref):
    won = won_ref[...]
    for h in range(A_HEADS):
        sl = slice(h * A_DV, (h + 1) * A_DV)
        og_ref[:, sl] = (_rms(o_ref[:, sl]) * won * _silu(z_ref[:, sl])).astype(bf16)
    y = jnp.dot(og_ref[...], wo_ref[...], preferred_element_type=f32)
    y_ref[...] = x_ref[...] + (1.0 + gt_ref[...]) * y


def _a_out(x, o, proj, mods, l, li, won, wo):
    G, R, _ = x.shape
    tm = min(R, 512)
    return pl.pallas_call(
        _a_out_kernel,
        grid=(G, R // tm),
        in_specs=[
            _tok_spec(tm, D_MODEL),
            _tok_spec(tm, A_V_DIM),
            _tok_spec(tm, A_V_DIM, A_CONV_DIM // A_V_DIM),
            mods.spec(l, 1, 2, tm),
            pl.BlockSpec((None, 1, A_DV), lambda g, i: (li, 0, 0)),
            _resident((None, A_V_DIM, D_MODEL), lambda g, i: (li, 0, 0)),
        ],
        out_specs=_tok_spec(tm, D_MODEL),
        out_shape=jax.ShapeDtypeStruct(x.shape, f32),
        scratch_shapes=[pltpu.VMEM((tm, A_V_DIM), bf16)],
        compiler_params=_params(("arbitrary", "arbitrary")),
        name="a_out",
    )(x, o, proj, mods.arr, won, wo)


def _b_kernel(x_ref, sh_ref, sc_ref, gt_ref, nw_ref, wi_ref, vw_ref, vb_ref, ws_ref, bs_ref, wo_ref,
              *rest, decode):
    if decode:
        y_ref, vout_ref, vn_ref, gd_ref = rest
    else:
        y_ref, vn_ref, gd_ref = rest
    tm = x_ref.shape[0]
    x = x_ref[...]
    h = _norm_mod(x, nw_ref[...], sc_ref[...], sh_ref[...]).astype(bf16)
    v = _gelu(jnp.dot(h, wi_ref[:, B_WIDTH:], preferred_element_type=f32))
    mu = jnp.mean(v, axis=-1, keepdims=True)
    vc = v - mu
    vn = vc * lax.rsqrt(jnp.mean(vc * vc, axis=-1, keepdims=True) + EPS) * vw_ref[...] + vb_ref[...]
    if decode:
        vout_ref[...] = vn
    else:
        vn_ref[...] = vn.astype(bf16)
        ri = lax.broadcasted_iota(jnp.int32, (CHUNK_B, CHUNK_B), 0)
        ci = lax.broadcasted_iota(jnp.int32, (CHUNK_B, CHUNK_B), 1)
        causal = ri >= ci
    for g in range(B_GROUPS):
        gs = slice(g * B_GROUP_DIM, (g + 1) * B_GROUP_DIM)
        u = _gelu(jnp.dot(h, wi_ref[:, gs], preferred_element_type=f32))
        if decode:
            gd_ref[:, gs] = (u * (ws_ref[:, gs] * vn[:, gs] + bs_ref[:, gs])).astype(bf16)
        else:
            wsg = jnp.where(causal, ws_ref[g], 0.0).astype(bf16)
            bias = bs_ref[:, g:g + 1]
            for c in range(tm // CHUNK_B):
                rs = slice(c * CHUNK_B, (c + 1) * CHUNK_B)
                mixed = jnp.dot(wsg, vn_ref[rs, gs], preferred_element_type=f32) + bias
                gd_ref[rs, gs] = (u[rs] * mixed).astype(bf16)
    y = jnp.dot(gd_ref[...], wo_ref[...], preferred_element_type=f32)
    y_ref[...] = x + (1.0 + gt_ref[...]) * y


def _b_mixer(x, mods, l, li, nw3, wi, vw, vb, ws, bs, wo, decode):
    G, R, _ = x.shape
    tm = min(R, 512)
    sub = 1
    if decode:
        ws_spec = pl.BlockSpec((None, 1, B_WIDTH), lambda g, i: (li, 0, 0))
        bs_spec = pl.BlockSpec((None, 1, B_WIDTH), lambda g, i: (li, 0, 0))
        out_specs = [_tok_spec(tm, D_MODEL), _tok_spec(tm, B_WIDTH)]
        out_shape = [jax.ShapeDtypeStruct(x.shape, f32), jax.ShapeDtypeStruct((G, R, B_WIDTH), f32)]
    else:
        ws_spec = pl.BlockSpec((None, B_GROUPS, CHUNK_B, CHUNK_B), lambda g, i: (li, 0, 0, 0))
        bs_spec = pl.BlockSpec((None, CHUNK_B, B_GROUPS), lambda g, i: (li, 0, 0))
        out_specs = _tok_spec(tm, D_MODEL)
        out_shape = jax.ShapeDtypeStruct(x.shape, f32)
    return pl.pallas_call(
        functools.partial(_b_kernel, decode=decode),
        grid=(G, R // tm),
        in_specs=[
            _tok_spec(tm, D_MODEL),
            mods.spec(l, sub, 0, tm), mods.spec(l, sub, 1, tm), mods.spec(l, sub, 2, tm),
            pl.BlockSpec((None, 1, D_MODEL), lambda g, i: (l * N_SUB + sub, 0, 0)),
            _resident((None, D_MODEL, 2 * B_WIDTH), lambda g, i: (li, 0, 0)),
            pl.BlockSpec((None, 1, B_WIDTH), lambda g, i: (li, 0, 0)),
            pl.BlockSpec((None, 1, B_WIDTH), lambda g, i: (li, 0, 0)),
            ws_spec, bs_spec,
            _resident((None, B_WIDTH, D_MODEL), lambda g, i: (li, 0, 0)),
        ],
        out_specs=out_specs,
        out_shape=out_shape,
        scratch_shapes=[pltpu.VMEM((tm, B_WIDTH), bf16), pltpu.VMEM((tm, B_WIDTH), bf16)],
        compiler_params=_params(("arbitrary", "arbitrary")),
        name="b_mixer",
    )(x, mods.arr, mods.arr, mods.arr, nw3, wi, vw, vb, ws, bs, wo)


def _trunk(xp, mp, xs, ms, conv_state, s_state, W):
    conv_p, s_p, conv_s, s_s, v_s = [], [], [], [], []
    for i in range(DEPTH):
        li = i // N_MIXERS
        xp, xs = _ffn(xp, mp, xs, ms, i, 0, 0, W["nw3"], W["wgu"], W["wdn"])
        if i % N_MIXERS == 0:
            proj, gates = _a_in(xs, ms, i, li, W["nw3"], W["a_wm"], W["a_ws"], W["a_alog"], W["a_dtb"])
            o, s_new = _a_step(proj, conv_state, gates, W["a_wconv"], s_state, li,
                               s_s[-1] if s_s else None)
            s_s.append(s_new)
            pre = proj[0, :, None, :A_CONV_DIM]
            conv_s.append(jnp.concatenate([conv_state[li][:, 1:], pre], axis=1))
            xs = _a_out(xs, o, proj, ms, i, li, W["a_won"], W["a_wo"])
            xp, s_new, tail = _a_prompt(xp, mp, i, li, W["nw3"], W["a_wm"], W["a_ws"], W["a_alog"],
                                        W["a_dtb"], W["a_wconv"], W["a_won"], W["a_wo"])
            s_p.append(s_new)
            conv_p.append(tail[:, 7::8])
        else:
            xs, v = _b_mixer(xs, ms, i, li, W["nw3"], W["b_wi"], W["b_vw"], W["b_vb"],
                             W["b_ws00"], W["b_bs0"], W["b_wo"], decode=True)
            v_s.append(v)
            xp = _b_mixer(xp, mp, i, li, W["nw3"], W["b_wi"], W["b_vw"], W["b_vb"],
                          W["b_ws"], W["b_bsT"], W["b_wo"], decode=False)
        fw = W["final_w"] if i == DEPTH - 1 else None
        xp, xs = _ffn(xp, mp, xs, ms, i, 1, 2, W["nw3"], W["wgu"], W["wdn"], final_w=fw)
    return xp, xs, conv_p, s_p, conv_s, s_s, v_s


def kernel(x_prompt, x_sample, state_a_conv, state_a_S, c_prompt, c_sample, w_ada, b_ada, norm_w, ffn_w_gu, ffn_w_down, a_w_in, a_w_conv, a_log, a_dt_bias, a_w_onorm, a_w_out, b_w_in, b_vnorm_w, b_vnorm_b, b_w_s, b_b_s, b_w_out, final_norm_w):
    n_a = a_w_in.shape[0]
    n_b = b_w_in.shape[0]
    bp, seq, _ = x_prompt.shape
    bs = x_sample.shape[0]
    assert x_sample.shape[1] == 1 and seq % A_LT == 0 and seq % CHUNK_B == 0

    mod_p, mod_s = _ada(c_prompt, c_sample, w_ada, b_ada)
    mods_p = _Mods(mod_p.reshape(DEPTH, bp, 1, N_SUB * 3 * D_MODEL), per_row=False)
    mods_s = _Mods(mod_s.reshape(DEPTH, 1, bs, N_SUB * 3 * D_MODEL), per_row=True)

    pad = LANES - 2 * A_HEADS
    gate_pad = lambda a: jnp.pad(a, ((0, 0), (A_HEADS, pad)))[:, None, :]
    W = {
        "nw3": norm_w.reshape(DEPTH * N_SUB, 1, D_MODEL),
        "wgu": ffn_w_gu.astype(bf16),
        "wdn": ffn_w_down.astype(bf16),
        "a_wm": a_w_in.astype(bf16),
        "a_ws": jnp.pad(a_w_in[:, :, A_MAIN_DIM:], ((0, 0), (0, 0), (0, pad))),
        "a_alog": gate_pad(a_log),
        "a_dtb": gate_pad(a_dt_bias),
        "a_wconv": a_w_conv,
        "a_won": a_w_onorm[:, None, :],
        "a_wo": a_w_out.astype(bf16),
        "b_wi": b_w_in.astype(bf16),
        "b_vw": b_vnorm_w[:, None, :],
        "b_vb": b_vnorm_b[:, None, :],
        "b_ws": b_w_s,
        "b_bsT": jnp.swapaxes(b_b_s, 1, 2),
        "b_ws00": jnp.repeat(b_w_s[:, :, 0, 0], B_GROUP_DIM, axis=1)[:, None, :],
        "b_bs0": jnp.repeat(b_b_s[:, :, 0], B_GROUP_DIM, axis=1)[:, None, :],
        "b_wo": b_w_out.astype(bf16),
        "final_w": final_norm_w,
    }

    y_p, y_s, conv_p, s_p, conv_s, s_s, v_s = _trunk(
        x_prompt, mods_p, x_sample.reshape(1, bs, D_MODEL), mods_s, state_a_conv, state_a_S, W)
    return (y_p,
            y_s.reshape(bs, 1, D_MODEL),
            jnp.stack(conv_p),
            jnp.stack(s_p),
            jnp.stack(conv_s),
            s_s[-1],
            jnp.stack(v_s).reshape(n_b, bs, 1, B_WIDTH))
```

```python
import functools

import jax
import jax.numpy as jnp
import numpy as np
from jax import lax
from jax.experimental import pallas as pl
from jax.experimental.pallas import tpu as pltpu

f32 = jnp.float32
bf16 = jnp.bfloat16

D_MODEL = 1024
DEPTH = 4
N_MIXERS = 2
A_HEADS = 8
A_DK = 128
A_DV = 128
A_QK_DIM = A_HEADS * A_DK
A_V_DIM = A_HEADS * A_DV
A_CONV_DIM = 2 * A_QK_DIM + A_V_DIM
A_MAIN_DIM = A_CONV_DIM + A_V_DIM
CONV_W = 4
CHUNK_A = 64
B_WIDTH = 2 * D_MODEL
B_GROUPS = 8
B_GROUP_DIM = B_WIDTH // B_GROUPS
CHUNK_B = 128
D_FF = 2816
FFN_RES = 0.5
N_SUB = 3
EPS = 1e-6

LANES = 128
FF_TILE = 256
GATE_COLS = 3 * A_HEADS
VMEM_LIMIT = 52 * 1024 * 1024


def _params(sem):
    return pltpu.CompilerParams(dimension_semantics=sem, vmem_limit_bytes=VMEM_LIMIT)


def _resident(shape, index_map):
    return pl.BlockSpec(shape, index_map, pipeline_mode=pl.Buffered(1))


def _bdot(a, b):
    return jnp.dot(a.astype(bf16), b.astype(bf16), preferred_element_type=f32)


def _bdot_nt(a, b):
    return lax.dot_general(a.astype(bf16), b.astype(bf16), (((1,), (1,)), ((), ())),
                           preferred_element_type=f32)


def _bdot_tn(a, b):
    return lax.dot_general(a.astype(bf16), b.astype(bf16), (((0,), (0,)), ((), ())),
                           preferred_element_type=f32)


def _split2(a):
    hi = a.astype(bf16)
    lo = (a - hi.astype(f32)).astype(bf16)
    return hi, lo


def _split3(a):
    hi = a.astype(bf16)
    r = a - hi.astype(f32)
    mid = r.astype(bf16)
    lo = (r - mid.astype(f32)).astype(bf16)
    return hi, mid, lo


def _dot3(a, b):
    ah, al = _split2(a)
    bh, bl = _split2(b)
    d = functools.partial(jnp.dot, preferred_element_type=f32)
    rows = a.shape[0]
    p = d(jnp.concatenate([ah, al], axis=0), bh)
    return p[:rows] + p[rows:] + d(ah, bl)


LOG2E = 1.4426950408889634


def _silu(x):
    return x / (1.0 + jnp.exp2(x * (-LOG2E)))


def _softplus(x):
    return jnp.maximum(x, 0.0) + jnp.log1p(jnp.exp(-jnp.abs(x)))


GELU_A = -LOG2E * 2.0 * (2.0 / 3.141592653589793) ** 0.5
GELU_B = GELU_A * 0.044715


def _gelu(x):
    return x / (1.0 + jnp.exp2(x * (GELU_A + GELU_B * (x * x))))


def _rms(x):
    return x * lax.rsqrt(jnp.mean(x * x, axis=-1, keepdims=True) + EPS)


def _norm_mod(x, nw, sc, sh):
    return (_rms(x) * nw) * (1.0 + sc) + sh


def _l2n(x):
    return x * lax.rsqrt(jnp.sum(x * x, axis=-1, keepdims=True) + EPS)


def _ada_kernel(c_ref, w_ref, b_ref, op_ref, os_ref):
    bp = op_ref.shape[0]
    mod = _bdot(_silu(c_ref[...]), w_ref[...].astype(bf16)) + b_ref[...]
    op_ref[...] = mod[:bp]
    os_ref[...] = mod[bp:]


def _ada(c_p, c_s, w_ada, b_ada):
    n = N_SUB * 3 * D_MODEL
    tn = 18 * LANES
    bp, bs = c_p.shape[0], c_s.shape[0]
    assert bp % 8 == 0
    return pl.pallas_call(
        _ada_kernel,
        grid=(DEPTH, n // tn),
        in_specs=[
            pl.BlockSpec((bp + bs, D_MODEL), lambda l, j: (0, 0)),
            pl.BlockSpec((None, D_MODEL, tn), lambda l, j: (l, 0, j)),
            pl.BlockSpec((None, 1, tn), lambda l, j: (l, 0, j)),
        ],
        out_specs=[
            pl.BlockSpec((None, bp, tn), lambda l, j: (l, 0, j)),
            pl.BlockSpec((None, bs, tn), lambda l, j: (l, 0, j)),
        ],
        out_shape=[jax.ShapeDtypeStruct((DEPTH, bp, n), f32),
                   jax.ShapeDtypeStruct((DEPTH, bs, n), f32)],
        compiler_params=_params(("arbitrary", "arbitrary")),
        name="ada",
    )(jnp.concatenate([c_p, c_s], axis=0), w_ada, b_ada.reshape(DEPTH, 1, n))


class _Mods:
    def __init__(self, arr, per_row):
        self.arr = arr
        self.per_row = per_row

    def spec(self, l, sub, kind, tm):
        col = sub * 3 + kind
        if self.per_row:
            return pl.BlockSpec((None, None, tm, D_MODEL), lambda g, i: (l, g, i, col))
        return pl.BlockSpec((None, None, 1, D_MODEL), lambda g, i: (l, g, 0, col))


def _tok_spec(tm, width, colblk=0):
    return pl.BlockSpec((None, tm, width), lambda g, i: (g, i, colblk))


def _ffn_kernel(xp_ref, shp_ref, scp_ref, gtp_ref, xs_ref, shs_ref, scs_ref, gts_ref,
                nw_ref, wg_ref, wu_ref, wd_ref, *rest, final, n_prompt):
    if final:
        fw_ref, op_ref, os_ref, act_ref = rest
    else:
        op_ref, os_ref, act_ref = rest

    def body(x_ref, sh_ref, sc_ref, gt_ref, o_ref):
        rows = x_ref.shape[0]
        x = x_ref[...]
        h = _norm_mod(x, nw_ref[...], sc_ref[...], sh_ref[...]).astype(bf16)
        for c in range(D_FF // FF_TILE):
            sl = slice(c * FF_TILE, (c + 1) * FF_TILE)
            g = jnp.dot(h, wg_ref[:, sl], preferred_element_type=f32)
            u = jnp.dot(h, wu_ref[:, sl], preferred_element_type=f32)
            act_ref[0:rows, sl] = (_silu(g) * u).astype(bf16)
        y = jnp.dot(act_ref[0:rows, :], wd_ref[...], preferred_element_type=f32)
        out = x + FFN_RES * (1.0 + gt_ref[...]) * y
        if final:
            out = _rms(out) * fw_ref[...]
        o_ref[...] = out

    i = pl.program_id(0)
    pl.when(i < n_prompt)(lambda: body(xp_ref, shp_ref, scp_ref, gtp_ref, op_ref))
    pl.when(i == n_prompt)(lambda: body(xs_ref, shs_ref, scs_ref, gts_ref, os_ref))


def _ffn(xp, mods_p, xs, mods_s, l, j, sub, nw3, wgu, wdn, final_w=None):
    B, L, _ = xp.shape
    bs = xs.shape[1]
    tm = min(L, 1024)
    tps = L // tm
    n_prompt = B * tps
    final = final_w is not None
    tile = lambda i: jnp.minimum(i, n_prompt - 1)
    pmod = lambda kind: pl.BlockSpec((None, None, 1, D_MODEL),
                                     lambda i: (l, tile(i) // tps, 0, sub * 3 + kind))
    smod = lambda kind: pl.BlockSpec((None, None, bs, D_MODEL), lambda i: (l, 0, 0, sub * 3 + kind))
    in_specs = [
        pl.BlockSpec((None, tm, D_MODEL), lambda i: (tile(i) // tps, tile(i) % tps, 0)),
        pmod(0), pmod(1), pmod(2),
        pl.BlockSpec((None, bs, D_MODEL), lambda i: (0, 0, 0)),
        smod(0), smod(1), smod(2),
        pl.BlockSpec((None, 1, D_MODEL), lambda i: (l * N_SUB + sub, 0, 0)),
        _resident((None, None, D_MODEL, D_FF), lambda i: (l, j, 0, 0)),
        _resident((None, None, D_MODEL, D_FF), lambda i: (l, j, 0, 1)),
        _resident((None, None, D_FF, D_MODEL), lambda i: (l, j, 0, 0)),
    ]
    args = [xp, mods_p.arr, mods_p.arr, mods_p.arr, xs, mods_s.arr, mods_s.arr, mods_s.arr,
            nw3, wgu, wgu, wdn]
    if final:
        in_specs.append(pl.BlockSpec((1, D_MODEL), lambda i: (0, 0)))
        args.append(final_w.reshape(1, D_MODEL))
    return pl.pallas_call(
        functools.partial(_ffn_kernel, final=final, n_prompt=n_prompt),
        grid=(n_prompt + 1,),
        in_specs=in_specs,
        out_specs=[
            pl.BlockSpec((None, tm, D_MODEL), lambda i: (tile(i) // tps, tile(i) % tps, 0)),
            pl.BlockSpec((None, bs, D_MODEL), lambda i: (0, 0, 0)),
        ],
        out_shape=[jax.ShapeDtypeStruct(xp.shape, f32), jax.ShapeDtypeStruct(xs.shape, f32)],
        scratch_shapes=[pltpu.VMEM((tm, D_FF), bf16)],
        compiler_params=_params(("arbitrary",)),
        name="ffn",
    )(*args)


def _gate_values(hf, ws_ref, alog_ref, dtb_ref):
    p = _dot3(hf, ws_ref[...])
    lane = lax.broadcasted_iota(jnp.int32, p.shape, 1)
    beta = jax.nn.sigmoid(p)
    g = -jnp.exp(alog_ref[...]) * _softplus(p + dtb_ref[...])
    is_g = (lane >= A_HEADS) & (lane < 2 * A_HEADS)
    return jnp.where(lane < A_HEADS, beta, jnp.where(is_g, g, 0.0))


def _a_in_kernel(x_ref, sh_ref, sc_ref, nw_ref, wm_ref, ws_ref, alog_ref, dtb_ref, proj_ref, gates_ref):
    hf = _norm_mod(x_ref[...], nw_ref[...], sc_ref[...], sh_ref[...])
    h = hf.astype(bf16)
    for c in range(A_MAIN_DIM // D_MODEL):
        sl = slice(c * D_MODEL, (c + 1) * D_MODEL)
        proj_ref[:, sl] = jnp.dot(h, wm_ref[:, sl], preferred_element_type=f32)
    gates_ref[...] = _gate_values(hf, ws_ref, alog_ref, dtb_ref)


def _a_in(x, mods, l, li, nw3, wm, ws, alog, dtb):
    G, R, _ = x.shape
    tm = min(R, 512)
    sub = 1
    return pl.pallas_call(
        _a_in_kernel,
        grid=(G, R // tm),
        in_specs=[
            _tok_spec(tm, D_MODEL),
            mods.spec(l, sub, 0, tm), mods.spec(l, sub, 1, tm),
            pl.BlockSpec((None, 1, D_MODEL), lambda g, i: (l * N_SUB + sub, 0, 0)),
            _resident((None, D_MODEL, A_MAIN_DIM), lambda g, i: (li, 0, 0)),
            _resident((None, D_MODEL, LANES), lambda g, i: (li, 0, 0)),
            pl.BlockSpec((None, 1, LANES), lambda g, i: (li, 0, 0)),
            pl.BlockSpec((None, 1, LANES), lambda g, i: (li, 0, 0)),
        ],
        out_specs=[_tok_spec(tm, A_MAIN_DIM), _tok_spec(tm, LANES)],
        out_shape=[jax.ShapeDtypeStruct((G, R, A_MAIN_DIM), f32),
                   jax.ShapeDtypeStruct((G, R, LANES), f32)],
        compiler_params=_params(("arbitrary", "arbitrary")),
        name="a_in",
    )(x, mods.arr, mods.arr, nw3, wm, ws, alog, dtb)


A_LT = 256
A_GROUP = 2


def _a_prompt_kernel(x_ref, sh_ref, sc_ref, gt_ref, nw_ref, wm_ref, ws_ref, alog_ref, dtb_ref, wc_ref,
                     won_ref, wo_ref, perm_ref, permt_ref, ptri_ref,
                     y_ref, s_ref, tail_ref, hist_ref, act_ref, o_scr, og_ref):
    t = pl.program_id(1)
    lt = x_ref.shape[0]
    nc = lt // CHUNK_A

    @pl.when(t == 0)
    def _():
        s_ref[...] = jnp.zeros(s_ref.shape, f32)
        hist_ref[...] = jnp.zeros(hist_ref.shape, f32)

    x = x_ref[...]
    hf = _norm_mod(x, nw_ref[...], sc_ref[...], sh_ref[...])
    h = hf.astype(bf16)

    ri = lax.broadcasted_iota(jnp.int32, (CHUNK_A, CHUNK_A), 0)
    ci = lax.broadcasted_iota(jnp.int32, (CHUNK_A, CHUNK_A), 1)
    pos_r = (ri % 8) * (CHUNK_A // 8) + ri // 8
    pos_c = (ci % 8) * (CHUNK_A // 8) + ci // 8
    causal = pos_r >= pos_c
    strict = pos_r > pos_c
    hp = jnp.dot(perm_ref[...], h, preferred_element_type=f32).astype(bf16)

    gates = _gate_values(hf, ws_ref, alog_ref, dtb_ref)
    lane = lax.broadcasted_iota(jnp.int32, gates.shape, 1)
    is_g = (lane >= A_HEADS) & (lane < 2 * A_HEADS)
    gparts = _split3(gates)
    sparts = _split3(pltpu.roll(jnp.where(is_g, gates, 0.0), A_HEADS, 1))
    ptri = ptri_ref[...]
    gcols, grows = [], []
    for c in range(nc):
        rs = slice(c * CHUNK_A, (c + 1) * CHUNK_A)
        gc = None
        for gp, sp in zip(gparts, sparts):
            term = jnp.dot(ptri, jnp.concatenate([gp[rs], sp[rs]], axis=0), preferred_element_type=f32)
            gc = term if gc is None else gc + term
        gcols.append(gc)
        grows.append(gc.T)

    ng = CHUNK_A // 8
    sub = lax.broadcasted_iota(jnp.int32, (8, 2 * A_DK), 0)
    pair = 2 * A_DK
    for p in range(A_HEADS // 2):
        for part in range(3):
            col = part * A_QK_DIM + p * pair
            cs = slice(col, col + pair)
            pre = jnp.dot(hp, wm_ref[:, cs], preferred_element_type=f32)
            grp = lambda c, i: pre[c * CHUNK_A + 8 * i:c * CHUNK_A + 8 * i + 8]
            acc = pre * wc_ref[CONV_W - 1:CONV_W, cs]
            shifted = [[] for _ in range(CONV_W)]
            for c in range(nc):
                w = []
                for j in range(CONV_W - 1):
                    i = ng - (CONV_W - 1) + j
                    prev = hist_ref[8 * j:8 * j + 8, cs] if c == 0 else grp(c - 1, i)
                    w.append(pltpu.roll(jnp.where(sub == 7, prev, grp(c, i)), 1, 0))
                for k_ in range(1, CONV_W):
                    shifted[k_] += w[CONV_W - 1 - k_:] + [pre[c * CHUNK_A:(c + 1) * CHUNK_A - 8 * k_]]
            hist_ref[:, cs] = pre[lt - 8 * (CONV_W - 1):]
            for k_ in range(1, CONV_W):
                acc = acc + jnp.concatenate(shifted[k_], axis=0) * wc_ref[CONV_W - 1 - k_:CONV_W - k_, cs]
            a2 = _silu(acc)
            for hh in range(2):
                a = a2[:, hh * A_DK:(hh + 1) * A_DK]
                if part == 0:
                    a = a * (lax.rsqrt(jnp.sum(a * a, axis=-1, keepdims=True) + EPS) * (A_DK ** -0.5))
                elif part == 1:
                    a = _l2n(a)
                act_ref[part * A_HEADS + 2 * p + hh] = a
    tail_ref[...] = hist_ref[...]

    z = jnp.dot(hp, wm_ref[:, A_CONV_DIM:A_MAIN_DIM], preferred_element_type=f32)
    won = won_ref[...]
    zg = [_silu(z[:, h_ * A_DV:(h_ + 1) * A_DV]) * won for h_ in range(A_HEADS)]

    states = [s_ref[h_] for h_ in range(A_HEADS)]

    def recurrence_pieces(c, insts):
        rs = slice(c * CHUNK_A, (c + 1) * CHUNK_A)
        tmp = {}

        def first():
            tmp["ws_qs"] = [jnp.dot(i["lhs1"], states[h_].astype(bf16), preferred_element_type=f32)
                            for h_, i in enumerate(insts)]

        def second():
            tmp["r"] = [jnp.dot(i["lhs2"], (i["u"] - w[:CHUNK_A]).astype(bf16), preferred_element_type=f32)
                        for i, w in zip(insts, tmp["ws_qs"])]

        def third():
            for h_, (i, w, r) in enumerate(zip(insts, tmp["ws_qs"], tmp["r"])):
                o_scr[rs, h_ * A_DV:(h_ + 1) * A_DV] = w[CHUNK_A:] + r[:CHUNK_A]
                states[h_] = states[h_] * i["edl"] + r[CHUNK_A:]

        return [first, second, third]

    pending = []
    for c0 in range(0, nc, A_GROUP):
        chunks = range(c0, min(c0 + A_GROUP, nc))

        inst = []
        for c in chunks:
            rs = slice(c * CHUNK_A, (c + 1) * CHUNK_A)
            for h_ in range(A_HEADS):
                qc = act_ref[h_, rs, :]
                kc = act_ref[A_HEADS + h_, rs, :]
                vc = act_ref[2 * A_HEADS + h_, rs, :]
                bcol = gcols[c][:, h_:h_ + 1]
                dcol = gcols[c][:, 2 * A_HEADS + h_:2 * A_HEADS + h_ + 1]
                drow = grows[c][2 * A_HEADS + h_:2 * A_HEADS + h_ + 1, :]
                dlast = dcol[CHUNK_A - 1:CHUNK_A, :]
                gam = jnp.where(causal, jnp.exp(jnp.where(causal, dcol - drow, 0.0)), 0.0)
                kb = kc * bcol
                ed = jnp.exp(dcol)
                kt = kc.T
                kq = _bdot(jnp.concatenate([kb, qc], axis=0), kt)
                m = jnp.where(strict, kq[:CHUNK_A] * gam, 0.0)
                attn = kq[CHUNK_A:] * gam
                kdec_t = kt * jnp.exp(dlast - drow)
                inst.append(dict(
                    m=m,
                    rhs=jnp.concatenate([vc * bcol, kb * ed], axis=1),
                    qed=(qc * ed).astype(bf16),
                    lhs2=jnp.concatenate([attn, kdec_t], axis=0).astype(bf16),
                    edl=jnp.exp(dlast)))

        ts = [-i["m"] for i in inst]
        ms = [_bdot(i["m"], i["m"]) for i in inst]
        n = 2
        while n < CHUNK_A:
            if 2 * n < CHUNK_A:
                prod = [_bdot(jnp.concatenate([t, m], axis=0), m) for t, m in zip(ts, ms)]
                ts = [t + m + p[:CHUNK_A] for t, m, p in zip(ts, ms, prod)]
                ms = [p[CHUNK_A:] for p in prod]
            else:
                ts = [t + m + _bdot(t, m) for t, m in zip(ts, ms)]
            n *= 2
            if pending:
                pending.pop(0)()
        rs_ = [-(i["m"] + t + _dot3(i["m"], t)) for i, t in zip(inst, ts)]
        if pending:
            pending.pop(0)()
        ts = [t + r + _bdot(t, r) for t, r in zip(ts, rs_)]
        while pending:
            pending.pop(0)()

        for i, t in zip(inst, ts):
            uw = i["rhs"] + _bdot(t, i["rhs"])
            i["u"] = uw[:, :A_DV]
            i["lhs1"] = jnp.concatenate([uw[:, A_DV:].astype(bf16), i["qed"]], axis=0)
        for k_, c in enumerate(chunks):
            pending += recurrence_pieces(c, inst[k_ * A_HEADS:(k_ + 1) * A_HEADS])
    while pending:
        pending.pop(0)()
    for h_ in range(A_HEADS):
        s_ref[h_] = states[h_]

    for h_ in range(A_HEADS):
        sl = slice(h_ * A_DV, (h_ + 1) * A_DV)
        og_ref[:, sl] = (_rms(o_scr[:, sl]) * zg[h_]).astype(bf16)
    og = jnp.dot(permt_ref[...], og_ref[...], preferred_element_type=f32).astype(bf16)
    y = jnp.dot(og, wo_ref[...], preferred_element_type=f32)
    y_ref[...] = x + (1.0 + gt_ref[...]) * y


def _a_prompt(x, mods, l, li, nw3, wm, ws, alog, dtb, wconv, won, wo):
    B, L, _ = x.shape
    lt = A_LT
    sub = 1
    const = lambda shape: pl.BlockSpec((None,) + shape, lambda b, t: (li,) + (0,) * len(shape))
    r = np.arange(CHUNK_A)
    pc = (np.arange(CHUNK_A)[None, :] == ((r % 8) * (CHUNK_A // 8) + r // 8)[:, None]).astype(np.float32)
    perm_np = np.kron(np.eye(lt // CHUNK_A, dtype=np.float32), pc)
    perm = jnp.asarray(perm_np, dtype=bf16)
    permt = jnp.asarray(perm_np.T, dtype=bf16)
    ptri = jnp.asarray(np.concatenate([pc, pc @ np.tril(np.ones((CHUNK_A, CHUNK_A), np.float32))], axis=1),
                       dtype=bf16)
    hist_rows = 8 * (CONV_W - 1)
    return pl.pallas_call(
        _a_prompt_kernel,
        grid=(B, L // lt),
        in_specs=[
            _tok_spec(lt, D_MODEL),
            mods.spec(l, sub, 0, lt), mods.spec(l, sub, 1, lt), mods.spec(l, sub, 2, lt),
            pl.BlockSpec((None, 1, D_MODEL), lambda b, t: (l * N_SUB + sub, 0, 0)),
            _resident((None, D_MODEL, A_MAIN_DIM), lambda b, t: (li, 0, 0)),
            _resident((None, D_MODEL, LANES), lambda b, t: (li, 0, 0)),
            const((1, LANES)), const((1, LANES)),
            const((CONV_W, A_CONV_DIM)),
            const((1, A_DV)),
            _resident((None, A_V_DIM, D_MODEL), lambda b, t: (li, 0, 0)),
            _resident((lt, lt), lambda b, t: (0, 0)),
            _resident((lt, lt), lambda b, t: (0, 0)),
            _resident((CHUNK_A, 2 * CHUNK_A), lambda b, t: (0, 0)),
        ],
        out_specs=[
            _tok_spec(lt, D_MODEL),
            pl.BlockSpec((None, A_HEADS, A_DK, A_DV), lambda b, t: (b, 0, 0, 0)),
            pl.BlockSpec((None, hist_rows, A_CONV_DIM), lambda b, t: (b, 0, 0)),
        ],
        out_shape=[jax.ShapeDtypeStruct(x.shape, f32),
                   jax.ShapeDtypeStruct((B, A_HEADS, A_DK, A_DV), f32),
                   jax.ShapeDtypeStruct((B, hist_rows, A_CONV_DIM), f32)],
        scratch_shapes=[pltpu.VMEM((hist_rows, A_CONV_DIM), f32),
                        pltpu.VMEM((3 * A_HEADS, lt, A_DK), f32),
                        pltpu.VMEM((lt, A_V_DIM), f32),
                        pltpu.VMEM((lt, A_V_DIM), bf16)],
        compiler_params=_params(("arbitrary", "arbitrary")),
        name="a_prompt",
    )(x, mods.arr, mods.arr, mods.arr, nw3, wm, ws, alog, dtb, wconv, won, wo, perm, permt, ptri)


A_BT = 8


def _a_step_kernel(pre_ref, conv_ref, gates_ref, wc_ref, s_ref, *rest, li):
    if li:
        prev_ref, o_ref, so_all_ref = rest
        so_all_ref[0:li] = prev_ref[...]
    else:
        o_ref, so_all_ref = rest
    so_ref = so_all_ref.at[li]
    bt = pre_ref.shape[0]
    wc = wc_ref[...]
    acc = pre_ref[...] * wc[CONV_W - 1:CONV_W, :]
    for j in range(CONV_W - 1):
        acc = acc + conv_ref[:, j, :] * wc[j:j + 1, :]
    qkv = _silu(acc)
    gates = gates_ref[...]
    row = lax.broadcasted_iota(jnp.int32, (bt, 8, LANES), 1)
    for h in range(A_HEADS):
        q = _l2n(qkv[:, h * A_DK:(h + 1) * A_DK]) * (A_DK ** -0.5)
        k = _l2n(qkv[:, A_QK_DIM + h * A_DK:A_QK_DIM + (h + 1) * A_DK])
        v = qkv[:, 2 * A_QK_DIM + h * A_DV:2 * A_QK_DIM + (h + 1) * A_DV]
        beta = gates[:, h:h + 1]
        eg = jnp.exp(gates[:, A_HEADS + h:A_HEADS + h + 1])
        s = s_ref[:, h]
        kq = jnp.where(row == 0, k[:, None, :], jnp.where(row == 1, q[:, None, :], 0.0))
        ks = jnp.einsum('bmk,bkn->bmn', kq.astype(bf16), s.astype(bf16),
                        preferred_element_type=f32)
        v_new = beta * (v - eg * ks[:, 0, :])
        qk = jnp.sum(q * k, axis=-1, keepdims=True)
        o_ref[:, h * A_DV:(h + 1) * A_DV] = eg * ks[:, 1, :] + qk * v_new
        kh = k.astype(bf16).astype(f32)
        vh = v_new.astype(bf16).astype(f32)
        kl, vl = k - kh, v_new - vh
        lhs = jnp.where(row == 2, kl[:, None, :], jnp.where(row < 2, kh[:, None, :], 0.0))
        rhs = jnp.where(row == 1, vl[:, None, :], jnp.where(row < 3, vh[:, None, :], 0.0))
        outer = lax.dot_general(lhs.astype(bf16), rhs.astype(bf16), (((1,), (1,)), ((0,), (0,))),
                                preferred_element_type=f32)
        so_ref[:, h] = s * eg[:, :, None] + outer


def _a_step(proj, conv_state, gates, wconv, s0, li, s_prev):
    nb = proj.shape[1]
    bt = A_BT
    in_specs = [
        pl.BlockSpec((None, bt, A_CONV_DIM), lambda i: (0, i, 0)),
        pl.BlockSpec((None, bt, CONV_W - 1, A_CONV_DIM), lambda i: (li, i, 0, 0)),
        pl.BlockSpec((None, bt, LANES), lambda i: (0, i, 0)),
        pl.BlockSpec((None, CONV_W, A_CONV_DIM), lambda i: (li, 0, 0)),
        pl.BlockSpec((None, bt, A_HEADS, A_DK, A_DV), lambda i: (li, i, 0, 0, 0)),
    ]
    args = [proj, conv_state, gates, wconv, s0]
    if li:
        in_specs.append(pl.BlockSpec((li, bt, A_HEADS, A_DK, A_DV), lambda i: (0, i, 0, 0, 0)))
        args.append(s_prev)
    return pl.pallas_call(
        functools.partial(_a_step_kernel, li=li),
        grid=(nb // bt,),
        in_specs=in_specs,
        out_specs=[
            pl.BlockSpec((None, bt, A_V_DIM), lambda i: (0, i, 0)),
            pl.BlockSpec((li + 1, bt, A_HEADS, A_DK, A_DV), lambda i: (0, i, 0, 0, 0)),
        ],
        out_shape=[jax.ShapeDtypeStruct((1, nb, A_V_DIM), f32),
                   jax.ShapeDtypeStruct((li + 1,) + s0.shape[1:], f32)],
        compiler_params=_params(("arbitrary",)),
        name="a_step",
    )(*args)


def _a_out_kernel(x_ref, o_ref, z_ref, gt_ref, won_ref, wo_ref, y_ref, og_ref):
    won = won_ref[...]
    for h in range(A_HEADS):
        sl = slice(h * A_DV, (h + 1) * A_DV)
        og_ref[:, sl] = (_rms(o_ref[:, sl]) * won * _silu(z_ref[:, sl])).astype(bf16)
    y = jnp.dot(og_ref[...], wo_ref[...], preferred_element_type=f32)
    y_ref[...] = x_ref[...] + (1.0 + gt_ref[...]) * y


def _a_out(x, o, proj, mods, l, li, won, wo):
    G, R, _ = x.shape
    tm = min(R, 512)
    return pl.pallas_call(
        _a_out_kernel,
        grid=(G, R // tm),
        in_specs=[
            _tok_spec(tm, D_MODEL),
            _tok_spec(tm, A_V_DIM),
            _tok_spec(tm, A_V_DIM, A_CONV_DIM // A_V_DIM),
            mods.spec(l, 1, 2, tm),
            pl.BlockSpec((None, 1, A_DV), lambda g, i: (li, 0, 0)),
            _resident((None, A_V_DIM, D_MODEL), lambda g, i: (li, 0, 0)),
        ],
        out_specs=_tok_spec(tm, D_MODEL),
        out_shape=jax.ShapeDtypeStruct(x.shape, f32),
        scratch_shapes=[pltpu.VMEM((tm, A_V_DIM), bf16)],
        compiler_params=_params(("arbitrary", "arbitrary")),
        name="a_out",
    )(x, o, proj, mods.arr, won, wo)


def _b_kernel(x_ref, sh_ref, sc_ref, gt_ref, nw_ref, wi_ref, vw_ref, vb_ref, ws_ref, bs_ref, wo_ref,
              *rest, decode):
    if decode:
        y_ref, vout_ref, vn_ref, gd_ref = rest
    else:
        y_ref, vn_ref, gd_ref = rest
    tm = x_ref.shape[0]
    x = x_ref[...]
    h = _norm_mod(x, nw_ref[...], sc_ref[...], sh_ref[...]).astype(bf16)
    hw = B_WIDTH // 2
    vd = [jnp.dot(h, wi_ref[:, B_WIDTH + k * hw:B_WIDTH + (k + 1) * hw], preferred_element_type=f32)
          for k in range(2)]
    udot = lambda g: jnp.dot(h, wi_ref[:, g * B_GROUP_DIM:(g + 1) * B_GROUP_DIM], preferred_element_type=f32)
    ahead = 2
    ud = [udot(g) for g in range(ahead)]
    v = jnp.concatenate([_gelu(vd[0]), _gelu(vd[1])], axis=1)
    mu = jnp.mean(v, axis=-1, keepdims=True)
    vc = v - mu
    vn = vc * lax.rsqrt(jnp.mean(vc * vc, axis=-1, keepdims=True) + EPS) * vw_ref[...] + vb_ref[...]
    if decode:
        vout_ref[...] = vn
    else:
        vn_ref[...] = vn.astype(bf16)
        ri = lax.broadcasted_iota(jnp.int32, (CHUNK_B, CHUNK_B), 0)
        ci = lax.broadcasted_iota(jnp.int32, (CHUNK_B, CHUNK_B), 1)
        causal = ri >= ci
    for g in range(B_GROUPS):
        gs = slice(g * B_GROUP_DIM, (g + 1) * B_GROUP_DIM)
        if g + ahead < B_GROUPS:
            ud.append(udot(g + ahead))
        u = _gelu(ud[g])
        if decode:
            gd_ref[:, gs] = (u * (ws_ref[:, gs] * vn[:, gs] + bs_ref[:, gs])).astype(bf16)
        else:
            wsg = jnp.where(causal, ws_ref[g], 0.0).astype(bf16)
            bias = bs_ref[:, g:g + 1]
            for c in range(tm // CHUNK_B):
                rs = slice(c * CHUNK_B, (c + 1) * CHUNK_B)
                mixed = jnp.dot(wsg, vn_ref[rs, gs], preferred_element_type=f32) + bias
                gd_ref[rs, gs] = (u[rs] * mixed).astype(bf16)
    y = jnp.dot(gd_ref[...], wo_ref[...], preferred_element_type=f32)
    y_ref[...] = x + (1.0 + gt_ref[...]) * y


def _b_mixer(x, mods, l, li, nw3, wi, vw, vb, ws, bs, wo, decode):
    G, R, _ = x.shape
    tm = min(R, 512)
    sub = 1
    if decode:
        ws_spec = pl.BlockSpec((None, 1, B_WIDTH), lambda g, i: (li, 0, 0))
        bs_spec = pl.BlockSpec((None, 1, B_WIDTH), lambda g, i: (li, 0, 0))
        out_specs = [_tok_spec(tm, D_MODEL), _tok_spec(tm, B_WIDTH)]
        out_shape = [jax.ShapeDtypeStruct(x.shape, f32), jax.ShapeDtypeStruct((G, R, B_WIDTH), f32)]
    else:
        ws_spec = pl.BlockSpec((None, B_GROUPS, CHUNK_B, CHUNK_B), lambda g, i: (li, 0, 0, 0))
        bs_spec = pl.BlockSpec((None, CHUNK_B, B_GROUPS), lambda g, i: (li, 0, 0))
        out_specs = _tok_spec(tm, D_MODEL)
        out_shape = jax.ShapeDtypeStruct(x.shape, f32)
    return pl.pallas_call(
        functools.partial(_b_kernel, decode=decode),
        grid=(G, R // tm),
        in_specs=[
            _tok_spec(tm, D_MODEL),
            mods.spec(l, sub, 0, tm), mods.spec(l, sub, 1, tm), mods.spec(l, sub, 2, tm),
            pl.BlockSpec((None, 1, D_MODEL), lambda g, i: (l * N_SUB + sub, 0, 0)),
            _resident((None, D_MODEL, 2 * B_WIDTH), lambda g, i: (li, 0, 0)),
            pl.BlockSpec((None, 1, B_WIDTH), lambda g, i: (li, 0, 0)),
            pl.BlockSpec((None, 1, B_WIDTH), lambda g, i: (li, 0, 0)),
            ws_spec, bs_spec,
            _resident((None, B_WIDTH, D_MODEL), lambda g, i: (li, 0, 0)),
        ],
        out_specs=out_specs,
        out_shape=out_shape,
        scratch_shapes=[pltpu.VMEM((tm, B_WIDTH), bf16), pltpu.VMEM((tm, B_WIDTH), bf16)],
        compiler_params=_params(("arbitrary", "arbitrary")),
        name="b_mixer",
    )(x, mods.arr, mods.arr, mods.arr, nw3, wi, vw, vb, ws, bs, wo)


def _trunk(xp, mp, xs, ms, conv_state, s_state, W):
    conv_p, s_p, conv_s, s_s, v_s = [], [], [], [], []
    for i in range(DEPTH):
        li = i // N_MIXERS
        xp, xs = _ffn(xp, mp, xs, ms, i, 0, 0, W["nw3"], W["wgu"], W["wdn"])
        if i % N_MIXERS == 0:
            proj, gates = _a_in(xs, ms, i, li, W["nw3"], W["a_wm"], W["a_ws"], W["a_alog"], W["a_dtb"])
            o, s_new = _a_step(proj, conv_state, gates, W["a_wconv"], s_state, li,
                               s_s[-1] if s_s else None)
            s_s.append(s_new)
            pre = proj[0, :, None, :A_CONV_DIM]
            conv_s.append(jnp.concatenate([conv_state[li][:, 1:], pre], axis=1))
            xs = _a_out(xs, o, proj, ms, i, li, W["a_won"], W["a_wo"])
            xp, s_new, tail = _a_prompt(xp, mp, i, li, W["nw3"], W["a_wm"], W["a_ws"], W["a_alog"],
                                        W["a_dtb"], W["a_wconv"], W["a_won"], W["a_wo"])
            s_p.append(s_new)
            conv_p.append(tail[:, 7::8])
        else:
            xs, v = _b_mixer(xs, ms, i, li, W["nw3"], W["b_wi"], W["b_vw"], W["b_vb"],
                             W["b_ws00"], W["b_bs0"], W["b_wo"], decode=True)
            v_s.append(v)
            xp = _b_mixer(xp, mp, i, li, W["nw3"], W["b_wi"], W["b_vw"], W["b_vb"],
                          W["b_ws"], W["b_bsT"], W["b_wo"], decode=False)
        fw = W["final_w"] if i == DEPTH - 1 else None
        xp, xs = _ffn(xp, mp, xs, ms, i, 1, 2, W["nw3"], W["wgu"], W["wdn"], final_w=fw)
    return xp, xs, conv_p, s_p, conv_s, s_s, v_s


def kernel(x_prompt, x_sample, state_a_conv, state_a_S, c_prompt, c_sample, w_ada, b_ada, norm_w, ffn_w_gu, ffn_w_down, a_w_in, a_w_conv, a_log, a_dt_bias, a_w_onorm, a_w_out, b_w_in, b_vnorm_w, b_vnorm_b, b_w_s, b_b_s, b_w_out, final_norm_w):
    n_a = a_w_in.shape[0]
    n_b = b_w_in.shape[0]
    bp, seq, _ = x_prompt.shape
    bs = x_sample.shape[0]
    assert x_sample.shape[1] == 1 and seq % A_LT == 0 and seq % CHUNK_B == 0

    mod_p, mod_s = _ada(c_prompt, c_sample, w_ada, b_ada)
    mods_p = _Mods(mod_p.reshape(DEPTH, bp, 1, N_SUB * 3 * D_MODEL), per_row=False)
    mods_s = _Mods(mod_s.reshape(DEPTH, 1, bs, N_SUB * 3 * D_MODEL), per_row=True)

    pad = LANES - 2 * A_HEADS
    gate_pad = lambda a: jnp.pad(a, ((0, 0), (A_HEADS, pad)))[:, None, :]
    W = {
        "nw3": norm_w.reshape(DEPTH * N_SUB, 1, D_MODEL),
        "wgu": ffn_w_gu.astype(bf16),
        "wdn": ffn_w_down.astype(bf16),
        "a_wm": a_w_in.astype(bf16),
        "a_ws": jnp.pad(a_w_in[:, :, A_MAIN_DIM:], ((0, 0), (0, 0), (0, pad))),
        "a_alog": gate_pad(a_log),
        "a_dtb": gate_pad(a_dt_bias),
        "a_wconv": a_w_conv,
        "a_won": a_w_onorm[:, None, :],
        "a_wo": a_w_out.astype(bf16),
        "b_wi": b_w_in.astype(bf16),
        "b_vw": b_vnorm_w[:, None, :],
        "b_vb": b_vnorm_b[:, None, :],
        "b_ws": b_w_s,
        "b_bsT": jnp.swapaxes(b_b_s, 1, 2),
        "b_ws00": jnp.repeat(b_w_s[:, :, 0, 0], B_GROUP_DIM, axis=1)[:, None, :],
        "b_bs0": jnp.repeat(b_b_s[:, :, 0], B_GROUP_DIM, axis=1)[:, None, :],
        "b_wo": b_w_out.astype(bf16),
        "final_w": final_norm_w,
    }

    y_p, y_s, conv_p, s_p, conv_s, s_s, v_s = _trunk(
        x_prompt, mods_p, x_sample.reshape(1, bs, D_MODEL), mods_s, state_a_conv, state_a_S, W)
    return (y_p,
            y_s.reshape(bs, 1, D_MODEL),
            jnp.stack(conv_p),
            jnp.stack(s_p),
            jnp.stack(conv_s),
            s_s[-1],
            jnp.stack(v_s).reshape(n_b, bs, 1, B_WIDTH))
```

```python
import functools

import jax
import jax.numpy as jnp
import numpy as np
from jax import lax
from jax.experimental import pallas as pl
from jax.experimental.pallas import tpu as pltpu

f32 = jnp.float32
bf16 = jnp.bfloat16

D_MODEL = 1024
DEPTH = 4
N_MIXERS = 2
A_HEADS = 8
A_DK = 128
A_DV = 128
A_QK_DIM = A_HEADS * A_DK
A_V_DIM = A_HEADS * A_DV
A_CONV_DIM = 2 * A_QK_DIM + A_V_DIM
A_MAIN_DIM = A_CONV_DIM + A_V_DIM
CONV_W = 4
CHUNK_A = 64
B_WIDTH = 2 * D_MODEL
B_GROUPS = 8
B_GROUP_DIM = B_WIDTH // B_GROUPS
CHUNK_B = 128
D_FF = 2816
FFN_RES = 0.5
N_SUB = 3
EPS = 1e-6

LANES = 128
FF_TILE = 256
GATE_COLS = 3 * A_HEADS
VMEM_LIMIT = 56 * 1024 * 1024


def _params(sem):
    return pltpu.CompilerParams(dimension_semantics=sem, vmem_limit_bytes=VMEM_LIMIT)


def _resident(shape, index_map):
    return pl.BlockSpec(shape, index_map, pipeline_mode=pl.Buffered(1))


def _bdot(a, b):
    return jnp.dot(a.astype(bf16), b.astype(bf16), preferred_element_type=f32)


def _bdot_nt(a, b):
    return lax.dot_general(a.astype(bf16), b.astype(bf16), (((1,), (1,)), ((), ())),
                           preferred_element_type=f32)


def _bdot_tn(a, b):
    return lax.dot_general(a.astype(bf16), b.astype(bf16), (((0,), (0,)), ((), ())),
                           preferred_element_type=f32)


def _split2(a):
    hi = a.astype(bf16)
    lo = (a - hi.astype(f32)).astype(bf16)
    return hi, lo


def _split3(a):
    hi = a.astype(bf16)
    r = a - hi.astype(f32)
    mid = r.astype(bf16)
    lo = (r - mid.astype(f32)).astype(bf16)
    return hi, mid, lo


def _dot3(a, b):
    ah, al = _split2(a)
    bh, bl = _split2(b)
    d = functools.partial(jnp.dot, preferred_element_type=f32)
    rows = a.shape[0]
    p = d(jnp.concatenate([ah, al], axis=0), bh)
    return p[:rows] + p[rows:] + d(ah, bl)


LOG2E = 1.4426950408889634


def _silu(x):
    return x / (1.0 + jnp.exp2(x * (-LOG2E)))


def _softplus(x):
    return jnp.maximum(x, 0.0) + jnp.log1p(jnp.exp(-jnp.abs(x)))


GELU_A = -LOG2E * 2.0 * (2.0 / 3.141592653589793) ** 0.5
GELU_B = GELU_A * 0.044715


def _gelu(x):
    return x / (1.0 + jnp.exp2(x * (GELU_A + GELU_B * (x * x))))


def _rms(x):
    return x * lax.rsqrt(jnp.mean(x * x, axis=-1, keepdims=True) + EPS)


def _norm_mod(x, nw, sc, sh):
    return (_rms(x) * nw) * (1.0 + sc) + sh


def _l2n(x):
    return x * lax.rsqrt(jnp.sum(x * x, axis=-1, keepdims=True) + EPS)


def _ada_kernel(c_ref, w_ref, b_ref, op_ref, os_ref):
    bp = op_ref.shape[0]
    mod = _bdot(_silu(c_ref[...]), w_ref[...].astype(bf16)) + b_ref[...]
    op_ref[...] = mod[:bp]
    os_ref[...] = mod[bp:]


def _ada(c_p, c_s, w_ada, b_ada):
    n = N_SUB * 3 * D_MODEL
    tn = 18 * LANES
    bp, bs = c_p.shape[0], c_s.shape[0]
    assert bp % 8 == 0
    return pl.pallas_call(
        _ada_kernel,
        grid=(DEPTH, n // tn),
        in_specs=[
            pl.BlockSpec((bp + bs, D_MODEL), lambda l, j: (0, 0)),
            pl.BlockSpec((None, D_MODEL, tn), lambda l, j: (l, 0, j)),
            pl.BlockSpec((None, 1, tn), lambda l, j: (l, 0, j)),
        ],
        out_specs=[
            pl.BlockSpec((None, bp, tn), lambda l, j: (l, 0, j)),
            pl.BlockSpec((None, bs, tn), lambda l, j: (l, 0, j)),
        ],
        out_shape=[jax.ShapeDtypeStruct((DEPTH, bp, n), f32),
                   jax.ShapeDtypeStruct((DEPTH, bs, n), f32)],
        compiler_params=_params(("arbitrary", "arbitrary")),
        name="ada",
    )(jnp.concatenate([c_p, c_s], axis=0), w_ada, b_ada.reshape(DEPTH, 1, n))


class _Mods:
    def __init__(self, arr, per_row):
        self.arr = arr
        self.per_row = per_row

    def spec(self, l, sub, kind, tm):
        col = sub * 3 + kind
        if self.per_row:
            return pl.BlockSpec((None, None, tm, D_MODEL), lambda g, i: (l, g, i, col))
        return pl.BlockSpec((None, None, 1, D_MODEL), lambda g, i: (l, g, 0, col))


def _tok_spec(tm, width, colblk=0):
    return pl.BlockSpec((None, tm, width), lambda g, i: (g, i, colblk))


def _ffn_kernel(xp_ref, shp_ref, scp_ref, gtp_ref, xs_ref, shs_ref, scs_ref, gts_ref,
                nw_ref, wg_ref, wu_ref, wd_ref, *rest, final, n_prompt, n_jobs):
    rest = list(rest)
    fw_ref = rest.pop(0) if final else None
    src_refs = [rest.pop(0) for _ in range(n_jobs)]
    op_ref, os_ref = rest.pop(0), rest.pop(0)
    dst_refs = [rest.pop(0) for _ in range(n_jobs)]
    act_ref, = rest

    def body(x_ref, sh_ref, sc_ref, gt_ref, o_ref, cast):
        if cast:
            for src, dst in zip(src_refs, dst_refs):
                dst[...] = src[...].astype(bf16)
        rows = x_ref.shape[0]
        x = x_ref[...]
        h = _norm_mod(x, nw_ref[...], sc_ref[...], sh_ref[...]).astype(bf16)
        for c in range(D_FF // FF_TILE):
            sl = slice(c * FF_TILE, (c + 1) * FF_TILE)
            g = jnp.dot(h, wg_ref[:, sl], preferred_element_type=f32)
            u = jnp.dot(h, wu_ref[:, sl], preferred_element_type=f32)
            act_ref[0:rows, sl] = (_silu(g) * u).astype(bf16)
        y = jnp.dot(act_ref[0:rows, :], wd_ref[...], preferred_element_type=f32)
        out = x + FFN_RES * (1.0 + gt_ref[...]) * y
        if final:
            out = _rms(out) * fw_ref[...]
        o_ref[...] = out

    i = pl.program_id(0)
    pl.when(i < n_prompt)(lambda: body(xp_ref, shp_ref, scp_ref, gtp_ref, op_ref, True))
    pl.when(i == n_prompt)(lambda: body(xs_ref, shs_ref, scs_ref, gts_ref, os_ref, False))


def _ffn(xp, mods_p, xs, mods_s, l, sub, nw3, wgu, wdn, jobs=(), final_w=None):
    B, L, _ = xp.shape
    bs = xs.shape[1]
    tm = min(L, 1024)
    tps = L // tm
    n_prompt = B * tps
    final = final_w is not None
    tile = lambda i: jnp.minimum(i, n_prompt - 1)
    pmod = lambda kind: pl.BlockSpec((None, None, 1, D_MODEL),
                                     lambda i: (l, tile(i) // tps, 0, sub * 3 + kind))
    smod = lambda kind: pl.BlockSpec((None, None, bs, D_MODEL), lambda i: (l, 0, 0, sub * 3 + kind))
    in_specs = [
        pl.BlockSpec((None, tm, D_MODEL), lambda i: (tile(i) // tps, tile(i) % tps, 0)),
        pmod(0), pmod(1), pmod(2),
        pl.BlockSpec((None, bs, D_MODEL), lambda i: (0, 0, 0)),
        smod(0), smod(1), smod(2),
        pl.BlockSpec((None, 1, D_MODEL), lambda i: (l * N_SUB + sub, 0, 0)),
        _resident((None, D_MODEL, D_FF), lambda i: (0, 0, 0)),
        _resident((None, D_MODEL, D_FF), lambda i: (0, 0, 1)),
        _resident((None, D_FF, D_MODEL), lambda i: (0, 0, 0)),
    ]
    args = [xp, mods_p.arr, mods_p.arr, mods_p.arr, xs, mods_s.arr, mods_s.arr, mods_s.arr,
            nw3, wgu, wgu, wdn]
    if final:
        in_specs.append(pl.BlockSpec((1, D_MODEL), lambda i: (0, 0)))
        args.append(final_w.reshape(1, D_MODEL))
    out_specs = [
        pl.BlockSpec((None, tm, D_MODEL), lambda i: (tile(i) // tps, tile(i) % tps, 0)),
        pl.BlockSpec((None, bs, D_MODEL), lambda i: (0, 0, 0)),
    ]
    out_shape = [jax.ShapeDtypeStruct(xp.shape, f32), jax.ShapeDtypeStruct(xs.shape, f32)]
    for arr, idx in jobs:
        lead = arr.shape[:-2]
        flat = arr.reshape((-1,) + arr.shape[-2:])
        k = int(np.ravel_multi_index(idx, lead))
        rows, cols = flat.shape[1:]
        assert rows % (n_prompt * 16) == 0
        chunk = rows // n_prompt
        in_specs.append(pl.BlockSpec((None, chunk, cols), lambda i, k=k: (k, tile(i), 0)))
        args.append(flat)
        out_specs.append(pl.BlockSpec((None, chunk, cols), lambda i: (0, tile(i), 0)))
        out_shape.append(jax.ShapeDtypeStruct((1, rows, cols), bf16))
    return pl.pallas_call(
        functools.partial(_ffn_kernel, final=final, n_prompt=n_prompt, n_jobs=len(jobs)),
        grid=(n_prompt + 1,),
        in_specs=in_specs,
        out_specs=out_specs,
        out_shape=out_shape,
        scratch_shapes=[pltpu.VMEM((tm, D_FF), bf16)],
        compiler_params=_params(("arbitrary",)),
        name="ffn",
    )(*args)


def _gate_values(hf, ws_ref, alog_ref, dtb_ref):
    p = _dot3(hf, ws_ref[...])
    lane = lax.broadcasted_iota(jnp.int32, p.shape, 1)
    beta = jax.nn.sigmoid(p)
    g = -jnp.exp(alog_ref[...]) * _softplus(p + dtb_ref[...])
    is_g = (lane >= A_HEADS) & (lane < 2 * A_HEADS)
    return jnp.where(lane < A_HEADS, beta, jnp.where(is_g, g, 0.0))


def _a_in_kernel(x_ref, sh_ref, sc_ref, nw_ref, wm_ref, ws_ref, alog_ref, dtb_ref, proj_ref, gates_ref):
    hf = _norm_mod(x_ref[...], nw_ref[...], sc_ref[...], sh_ref[...])
    h = hf.astype(bf16)
    for c in range(A_MAIN_DIM // D_MODEL):
        sl = slice(c * D_MODEL, (c + 1) * D_MODEL)
        proj_ref[:, sl] = jnp.dot(h, wm_ref[:, sl], preferred_element_type=f32)
    gates_ref[...] = _gate_values(hf, ws_ref, alog_ref, dtb_ref)


def _a_in(x, mods, l, li, nw3, wm, ws, alog, dtb):
    G, R, _ = x.shape
    tm = min(R, 512)
    sub = 1
    return pl.pallas_call(
        _a_in_kernel,
        grid=(G, R // tm),
        in_specs=[
            _tok_spec(tm, D_MODEL),
            mods.spec(l, sub, 0, tm), mods.spec(l, sub, 1, tm),
            pl.BlockSpec((None, 1, D_MODEL), lambda g, i: (l * N_SUB + sub, 0, 0)),
            _resident((None, D_MODEL, A_MAIN_DIM), lambda g, i: (0, 0, 0)),
            _resident((None, D_MODEL, LANES), lambda g, i: (li, 0, 0)),
            pl.BlockSpec((None, 1, LANES), lambda g, i: (li, 0, 0)),
            pl.BlockSpec((None, 1, LANES), lambda g, i: (li, 0, 0)),
        ],
        out_specs=[_tok_spec(tm, A_MAIN_DIM), _tok_spec(tm, LANES)],
        out_shape=[jax.ShapeDtypeStruct((G, R, A_MAIN_DIM), f32),
                   jax.ShapeDtypeStruct((G, R, LANES), f32)],
        compiler_params=_params(("arbitrary", "arbitrary")),
        name="a_in",
    )(x, mods.arr, mods.arr, nw3, wm, ws, alog, dtb)


A_LT = 256
A_GROUP = 2


def _a_prompt_kernel(x_ref, sh_ref, sc_ref, gt_ref, nw_ref, wm_ref, ws_ref, alog_ref, dtb_ref, wc_ref,
                     won_ref, wo_ref, perm_ref, permt_ref, ptri_ref,
                     y_ref, s_ref, tail_ref, hist_ref, act_ref, o_scr, og_ref):
    t = pl.program_id(1)
    lt = x_ref.shape[0]
    nc = lt // CHUNK_A

    @pl.when(t == 0)
    def _():
        s_ref[...] = jnp.zeros(s_ref.shape, f32)
        hist_ref[...] = jnp.zeros(hist_ref.shape, f32)

    x = x_ref[...]
    hf = _norm_mod(x, nw_ref[...], sc_ref[...], sh_ref[...])
    h = hf.astype(bf16)

    ri = lax.broadcasted_iota(jnp.int32, (CHUNK_A, CHUNK_A), 0)
    ci = lax.broadcasted_iota(jnp.int32, (CHUNK_A, CHUNK_A), 1)
    pos_r = (ri % 8) * (CHUNK_A // 8) + ri // 8
    pos_c = (ci % 8) * (CHUNK_A // 8) + ci // 8
    causal = pos_r >= pos_c
    strict = pos_r > pos_c
    hp = jnp.dot(perm_ref[...], h, preferred_element_type=f32).astype(bf16)

    gates = _gate_values(hf, ws_ref, alog_ref, dtb_ref)
    lane = lax.broadcasted_iota(jnp.int32, gates.shape, 1)
    is_g = (lane >= A_HEADS) & (lane < 2 * A_HEADS)
    gparts = _split3(gates)
    sparts = _split3(pltpu.roll(jnp.where(is_g, gates, 0.0), A_HEADS, 1))
    ptri = ptri_ref[...]
    gcols, grows = [], []
    for c in range(nc):
        rs = slice(c * CHUNK_A, (c + 1) * CHUNK_A)
        gc = None
        for gp, sp in zip(gparts, sparts):
            term = jnp.dot(ptri, jnp.concatenate([gp[rs], sp[rs]], axis=0), preferred_element_type=f32)
            gc = term if gc is None else gc + term
        gcols.append(gc)
        grows.append(gc.T)

    ng = CHUNK_A // 8
    sub = lax.broadcasted_iota(jnp.int32, (8, 2 * A_DK), 0)
    pair = 2 * A_DK
    for p in range(A_HEADS // 2):
        for part in range(3):
            col = part * A_QK_DIM + p * pair
            cs = slice(col, col + pair)
            pre = jnp.dot(hp, wm_ref[:, cs], preferred_element_type=f32)
            grp = lambda c, i: pre[c * CHUNK_A + 8 * i:c * CHUNK_A + 8 * i + 8]
            acc = pre * wc_ref[CONV_W - 1:CONV_W, cs]
            shifted = [[] for _ in range(CONV_W)]
            for c in range(nc):
                w = []
                for j in range(CONV_W - 1):
                    i = ng - (CONV_W - 1) + j
                    prev = hist_ref[8 * j:8 * j + 8, cs] if c == 0 else grp(c - 1, i)
                    w.append(pltpu.roll(jnp.where(sub == 7, prev, grp(c, i)), 1, 0))
                for k_ in range(1, CONV_W):
                    shifted[k_] += w[CONV_W - 1 - k_:] + [pre[c * CHUNK_A:(c + 1) * CHUNK_A - 8 * k_]]
            hist_ref[:, cs] = pre[lt - 8 * (CONV_W - 1):]
            for k_ in range(1, CONV_W):
                acc = acc + jnp.concatenate(shifted[k_], axis=0) * wc_ref[CONV_W - 1 - k_:CONV_W - k_, cs]
            a2 = _silu(acc)
            for hh in range(2):
                a = a2[:, hh * A_DK:(hh + 1) * A_DK]
                if part == 0:
                    a = a * (lax.rsqrt(jnp.sum(a * a, axis=-1, keepdims=True) + EPS) * (A_DK ** -0.5))
                elif part == 1:
                    a = _l2n(a)
                act_ref[part * A_HEADS + 2 * p + hh] = a
    tail_ref[...] = hist_ref[...]

    z = jnp.dot(hp, wm_ref[:, A_CONV_DIM:A_MAIN_DIM], preferred_element_type=f32)
    won = won_ref[...]
    zg = [_silu(z[:, h_ * A_DV:(h_ + 1) * A_DV]) * won for h_ in range(A_HEADS)]

    states = [s_ref[h_] for h_ in range(A_HEADS)]

    def recurrence_pieces(c, insts):
        rs = slice(c * CHUNK_A, (c + 1) * CHUNK_A)
        tmp = {}

        def first():
            tmp["ws_qs"] = [jnp.dot(i["lhs1"], states[h_].astype(bf16), preferred_element_type=f32)
                            for h_, i in enumerate(insts)]

        def second():
            tmp["r"] = [jnp.dot(i["lhs2"], (i["u"] - w[:CHUNK_A]).astype(bf16), preferred_element_type=f32)
                        for i, w in zip(insts, tmp["ws_qs"])]

        def third():
            for h_, (i, w, r) in enumerate(zip(insts, tmp["ws_qs"], tmp["r"])):
                o_scr[rs, h_ * A_DV:(h_ + 1) * A_DV] = w[CHUNK_A:] + r[:CHUNK_A]
                states[h_] = states[h_] * i["edl"] + r[CHUNK_A:]

        return [first, second, third]

    pending = []
    for c0 in range(0, nc, A_GROUP):
        chunks = range(c0, min(c0 + A_GROUP, nc))

        inst = []
        for c in chunks:
            rs = slice(c * CHUNK_A, (c + 1) * CHUNK_A)
            for h_ in range(A_HEADS):
                qc = act_ref[h_, rs, :]
                kc = act_ref[A_HEADS + h_, rs, :]
                vc = act_ref[2 * A_HEADS + h_, rs, :]
                bcol = gcols[c][:, h_:h_ + 1]
                dcol = gcols[c][:, 2 * A_HEADS + h_:2 * A_HEADS + h_ + 1]
                drow = grows[c][2 * A_HEADS + h_:2 * A_HEADS + h_ + 1, :]
                dlast = dcol[CHUNK_A - 1:CHUNK_A, :]
                gam = jnp.where(causal, jnp.exp(jnp.where(causal, dcol - drow, 0.0)), 0.0)
                kb = kc * bcol
                ed = jnp.exp(dcol)
                kt = kc.T
                kq = _bdot(jnp.concatenate([kb, qc], axis=0), kt)
                m = jnp.where(strict, kq[:CHUNK_A] * gam, 0.0)
                attn = kq[CHUNK_A:] * gam
                kdec_t = kt * jnp.exp(dlast - drow)
                inst.append(dict(
                    m=m,
                    rhs=jnp.concatenate([vc * bcol, kb * ed], axis=1),
                    qed=(qc * ed).astype(bf16),
                    lhs2=jnp.concatenate([attn, kdec_t], axis=0).astype(bf16),
                    edl=jnp.exp(dlast)))

        ts = [-i["m"] for i in inst]
        ms = [_bdot(i["m"], i["m"]) for i in inst]
        n = 2
        while n < CHUNK_A:
            if 2 * n < CHUNK_A:
                prod = [_bdot(jnp.concatenate([t, m], axis=0), m) for t, m in zip(ts, ms)]
                ts = [t + m + p[:CHUNK_A] for t, m, p in zip(ts, ms, prod)]
                ms = [p[CHUNK_A:] for p in prod]
            else:
                ts = [t + m + _bdot(t, m) for t, m in zip(ts, ms)]
            n *= 2
            if pending:
                pending.pop(0)()
        rs_ = [-(i["m"] + t + _dot3(i["m"], t)) for i, t in zip(inst, ts)]
        if pending:
            pending.pop(0)()
        ts = [t + r + _bdot(t, r) for t, r in zip(ts, rs_)]
        while pending:
            pending.pop(0)()

        for i, t in zip(inst, ts):
            uw = i["rhs"] + _bdot(t, i["rhs"])
            i["u"] = uw[:, :A_DV]
            i["lhs1"] = jnp.concatenate([uw[:, A_DV:].astype(bf16), i["qed"]], axis=0)
        for k_, c in enumerate(chunks):
            pending += recurrence_pieces(c, inst[k_ * A_HEADS:(k_ + 1) * A_HEADS])
    while pending:
        pending.pop(0)()
    for h_ in range(A_HEADS):
        s_ref[h_] = states[h_]

    for h_ in range(A_HEADS):
        sl = slice(h_ * A_DV, (h_ + 1) * A_DV)
        og_ref[:, sl] = (_rms(o_scr[:, sl]) * zg[h_]).astype(bf16)
    og = jnp.dot(permt_ref[...], og_ref[...], preferred_element_type=f32).astype(bf16)
    y = jnp.dot(og, wo_ref[...], preferred_element_type=f32)
    y_ref[...] = x + (1.0 + gt_ref[...]) * y


def _a_prompt(x, mods, l, li, nw3, wm, ws, alog, dtb, wconv, won, wo):
    B, L, _ = x.shape
    lt = A_LT
    sub = 1
    const = lambda shape: pl.BlockSpec((None,) + shape, lambda b, t: (li,) + (0,) * len(shape))
    r = np.arange(CHUNK_A)
    pc = (np.arange(CHUNK_A)[None, :] == ((r % 8) * (CHUNK_A // 8) + r // 8)[:, None]).astype(np.float32)
    perm_np = np.kron(np.eye(lt // CHUNK_A, dtype=np.float32), pc)
    perm = jnp.asarray(perm_np, dtype=bf16)
    permt = jnp.asarray(perm_np.T, dtype=bf16)
    ptri = jnp.asarray(np.concatenate([pc, pc @ np.tril(np.ones((CHUNK_A, CHUNK_A), np.float32))], axis=1),
                       dtype=bf16)
    hist_rows = 8 * (CONV_W - 1)
    return pl.pallas_call(
        _a_prompt_kernel,
        grid=(B, L // lt),
        in_specs=[
            _tok_spec(lt, D_MODEL),
            mods.spec(l, sub, 0, lt), mods.spec(l, sub, 1, lt), mods.spec(l, sub, 2, lt),
            pl.BlockSpec((None, 1, D_MODEL), lambda b, t: (l * N_SUB + sub, 0, 0)),
            _resident((None, D_MODEL, A_MAIN_DIM), lambda b, t: (0, 0, 0)),
            _resident((None, D_MODEL, LANES), lambda b, t: (li, 0, 0)),
            const((1, LANES)), const((1, LANES)),
            const((CONV_W, A_CONV_DIM)),
            const((1, A_DV)),
            _resident((None, A_V_DIM, D_MODEL), lambda b, t: (0, 0, 0)),
            _resident((lt, lt), lambda b, t: (0, 0)),
            _resident((lt, lt), lambda b, t: (0, 0)),
            _resident((CHUNK_A, 2 * CHUNK_A), lambda b, t: (0, 0)),
        ],
        out_specs=[
            _tok_spec(lt, D_MODEL),
            pl.BlockSpec((None, A_HEADS, A_DK, A_DV), lambda b, t: (b, 0, 0, 0)),
            pl.BlockSpec((None, hist_rows, A_CONV_DIM), lambda b, t: (b, 0, 0)),
        ],
        out_shape=[jax.ShapeDtypeStruct(x.shape, f32),
                   jax.ShapeDtypeStruct((B, A_HEADS, A_DK, A_DV), f32),
                   jax.ShapeDtypeStruct((B, hist_rows, A_CONV_DIM), f32)],
        scratch_shapes=[pltpu.VMEM((hist_rows, A_CONV_DIM), f32),
                        pltpu.VMEM((3 * A_HEADS, lt, A_DK), f32),
                        pltpu.VMEM((lt, A_V_DIM), f32),
                        pltpu.VMEM((lt, A_V_DIM), bf16)],
        compiler_params=_params(("arbitrary", "arbitrary")),
        name="a_prompt",
    )(x, mods.arr, mods.arr, mods.arr, nw3, wm, ws, alog, dtb, wconv, won, wo, perm, permt, ptri)


A_BT = 8


def _a_step_kernel(pre_ref, conv_ref, gates_ref, wc_ref, s_ref, *rest, li):
    if li:
        prev_ref, o_ref, so_all_ref = rest
        so_all_ref[0:li] = prev_ref[...]
    else:
        o_ref, so_all_ref = rest
    so_ref = so_all_ref.at[li]
    bt = pre_ref.shape[0]
    wc = wc_ref[...]
    acc = pre_ref[...] * wc[CONV_W - 1:CONV_W, :]
    for j in range(CONV_W - 1):
        acc = acc + conv_ref[:, j, :] * wc[j:j + 1, :]
    qkv = _silu(acc)
    gates = gates_ref[...]
    row = lax.broadcasted_iota(jnp.int32, (bt, 8, LANES), 1)
    for h in range(A_HEADS):
        q = _l2n(qkv[:, h * A_DK:(h + 1) * A_DK]) * (A_DK ** -0.5)
        k = _l2n(qkv[:, A_QK_DIM + h * A_DK:A_QK_DIM + (h + 1) * A_DK])
        v = qkv[:, 2 * A_QK_DIM + h * A_DV:2 * A_QK_DIM + (h + 1) * A_DV]
        beta = gates[:, h:h + 1]
        eg = jnp.exp(gates[:, A_HEADS + h:A_HEADS + h + 1])
        s = s_ref[:, h]
        kq = jnp.where(row == 0, k[:, None, :], jnp.where(row == 1, q[:, None, :], 0.0))
        ks = jnp.einsum('bmk,bkn->bmn', kq.astype(bf16), s.astype(bf16),
                        preferred_element_type=f32)
        v_new = beta * (v - eg * ks[:, 0, :])
        qk = jnp.sum(q * k, axis=-1, keepdims=True)
        o_ref[:, h * A_DV:(h + 1) * A_DV] = eg * ks[:, 1, :] + qk * v_new
        kh = k.astype(bf16).astype(f32)
        vh = v_new.astype(bf16).astype(f32)
        kl, vl = k - kh, v_new - vh
        lhs = jnp.where(row == 2, kl[:, None, :], jnp.where(row < 2, kh[:, None, :], 0.0))
        rhs = jnp.where(row == 1, vl[:, None, :], jnp.where(row < 3, vh[:, None, :], 0.0))
        outer = lax.dot_general(lhs.astype(bf16), rhs.astype(bf16), (((1,), (1,)), ((0,), (0,))),
                                preferred_element_type=f32)
        so_ref[:, h] = s * eg[:, :, None] + outer


def _a_step(proj, conv_state, gates, wconv, s0, li, s_prev):
    nb = proj.shape[1]
    bt = A_BT
    in_specs = [
        pl.BlockSpec((None, bt, A_CONV_DIM), lambda i: (0, i, 0)),
        pl.BlockSpec((None, bt, CONV_W - 1, A_CONV_DIM), lambda i: (li, i, 0, 0)),
        pl.BlockSpec((None, bt, LANES), lambda i: (0, i, 0)),
        pl.BlockSpec((None, CONV_W, A_CONV_DIM), lambda i: (li, 0, 0)),
        pl.BlockSpec((None, bt, A_HEADS, A_DK, A_DV), lambda i: (li, i, 0, 0, 0)),
    ]
    args = [proj, conv_state, gates, wconv, s0]
    if li:
        in_specs.append(pl.BlockSpec((li, bt, A_HEADS, A_DK, A_DV), lambda i: (0, i, 0, 0, 0)))
        args.append(s_prev)
    return pl.pallas_call(
        functools.partial(_a_step_kernel, li=li),
        grid=(nb // bt,),
        in_specs=in_specs,
        out_specs=[
            pl.BlockSpec((None, bt, A_V_DIM), lambda i: (0, i, 0)),
            pl.BlockSpec((li + 1, bt, A_HEADS, A_DK, A_DV), lambda i: (0, i, 0, 0, 0)),
        ],
        out_shape=[jax.ShapeDtypeStruct((1, nb, A_V_DIM), f32),
                   jax.ShapeDtypeStruct((li + 1,) + s0.shape[1:], f32)],
        compiler_params=_params(("arbitrary",)),
        name="a_step",
    )(*args)


def _a_out_kernel(x_ref, o_ref, z_ref, gt_ref, won_ref, wo_ref, y_ref, og_ref):
    won = won_ref[...]
    for h in range(A_HEADS):
        sl = slice(h * A_DV, (h + 1) * A_DV)
        og_ref[:, sl] = (_rms(o_ref[:, sl]) * won * _silu(z_ref[:, sl])).astype(bf16)
    y = jnp.dot(og_ref[...], wo_ref[...], preferred_element_type=f32)
    y_ref[...] = x_ref[...] + (1.0 + gt_ref[...]) * y


def _a_out(x, o, proj, mods, l, li, won, wo):
    G, R, _ = x.shape
    tm = min(R, 512)
    return pl.pallas_call(
        _a_out_kernel,
        grid=(G, R // tm),
        in_specs=[
            _tok_spec(tm, D_MODEL),
            _tok_spec(tm, A_V_DIM),
            _tok_spec(tm, A_V_DIM, A_CONV_DIM // A_V_DIM),
            mods.spec(l, 1, 2, tm),
            pl.BlockSpec((None, 1, A_DV), lambda g, i: (li, 0, 0)),
            _resident((None, A_V_DIM, D_MODEL), lambda g, i: (0, 0, 0)),
        ],
        out_specs=_tok_spec(tm, D_MODEL),
        out_shape=jax.ShapeDtypeStruct(x.shape, f32),
        scratch_shapes=[pltpu.VMEM((tm, A_V_DIM), bf16)],
        compiler_params=_params(("arbitrary", "arbitrary")),
        name="a_out",
    )(x, o, proj, mods.arr, won, wo)


def _b_kernel(x_ref, sh_ref, sc_ref, gt_ref, nw_ref, wi_ref, vw_ref, vb_ref, ws_ref, bs_ref, wo_ref,
              *rest, decode):
    if decode:
        y_ref, vout_ref, vn_ref, gd_ref = rest
    else:
        y_ref, vn_ref, gd_ref = rest
    tm = x_ref.shape[0]
    x = x_ref[...]
    h = _norm_mod(x, nw_ref[...], sc_ref[...], sh_ref[...]).astype(bf16)
    hw = B_WIDTH // 2
    vd = [jnp.dot(h, wi_ref[:, B_WIDTH + k * hw:B_WIDTH + (k + 1) * hw], preferred_element_type=f32)
          for k in range(2)]
    udot = lambda g: jnp.dot(h, wi_ref[:, g * B_GROUP_DIM:(g + 1) * B_GROUP_DIM], preferred_element_type=f32)
    ahead = 2
    ud = [udot(g) for g in range(ahead)]
    v = jnp.concatenate([_gelu(vd[0]), _gelu(vd[1])], axis=1)
    mu = jnp.mean(v, axis=-1, keepdims=True)
    vc = v - mu
    vn = vc * lax.rsqrt(jnp.mean(vc * vc, axis=-1, keepdims=True) + EPS) * vw_ref[...] + vb_ref[...]
    if decode:
        vout_ref[...] = vn
    else:
        vn_ref[...] = vn.astype(bf16)
        ri = lax.broadcasted_iota(jnp.int32, (CHUNK_B, CHUNK_B), 0)
        ci = lax.broadcasted_iota(jnp.int32, (CHUNK_B, CHUNK_B), 1)
        causal = ri >= ci
    for g in range(B_GROUPS):
        gs = slice(g * B_GROUP_DIM, (g + 1) * B_GROUP_DIM)
        if g + ahead < B_GROUPS:
            ud.append(udot(g + ahead))
        u = _gelu(ud[g])
        if decode:
            gd_ref[:, gs] = (u * (ws_ref[:, gs] * vn[:, gs] + bs_ref[:, gs])).astype(bf16)
        else:
            wsg = jnp.where(causal, ws_ref[g], 0.0).astype(bf16)
            bias = bs_ref[:, g:g + 1]
            for c in range(tm // CHUNK_B):
                rs = slice(c * CHUNK_B, (c + 1) * CHUNK_B)
                mixed = jnp.dot(wsg, vn_ref[rs, gs], preferred_element_type=f32) + bias
                gd_ref[rs, gs] = (u[rs] * mixed).astype(bf16)
    y = jnp.dot(gd_ref[...], wo_ref[...], preferred_element_type=f32)
    y_ref[...] = x + (1.0 + gt_ref[...]) * y


def _b_mixer(x, mods, l, li, nw3, wi, vw, vb, ws, bs, wo, decode):
    G, R, _ = x.shape
    tm = min(R, 512)
    sub = 1
    if decode:
        ws_spec = pl.BlockSpec((None, 1, B_WIDTH), lambda g, i: (li, 0, 0))
        bs_spec = pl.BlockSpec((None, 1, B_WIDTH), lambda g, i: (li, 0, 0))
        out_specs = [_tok_spec(tm, D_MODEL), _tok_spec(tm, B_WIDTH)]
        out_shape = [jax.ShapeDtypeStruct(x.shape, f32), jax.ShapeDtypeStruct((G, R, B_WIDTH), f32)]
    else:
        ws_spec = pl.BlockSpec((None, B_GROUPS, CHUNK_B, CHUNK_B), lambda g, i: (li, 0, 0, 0))
        bs_spec = pl.BlockSpec((None, CHUNK_B, B_GROUPS), lambda g, i: (li, 0, 0))
        out_specs = _tok_spec(tm, D_MODEL)
        out_shape = jax.ShapeDtypeStruct(x.shape, f32)
    return pl.pallas_call(
        functools.partial(_b_kernel, decode=decode),
        grid=(G, R // tm),
        in_specs=[
            _tok_spec(tm, D_MODEL),
            mods.spec(l, sub, 0, tm), mods.spec(l, sub, 1, tm), mods.spec(l, sub, 2, tm),
            pl.BlockSpec((None, 1, D_MODEL), lambda g, i: (l * N_SUB + sub, 0, 0)),
            _resident((None, D_MODEL, 2 * B_WIDTH), lambda g, i: (0, 0, 0)),
            pl.BlockSpec((None, 1, B_WIDTH), lambda g, i: (li, 0, 0)),
            pl.BlockSpec((None, 1, B_WIDTH), lambda g, i: (li, 0, 0)),
            ws_spec, bs_spec,
            _resident((None, B_WIDTH, D_MODEL), lambda g, i: (0, 0, 0)),
        ],
        out_specs=out_specs,
        out_shape=out_shape,
        scratch_shapes=[pltpu.VMEM((tm, B_WIDTH), bf16), pltpu.VMEM((tm, B_WIDTH), bf16)],
        compiler_params=_params(("arbitrary", "arbitrary")),
        name="b_mixer",
    )(x, mods.arr, mods.arr, mods.arr, nw3, wi, vw, vb, ws, bs, wo)


def _trunk(xp, mp, xs, ms, conv_state, s_state, W):
    conv_p, s_p, conv_s, s_s, v_s = [], [], [], [], []
    wgu = W["ffn_gu"][0, 0][None].astype(bf16)
    wdn = W["ffn_dn"][0, 0][None].astype(bf16)
    for i in range(DEPTH):
        li = i // N_MIXERS
        is_a = i % N_MIXERS == 0
        mix = ("a_w_in", "a_w_out") if is_a else ("b_w_in", "b_w_out")
        jobs = [(W["ffn_gu"], (i, 1)), (W["ffn_dn"], (i, 1))] + [(W[n], (li,)) for n in mix]
        xp, xs, wgu, wdn, w_in, w_out = _ffn(xp, mp, xs, ms, i, 0, W["nw3"], wgu, wdn, jobs=jobs)
        if is_a:
            proj, gates = _a_in(xs, ms, i, li, W["nw3"], w_in, W["a_ws"], W["a_alog"], W["a_dtb"])
            o, s_new = _a_step(proj, conv_state, gates, W["a_wconv"], s_state, li,
                               s_s[-1] if s_s else None)
            s_s.append(s_new)
            pre = proj[0, :, None, :A_CONV_DIM]
            conv_s.append(jnp.concatenate([conv_state[li][:, 1:], pre], axis=1))
            xs = _a_out(xs, o, proj, ms, i, li, W["a_won"], w_out)
            xp, s_new, tail = _a_prompt(xp, mp, i, li, W["nw3"], w_in, W["a_ws"], W["a_alog"],
                                        W["a_dtb"], W["a_wconv"], W["a_won"], w_out)
            s_p.append(s_new)
            conv_p.append(tail[:, 7::8])
        else:
            xs, v = _b_mixer(xs, ms, i, li, W["nw3"], w_in, W["b_vw"], W["b_vb"],
                             W["b_ws00"], W["b_bs0"], w_out, decode=True)
            v_s.append(v)
            xp = _b_mixer(xp, mp, i, li, W["nw3"], w_in, W["b_vw"], W["b_vb"],
                          W["b_ws"], W["b_bsT"], w_out, decode=False)
        last = i == DEPTH - 1
        jobs = [] if last else [(W["ffn_gu"], (i + 1, 0)), (W["ffn_dn"], (i + 1, 0))]
        out = _ffn(xp, mp, xs, ms, i, 2, W["nw3"], wgu, wdn, jobs=jobs,
                   final_w=W["final_w"] if last else None)
        xp, xs = out[:2]
        if jobs:
            wgu, wdn = out[2:]
    return xp, xs, conv_p, s_p, conv_s, s_s, v_s


def kernel(x_prompt, x_sample, state_a_conv, state_a_S, c_prompt, c_sample, w_ada, b_ada, norm_w, ffn_w_gu, ffn_w_down, a_w_in, a_w_conv, a_log, a_dt_bias, a_w_onorm, a_w_out, b_w_in, b_vnorm_w, b_vnorm_b, b_w_s, b_b_s, b_w_out, final_norm_w):
    n_a = a_w_in.shape[0]
    n_b = b_w_in.shape[0]
    bp, seq, _ = x_prompt.shape
    bs = x_sample.shape[0]
    assert x_sample.shape[1] == 1 and seq % A_LT == 0 and seq % CHUNK_B == 0

    mod_p, mod_s = _ada(c_prompt, c_sample, w_ada, b_ada)
    mods_p = _Mods(mod_p.reshape(DEPTH, bp, 1, N_SUB * 3 * D_MODEL), per_row=False)
    mods_s = _Mods(mod_s.reshape(DEPTH, 1, bs, N_SUB * 3 * D_MODEL), per_row=True)

    pad = LANES - 2 * A_HEADS
    gate_pad = lambda a: jnp.pad(a, ((0, 0), (A_HEADS, pad)))[:, None, :]
    W = {
        "nw3": norm_w.reshape(DEPTH * N_SUB, 1, D_MODEL),
        "ffn_gu": ffn_w_gu,
        "ffn_dn": ffn_w_down,
        "a_w_in": a_w_in,
        "a_w_out": a_w_out,
        "b_w_in": b_w_in,
        "b_w_out": b_w_out,
        "a_ws": jnp.pad(a_w_in[:, :, A_MAIN_DIM:], ((0, 0), (0, 0), (0, pad))),
        "a_alog": gate_pad(a_log),
        "a_dtb": gate_pad(a_dt_bias),
        "a_wconv": a_w_conv,
        "a_won": a_w_onorm[:, None, :],
        "b_vw": b_vnorm_w[:, None, :],
        "b_vb": b_vnorm_b[:, None, :],
        "b_ws": b_w_s,
        "b_bsT": jnp.swapaxes(b_b_s, 1, 2),
        "b_ws00": jnp.repeat(b_w_s[:, :, 0, 0], B_GROUP_DIM, axis=1)[:, None, :],
        "b_bs0": jnp.repeat(b_b_s[:, :, 0], B_GROUP_DIM, axis=1)[:, None, :],
        "final_w": final_norm_w,
    }

    y_p, y_s, conv_p, s_p, conv_s, s_s, v_s = _trunk(
        x_prompt, mods_p, x_sample.reshape(1, bs, D_MODEL), mods_s, state_a_conv, state_a_S, W)
    return (y_p,
            y_s.reshape(bs, 1, D_MODEL),
            jnp.stack(conv_p),
            jnp.stack(s_p),
            jnp.stack(conv_s),
            s_s[-1],
            jnp.stack(v_s).reshape(n_b, bs, 1, B_WIDTH))
```

```python
import functools

import jax
import jax.numpy as jnp
import numpy as np
from jax import lax
from jax.experimental import pallas as pl
from jax.experimental.pallas import tpu as pltpu

f32 = jnp.float32
bf16 = jnp.bfloat16

D_MODEL = 1024
DEPTH = 4
N_MIXERS = 2
A_HEADS = 8
A_DK = 128
A_DV = 128
A_QK_DIM = A_HEADS * A_DK
A_V_DIM = A_HEADS * A_DV
A_CONV_DIM = 2 * A_QK_DIM + A_V_DIM
A_MAIN_DIM = A_CONV_DIM + A_V_DIM
CONV_W = 4
CHUNK_A = 64
B_WIDTH = 2 * D_MODEL
B_GROUPS = 8
B_GROUP_DIM = B_WIDTH // B_GROUPS
CHUNK_B = 128
D_FF = 2816
FFN_RES = 0.5
N_SUB = 3
EPS = 1e-6

LANES = 128
FF_TILE = 256
FFN_PIECES = 4
GATE_COLS = 3 * A_HEADS
VMEM_LIMIT = 56 * 1024 * 1024


def _params(sem):
    return pltpu.CompilerParams(dimension_semantics=sem, vmem_limit_bytes=VMEM_LIMIT)


def _resident(shape, index_map):
    return pl.BlockSpec(shape, index_map, pipeline_mode=pl.Buffered(1))


def _bdot(a, b):
    return jnp.dot(a.astype(bf16), b.astype(bf16), preferred_element_type=f32)


def _bdot_nt(a, b):
    return lax.dot_general(a.astype(bf16), b.astype(bf16), (((1,), (1,)), ((), ())),
                           preferred_element_type=f32)


def _bdot_tn(a, b):
    return lax.dot_general(a.astype(bf16), b.astype(bf16), (((0,), (0,)), ((), ())),
                           preferred_element_type=f32)


def _split2(a):
    hi = a.astype(bf16)
    lo = (a - hi.astype(f32)).astype(bf16)
    return hi, lo


def _split3(a):
    hi = a.astype(bf16)
    r = a - hi.astype(f32)
    mid = r.astype(bf16)
    lo = (r - mid.astype(f32)).astype(bf16)
    return hi, mid, lo


def _dot3(a, b):
    ah, al = _split2(a)
    bh, bl = _split2(b)
    d = functools.partial(jnp.dot, preferred_element_type=f32)
    rows = a.shape[0]
    p = d(jnp.concatenate([ah, al], axis=0), bh)
    return p[:rows] + p[rows:] + d(ah, bl)


LOG2E = 1.4426950408889634


def _silu(x):
    return x / (1.0 + jnp.exp2(x * (-LOG2E)))


def _softplus(x):
    return jnp.maximum(x, 0.0) + jnp.log1p(jnp.exp(-jnp.abs(x)))


GELU_A = -LOG2E * 2.0 * (2.0 / 3.141592653589793) ** 0.5
GELU_B = GELU_A * 0.044715


def _gelu(x):
    return x / (1.0 + jnp.exp2(x * (GELU_A + GELU_B * (x * x))))


def _rms(x):
    return x * lax.rsqrt(jnp.mean(x * x, axis=-1, keepdims=True) + EPS)


def _norm_mod(x, nw, sc, sh):
    return (_rms(x) * nw) * (1.0 + sc) + sh


def _l2n(x):
    return x * lax.rsqrt(jnp.sum(x * x, axis=-1, keepdims=True) + EPS)


def _ada_kernel(c_ref, w_ref, b_ref, op_ref, os_ref):
    bp = op_ref.shape[0]
    mod = _bdot(_silu(c_ref[...]), w_ref[...].astype(bf16)) + b_ref[...]
    op_ref[...] = mod[:bp]
    os_ref[...] = mod[bp:]


def _ada(c_p, c_s, w_ada, b_ada):
    n = N_SUB * 3 * D_MODEL
    tn = 18 * LANES
    bp, bs = c_p.shape[0], c_s.shape[0]
    assert bp % 8 == 0
    return pl.pallas_call(
        _ada_kernel,
        grid=(DEPTH, n // tn),
        in_specs=[
            pl.BlockSpec((bp + bs, D_MODEL), lambda l, j: (0, 0)),
            pl.BlockSpec((None, D_MODEL, tn), lambda l, j: (l, 0, j)),
            pl.BlockSpec((None, 1, tn), lambda l, j: (l, 0, j)),
        ],
        out_specs=[
            pl.BlockSpec((None, bp, tn), lambda l, j: (l, 0, j)),
            pl.BlockSpec((None, bs, tn), lambda l, j: (l, 0, j)),
        ],
        out_shape=[jax.ShapeDtypeStruct((DEPTH, bp, n), f32),
                   jax.ShapeDtypeStruct((DEPTH, bs, n), f32)],
        compiler_params=_params(("arbitrary", "arbitrary")),
        name="ada",
    )(jnp.concatenate([c_p, c_s], axis=0), w_ada, b_ada.reshape(DEPTH, 1, n))


class _Mods:
    def __init__(self, arr, per_row):
        self.arr = arr
        self.per_row = per_row

    def spec(self, l, sub, kind, tm):
        col = sub * 3 + kind
        if self.per_row:
            return pl.BlockSpec((None, None, tm, D_MODEL), lambda g, i: (l, g, i, col))
        return pl.BlockSpec((None, None, 1, D_MODEL), lambda g, i: (l, g, 0, col))


def _tok_spec(tm, width, colblk=0):
    return pl.BlockSpec((None, tm, width), lambda g, i: (g, i, colblk))


def _ffn_kernel(xp_ref, shp_ref, scp_ref, gtp_ref, xs_ref, shs_ref, scs_ref, gts_ref,
                nw_ref, wg_ref, wu_ref, wd_ref, *rest, final, n_prompt, n_jobs):
    rest = list(rest)
    fw_ref = rest.pop(0) if final else None
    src_refs = [rest.pop(0) for _ in range(n_jobs)]
    op_ref, os_ref = rest.pop(0), rest.pop(0)
    dst_refs = [rest.pop(0) for _ in range(n_jobs)]
    act_ref, h_ref = rest

    def body(x_ref, sh_ref, sc_ref, gt_ref, o_ref, cast):
        if cast:
            for src, dst in zip(src_refs, dst_refs):
                dst[...] = src[...].astype(bf16)
        rows = x_ref.shape[0]
        per_row = sc_ref.shape[0] == rows
        mod = lambda ref, rs: ref[rs, :] if per_row else ref[...]
        pieces = FFN_PIECES if rows % (FFN_PIECES * LANES) == 0 else 1
        pr = rows // pieces
        sl0 = slice(0, FF_TILE)
        for p in range(pieces):
            rs = slice(p * pr, (p + 1) * pr)
            hq = _norm_mod(x_ref[rs, :], nw_ref[...], mod(sc_ref, rs), mod(sh_ref, rs)).astype(bf16)
            h_ref[rs, :] = hq
            g = jnp.dot(hq, wg_ref[:, sl0], preferred_element_type=f32)
            u = jnp.dot(hq, wu_ref[:, sl0], preferred_element_type=f32)
            act_ref[rs, sl0] = (_silu(g) * u).astype(bf16)
        h = h_ref[0:rows, :]
        for c in range(1, D_FF // FF_TILE):
            sl = slice(c * FF_TILE, (c + 1) * FF_TILE)
            g = jnp.dot(h, wg_ref[:, sl], preferred_element_type=f32)
            u = jnp.dot(h, wu_ref[:, sl], preferred_element_type=f32)
            act_ref[0:rows, sl] = (_silu(g) * u).astype(bf16)
        halves = 1
        hr = rows // halves
        ys = [jnp.dot(act_ref[k * hr:(k + 1) * hr, :], wd_ref[...], preferred_element_type=f32)
              for k in range(halves)]
        for k in range(halves):
            rs = slice(k * hr, (k + 1) * hr)
            out = x_ref[rs, :] + FFN_RES * (1.0 + mod(gt_ref, rs)) * ys[k]
            if final:
                out = _rms(out) * fw_ref[...]
            o_ref[rs, :] = out

    i = pl.program_id(0)
    pl.when(i < n_prompt)(lambda: body(xp_ref, shp_ref, scp_ref, gtp_ref, op_ref, True))
    pl.when(i == n_prompt)(lambda: body(xs_ref, shs_ref, scs_ref, gts_ref, os_ref, False))


def _ffn(xp, mods_p, xs, mods_s, l, sub, nw3, wgu, wdn, jobs=(), final_w=None):
    B, L, _ = xp.shape
    bs = xs.shape[1]
    tm = min(L, 1024)
    tps = L // tm
    n_prompt = B * tps
    final = final_w is not None
    tile = lambda i: jnp.minimum(i, n_prompt - 1)
    pmod = lambda kind: pl.BlockSpec((None, None, 1, D_MODEL),
                                     lambda i: (l, tile(i) // tps, 0, sub * 3 + kind))
    smod = lambda kind: pl.BlockSpec((None, None, bs, D_MODEL), lambda i: (l, 0, 0, sub * 3 + kind))
    in_specs = [
        pl.BlockSpec((None, tm, D_MODEL), lambda i: (tile(i) // tps, tile(i) % tps, 0)),
        pmod(0), pmod(1), pmod(2),
        pl.BlockSpec((None, bs, D_MODEL), lambda i: (0, 0, 0)),
        smod(0), smod(1), smod(2),
        pl.BlockSpec((None, 1, D_MODEL), lambda i: (l * N_SUB + sub, 0, 0)),
        _resident((None, D_MODEL, D_FF), lambda i: (0, 0, 0)),
        _resident((None, D_MODEL, D_FF), lambda i: (0, 0, 1)),
        _resident((None, D_FF, D_MODEL), lambda i: (0, 0, 0)),
    ]
    args = [xp, mods_p.arr, mods_p.arr, mods_p.arr, xs, mods_s.arr, mods_s.arr, mods_s.arr,
            nw3, wgu, wgu, wdn]
    if final:
        in_specs.append(pl.BlockSpec((1, D_MODEL), lambda i: (0, 0)))
        args.append(final_w.reshape(1, D_MODEL))
    out_specs = [
        pl.BlockSpec((None, tm, D_MODEL), lambda i: (tile(i) // tps, tile(i) % tps, 0)),
        pl.BlockSpec((None, bs, D_MODEL), lambda i: (0, 0, 0)),
    ]
    out_shape = [jax.ShapeDtypeStruct(xp.shape, f32), jax.ShapeDtypeStruct(xs.shape, f32)]
    for arr, idx in jobs:
        lead = arr.shape[:-2]
        flat = arr.reshape((-1,) + arr.shape[-2:])
        k = int(np.ravel_multi_index(idx, lead))
        rows, cols = flat.shape[1:]
        assert rows % (n_prompt * 16) == 0
        chunk = rows // n_prompt
        in_specs.append(pl.BlockSpec((None, chunk, cols), lambda i, k=k: (k, tile(i), 0)))
        args.append(flat)
        out_specs.append(pl.BlockSpec((None, chunk, cols), lambda i: (0, tile(i), 0)))
        out_shape.append(jax.ShapeDtypeStruct((1, rows, cols), bf16))
    return pl.pallas_call(
        functools.partial(_ffn_kernel, final=final, n_prompt=n_prompt, n_jobs=len(jobs)),
        grid=(n_prompt + 1,),
        in_specs=in_specs,
        out_specs=out_specs,
        out_shape=out_shape,
        scratch_shapes=[pltpu.VMEM((tm, D_FF), bf16), pltpu.VMEM((tm, D_MODEL), bf16)],
        compiler_params=_params(("arbitrary",)),
        name="ffn",
    )(*args)


def _gate_values(hf, ws_ref, alog_ref, dtb_ref):
    p = _dot3(hf, ws_ref[...])
    lane = lax.broadcasted_iota(jnp.int32, p.shape, 1)
    beta = jax.nn.sigmoid(p)
    g = -jnp.exp(alog_ref[...]) * _softplus(p + dtb_ref[...])
    is_g = (lane >= A_HEADS) & (lane < 2 * A_HEADS)
    return jnp.where(lane < A_HEADS, beta, jnp.where(is_g, g, 0.0))


def _a_in_kernel(x_ref, sh_ref, sc_ref, nw_ref, wm_ref, ws_ref, alog_ref, dtb_ref, proj_ref, gates_ref):
    hf = _norm_mod(x_ref[...], nw_ref[...], sc_ref[...], sh_ref[...])
    h = hf.astype(bf16)
    for c in range(A_MAIN_DIM // D_MODEL):
        sl = slice(c * D_MODEL, (c + 1) * D_MODEL)
        proj_ref[:, sl] = jnp.dot(h, wm_ref[:, sl], preferred_element_type=f32)
    gates_ref[...] = _gate_values(hf, ws_ref, alog_ref, dtb_ref)


def _a_in(x, mods, l, li, nw3, wm, ws, alog, dtb):
    G, R, _ = x.shape
    tm = min(R, 512)
    sub = 1
    return pl.pallas_call(
        _a_in_kernel,
        grid=(G, R // tm),
        in_specs=[
            _tok_spec(tm, D_MODEL),
            mods.spec(l, sub, 0, tm), mods.spec(l, sub, 1, tm),
            pl.BlockSpec((None, 1, D_MODEL), lambda g, i: (l * N_SUB + sub, 0, 0)),
            _resident((None, D_MODEL, A_MAIN_DIM), lambda g, i: (0, 0, 0)),
            _resident((None, D_MODEL, LANES), lambda g, i: (li, 0, 0)),
            pl.BlockSpec((None, 1, LANES), lambda g, i: (li, 0, 0)),
            pl.BlockSpec((None, 1, LANES), lambda g, i: (li, 0, 0)),
        ],
        out_specs=[_tok_spec(tm, A_MAIN_DIM), _tok_spec(tm, LANES)],
        out_shape=[jax.ShapeDtypeStruct((G, R, A_MAIN_DIM), f32),
                   jax.ShapeDtypeStruct((G, R, LANES), f32)],
        compiler_params=_params(("arbitrary", "arbitrary")),
        name="a_in",
    )(x, mods.arr, mods.arr, nw3, wm, ws, alog, dtb)


A_LT = 256
A_GROUP = 2


def _a_prompt_kernel(x_ref, sh_ref, sc_ref, gt_ref, nw_ref, wm_ref, ws_ref, alog_ref, dtb_ref, wc_ref,
                     won_ref, wo_ref, perm_ref, permt_ref, ptri_ref,
                     y_ref, s_ref, tail_ref, hist_ref, act_ref, o_scr, og_ref):
    t = pl.program_id(1)
    lt = x_ref.shape[0]
    nc = lt // CHUNK_A

    @pl.when(t == 0)
    def _():
        s_ref[...] = jnp.zeros(s_ref.shape, f32)
        hist_ref[...] = jnp.zeros(hist_ref.shape, f32)

    x = x_ref[...]
    hf = _norm_mod(x, nw_ref[...], sc_ref[...], sh_ref[...])
    h = hf.astype(bf16)

    ri = lax.broadcasted_iota(jnp.int32, (CHUNK_A, CHUNK_A), 0)
    ci = lax.broadcasted_iota(jnp.int32, (CHUNK_A, CHUNK_A), 1)
    pos_r = (ri % 8) * (CHUNK_A // 8) + ri // 8
    pos_c = (ci % 8) * (CHUNK_A // 8) + ci // 8
    causal = pos_r >= pos_c
    strict = pos_r > pos_c
    hp = jnp.dot(perm_ref[...], h, preferred_element_type=f32).astype(bf16)

    gates = _gate_values(hf, ws_ref, alog_ref, dtb_ref)
    lane = lax.broadcasted_iota(jnp.int32, gates.shape, 1)
    is_g = (lane >= A_HEADS) & (lane < 2 * A_HEADS)
    gparts = _split3(gates)
    sparts = _split3(pltpu.roll(jnp.where(is_g, gates, 0.0), A_HEADS, 1))
    ptri = ptri_ref[...]
    gcols, grows = [], []
    for c in range(nc):
        rs = slice(c * CHUNK_A, (c + 1) * CHUNK_A)
        gc = None
        for gp, sp in zip(gparts, sparts):
            term = jnp.dot(ptri, jnp.concatenate([gp[rs], sp[rs]], axis=0), preferred_element_type=f32)
            gc = term if gc is None else gc + term
        gcols.append(gc)
        grows.append(gc.T)

    ng = CHUNK_A // 8
    sub = lax.broadcasted_iota(jnp.int32, (8, 2 * A_DK), 0)
    pair = 2 * A_DK
    for p in range(A_HEADS // 2):
        for part in range(3):
            col = part * A_QK_DIM + p * pair
            cs = slice(col, col + pair)
            pre = jnp.dot(hp, wm_ref[:, cs], preferred_element_type=f32)
            grp = lambda c, i: pre[c * CHUNK_A + 8 * i:c * CHUNK_A + 8 * i + 8]
            acc = pre * wc_ref[CONV_W - 1:CONV_W, cs]
            shifted = [[] for _ in range(CONV_W)]
            for c in range(nc):
                w = []
                for j in range(CONV_W - 1):
                    i = ng - (CONV_W - 1) + j
                    prev = hist_ref[8 * j:8 * j + 8, cs] if c == 0 else grp(c - 1, i)
                    w.append(pltpu.roll(jnp.where(sub == 7, prev, grp(c, i)), 1, 0))
                for k_ in range(1, CONV_W):
                    shifted[k_] += w[CONV_W - 1 - k_:] + [pre[c * CHUNK_A:(c + 1) * CHUNK_A - 8 * k_]]
            hist_ref[:, cs] = pre[lt - 8 * (CONV_W - 1):]
            for k_ in range(1, CONV_W):
                acc = acc + jnp.concatenate(shifted[k_], axis=0) * wc_ref[CONV_W - 1 - k_:CONV_W - k_, cs]
            a2 = _silu(acc)
            for hh in range(2):
                a = a2[:, hh * A_DK:(hh + 1) * A_DK]
                if part == 0:
                    a = a * (lax.rsqrt(jnp.sum(a * a, axis=-1, keepdims=True) + EPS) * (A_DK ** -0.5))
                elif part == 1:
                    a = _l2n(a)
                act_ref[part * A_HEADS + 2 * p + hh] = a
    tail_ref[...] = hist_ref[...]

    z = jnp.dot(hp, wm_ref[:, A_CONV_DIM:A_MAIN_DIM], preferred_element_type=f32)
    won = won_ref[...]
    zg = [_silu(z[:, h_ * A_DV:(h_ + 1) * A_DV]) * won for h_ in range(A_HEADS)]

    states = [s_ref[h_] for h_ in range(A_HEADS)]

    def recurrence_pieces(c, insts):
        rs = slice(c * CHUNK_A, (c + 1) * CHUNK_A)
        tmp = {}

        def first():
            tmp["ws_qs"] = [jnp.dot(i["lhs1"], states[h_].astype(bf16), preferred_element_type=f32)
                            for h_, i in enumerate(insts)]

        def second():
            tmp["r"] = [jnp.dot(i["lhs2"], (i["u"] - w[:CHUNK_A]).astype(bf16), preferred_element_type=f32)
                        for i, w in zip(insts, tmp["ws_qs"])]

        def third():
            for h_, (i, w, r) in enumerate(zip(insts, tmp["ws_qs"], tmp["r"])):
                o_scr[rs, h_ * A_DV:(h_ + 1) * A_DV] = w[CHUNK_A:] + r[:CHUNK_A]
                states[h_] = states[h_] * i["edl"] + r[CHUNK_A:]

        return [first, second, third]

    pending = []
    for c0 in range(0, nc, A_GROUP):
        chunks = range(c0, min(c0 + A_GROUP, nc))

        inst = []
        for c in chunks:
            rs = slice(c * CHUNK_A, (c + 1) * CHUNK_A)
            for h_ in range(A_HEADS):
                qc = act_ref[h_, rs, :]
                kc = act_ref[A_HEADS + h_, rs, :]
                vc = act_ref[2 * A_HEADS + h_, rs, :]
                bcol = gcols[c][:, h_:h_ + 1]
                dcol = gcols[c][:, 2 * A_HEADS + h_:2 * A_HEADS + h_ + 1]
                drow = grows[c][2 * A_HEADS + h_:2 * A_HEADS + h_ + 1, :]
                dlast = dcol[CHUNK_A - 1:CHUNK_A, :]
                gam = jnp.where(causal, jnp.exp(jnp.where(causal, dcol - drow, 0.0)), 0.0)
                kb = kc * bcol
                ed = jnp.exp(dcol)
                kt = kc.T
                kq = _bdot(jnp.concatenate([kb, qc], axis=0), kt)
                m = jnp.where(strict, kq[:CHUNK_A] * gam, 0.0)
                attn = kq[CHUNK_A:] * gam
                kdec_t = kt * jnp.exp(dlast - drow)
                inst.append(dict(
                    m=m,
                    rhs=jnp.concatenate([vc * bcol, kb * ed], axis=1),
                    qed=(qc * ed).astype(bf16),
                    lhs2=jnp.concatenate([attn, kdec_t], axis=0).astype(bf16),
                    edl=jnp.exp(dlast)))

        ts = [-i["m"] for i in inst]
        ms = [_bdot(i["m"], i["m"]) for i in inst]
        n = 2
        while n < CHUNK_A:
            if 2 * n < CHUNK_A:
                prod = [_bdot(jnp.concatenate([t, m], axis=0), m) for t, m in zip(ts, ms)]
                ts = [t + m + p[:CHUNK_A] for t, m, p in zip(ts, ms, prod)]
                ms = [p[CHUNK_A:] for p in prod]
            else:
                ts = [t + m + _bdot(t, m) for t, m in zip(ts, ms)]
            n *= 2
            if pending:
                pending.pop(0)()
        rs_ = [-(i["m"] + t + _dot3(i["m"], t)) for i, t in zip(inst, ts)]
        if pending:
            pending.pop(0)()
        ts = [t + r + _bdot(t, r) for t, r in zip(ts, rs_)]
        while pending:
            pending.pop(0)()

        for i, t in zip(inst, ts):
            uw = i["rhs"] + _bdot(t, i["rhs"])
            i["u"] = uw[:, :A_DV]
            i["lhs1"] = jnp.concatenate([uw[:, A_DV:].astype(bf16), i["qed"]], axis=0)
        for k_, c in enumerate(chunks):
            pending += recurrence_pieces(c, inst[k_ * A_HEADS:(k_ + 1) * A_HEADS])
    while pending:
        pending.pop(0)()
    for h_ in range(A_HEADS):
        s_ref[h_] = states[h_]

    for h_ in range(A_HEADS):
        sl = slice(h_ * A_DV, (h_ + 1) * A_DV)
        og_ref[:, sl] = (_rms(o_scr[:, sl]) * zg[h_]).astype(bf16)
    og = jnp.dot(permt_ref[...], og_ref[...], preferred_element_type=f32).astype(bf16)
    y = jnp.dot(og, wo_ref[...], preferred_element_type=f32)
    y_ref[...] = x + (1.0 + gt_ref[...]) * y


def _a_prompt(x, mods, l, li, nw3, wm, ws, alog, dtb, wconv, won, wo):
    B, L, _ = x.shape
    lt = A_LT
    sub = 1
    const = lambda shape: pl.BlockSpec((None,) + shape, lambda b, t: (li,) + (0,) * len(shape))
    r = np.arange(CHUNK_A)
    pc = (np.arange(CHUNK_A)[None, :] == ((r % 8) * (CHUNK_A // 8) + r // 8)[:, None]).astype(np.float32)
    perm_np = np.kron(np.eye(lt // CHUNK_A, dtype=np.float32), pc)
    perm = jnp.asarray(perm_np, dtype=bf16)
    permt = jnp.asarray(perm_np.T, dtype=bf16)
    ptri = jnp.asarray(np.concatenate([pc, pc @ np.tril(np.ones((CHUNK_A, CHUNK_A), np.float32))], axis=1),
                       dtype=bf16)
    hist_rows = 8 * (CONV_W - 1)
    return pl.pallas_call(
        _a_prompt_kernel,
        grid=(B, L // lt),
        in_specs=[
            _tok_spec(lt, D_MODEL),
            mods.spec(l, sub, 0, lt), mods.spec(l, sub, 1, lt), mods.spec(l, sub, 2, lt),
            pl.BlockSpec((None, 1, D_MODEL), lambda b, t: (l * N_SUB + sub, 0, 0)),
            _resident((None, D_MODEL, A_MAIN_DIM), lambda b, t: (0, 0, 0)),
            _resident((None, D_MODEL, LANES), lambda b, t: (li, 0, 0)),
            const((1, LANES)), const((1, LANES)),
            const((CONV_W, A_CONV_DIM)),
            const((1, A_DV)),
            _resident((None, A_V_DIM, D_MODEL), lambda b, t: (0, 0, 0)),
            _resident((lt, lt), lambda b, t: (0, 0)),
            _resident((lt, lt), lambda b, t: (0, 0)),
            _resident((CHUNK_A, 2 * CHUNK_A), lambda b, t: (0, 0)),
        ],
        out_specs=[
            _tok_spec(lt, D_MODEL),
            pl.BlockSpec((None, A_HEADS, A_DK, A_DV), lambda b, t: (b, 0, 0, 0)),
            pl.BlockSpec((None, hist_rows, A_CONV_DIM), lambda b, t: (b, 0, 0)),
        ],
        out_shape=[jax.ShapeDtypeStruct(x.shape, f32),
                   jax.ShapeDtypeStruct((B, A_HEADS, A_DK, A_DV), f32),
                   jax.ShapeDtypeStruct((B, hist_rows, A_CONV_DIM), f32)],
        scratch_shapes=[pltpu.VMEM((hist_rows, A_CONV_DIM), f32),
                        pltpu.VMEM((3 * A_HEADS, lt, A_DK), f32),
                        pltpu.VMEM((lt, A_V_DIM), f32),
                        pltpu.VMEM((lt, A_V_DIM), bf16)],
        compiler_params=_params(("arbitrary", "arbitrary")),
        name="a_prompt",
    )(x, mods.arr, mods.arr, mods.arr, nw3, wm, ws, alog, dtb, wconv, won, wo, perm, permt, ptri)


A_BT = 8


def _a_step_kernel(pre_ref, conv_ref, gates_ref, wc_ref, s_ref, *rest, li):
    if li:
        prev_ref, o_ref, so_all_ref = rest
        so_all_ref[0:li] = prev_ref[...]
    else:
        o_ref, so_all_ref = rest
    so_ref = so_all_ref.at[li]
    bt = pre_ref.shape[0]
    wc = wc_ref[...]
    acc = pre_ref[...] * wc[CONV_W - 1:CONV_W, :]
    for j in range(CONV_W - 1):
        acc = acc + conv_ref[:, j, :] * wc[j:j + 1, :]
    qkv = _silu(acc)
    gates = gates_ref[...]
    row = lax.broadcasted_iota(jnp.int32, (bt, 8, LANES), 1)
    for h in range(A_HEADS):
        q = _l2n(qkv[:, h * A_DK:(h + 1) * A_DK]) * (A_DK ** -0.5)
        k = _l2n(qkv[:, A_QK_DIM + h * A_DK:A_QK_DIM + (h + 1) * A_DK])
        v = qkv[:, 2 * A_QK_DIM + h * A_DV:2 * A_QK_DIM + (h + 1) * A_DV]
        beta = gates[:, h:h + 1]
        eg = jnp.exp(gates[:, A_HEADS + h:A_HEADS + h + 1])
        s = s_ref[:, h]
        kq = jnp.where(row == 0, k[:, None, :], jnp.where(row == 1, q[:, None, :], 0.0))
        ks = jnp.einsum('bmk,bkn->bmn', kq.astype(bf16), s.astype(bf16),
                        preferred_element_type=f32)
        v_new = beta * (v - eg * ks[:, 0, :])
        qk = jnp.sum(q * k, axis=-1, keepdims=True)
        o_ref[:, h * A_DV:(h + 1) * A_DV] = eg * ks[:, 1, :] + qk * v_new
        kh = k.astype(bf16).astype(f32)
        vh = v_new.astype(bf16).astype(f32)
        kl, vl = k - kh, v_new - vh
        lhs = jnp.where(row == 2, kl[:, None, :], jnp.where(row < 2, kh[:, None, :], 0.0))
        rhs = jnp.where(row == 1, vl[:, None, :], jnp.where(row < 3, vh[:, None, :], 0.0))
        outer = lax.dot_general(lhs.astype(bf16), rhs.astype(bf16), (((1,), (1,)), ((0,), (0,))),
                                preferred_element_type=f32)
        so_ref[:, h] = s * eg[:, :, None] + outer


def _a_step(proj, conv_state, gates, wconv, s0, li, s_prev):
    nb = proj.shape[1]
    bt = A_BT
    in_specs = [
        pl.BlockSpec((None, bt, A_CONV_DIM), lambda i: (0, i, 0)),
        pl.BlockSpec((None, bt, CONV_W - 1, A_CONV_DIM), lambda i: (li, i, 0, 0)),
        pl.BlockSpec((None, bt, LANES), lambda i: (0, i, 0)),
        pl.BlockSpec((None, CONV_W, A_CONV_DIM), lambda i: (li, 0, 0)),
        pl.BlockSpec((None, bt, A_HEADS, A_DK, A_DV), lambda i: (li, i, 0, 0, 0)),
    ]
    args = [proj, conv_state, gates, wconv, s0]
    if li:
        in_specs.append(pl.BlockSpec((li, bt, A_HEADS, A_DK, A_DV), lambda i: (0, i, 0, 0, 0)))
        args.append(s_prev)
    return pl.pallas_call(
        functools.partial(_a_step_kernel, li=li),
        grid=(nb // bt,),
        in_specs=in_specs,
        out_specs=[
            pl.BlockSpec((None, bt, A_V_DIM), lambda i: (0, i, 0)),
            pl.BlockSpec((li + 1, bt, A_HEADS, A_DK, A_DV), lambda i: (0, i, 0, 0, 0)),
        ],
        out_shape=[jax.ShapeDtypeStruct((1, nb, A_V_DIM), f32),
                   jax.ShapeDtypeStruct((li + 1,) + s0.shape[1:], f32)],
        compiler_params=_params(("arbitrary",)),
        name="a_step",
    )(*args)


def _a_out_kernel(x_ref, o_ref, z_ref, gt_ref, won_ref, wo_ref, y_ref, og_ref):
    won = won_ref[...]
    for h in range(A_HEADS):
        sl = slice(h * A_DV, (h + 1) * A_DV)
        og_ref[:, sl] = (_rms(o_ref[:, sl]) * won * _silu(z_ref[:, sl])).astype(bf16)
    y = jnp.dot(og_ref[...], wo_ref[...], preferred_element_type=f32)
    y_ref[...] = x_ref[...] + (1.0 + gt_ref[...]) * y


def _a_out(x, o, proj, mods, l, li, won, wo):
    G, R, _ = x.shape
    tm = min(R, 512)
    return pl.pallas_call(
        _a_out_kernel,
        grid=(G, R // tm),
        in_specs=[
            _tok_spec(tm, D_MODEL),
            _tok_spec(tm, A_V_DIM),
            _tok_spec(tm, A_V_DIM, A_CONV_DIM // A_V_DIM),
            mods.spec(l, 1, 2, tm),
            pl.BlockSpec((None, 1, A_DV), lambda g, i: (li, 0, 0)),
            _resident((None, A_V_DIM, D_MODEL), lambda g, i: (0, 0, 0)),
        ],
        out_specs=_tok_spec(tm, D_MODEL),
        out_shape=jax.ShapeDtypeStruct(x.shape, f32),
        scratch_shapes=[pltpu.VMEM((tm, A_V_DIM), bf16)],
        compiler_params=_params(("arbitrary", "arbitrary")),
        name="a_out",
    )(x, o, proj, mods.arr, won, wo)


def _b_kernel(x_ref, sh_ref, sc_ref, gt_ref, nw_ref, wi_ref, vw_ref, vb_ref, ws_ref, bs_ref, wo_ref,
              *rest, decode):
    if decode:
        y_ref, vout_ref, vn_ref, gd_ref = rest
    else:
        y_ref, vn_ref, gd_ref = rest
    tm = x_ref.shape[0]
    x = x_ref[...]
    h = _norm_mod(x, nw_ref[...], sc_ref[...], sh_ref[...]).astype(bf16)
    hw = B_WIDTH // 2
    vd = [jnp.dot(h, wi_ref[:, B_WIDTH + k * hw:B_WIDTH + (k + 1) * hw], preferred_element_type=f32)
          for k in range(2)]
    udot = lambda g: jnp.dot(h, wi_ref[:, g * B_GROUP_DIM:(g + 1) * B_GROUP_DIM], preferred_element_type=f32)
    ahead = 2
    ud = [udot(g) for g in range(ahead)]
    v = jnp.concatenate([_gelu(vd[0]), _gelu(vd[1])], axis=1)
    mu = jnp.mean(v, axis=-1, keepdims=True)
    vc = v - mu
    vn = vc * lax.rsqrt(jnp.mean(vc * vc, axis=-1, keepdims=True) + EPS) * vw_ref[...] + vb_ref[...]
    if decode:
        vout_ref[...] = vn
    else:
        vn_ref[...] = vn.astype(bf16)
        ri = lax.broadcasted_iota(jnp.int32, (CHUNK_B, CHUNK_B), 0)
        ci = lax.broadcasted_iota(jnp.int32, (CHUNK_B, CHUNK_B), 1)
        causal = ri >= ci
    for g in range(B_GROUPS):
        gs = slice(g * B_GROUP_DIM, (g + 1) * B_GROUP_DIM)
        if g + ahead < B_GROUPS:
            ud.append(udot(g + ahead))
        u = _gelu(ud[g])
        if decode:
            gd_ref[:, gs] = (u * (ws_ref[:, gs] * vn[:, gs] + bs_ref[:, gs])).astype(bf16)
        else:
            wsg = jnp.where(causal, ws_ref[g], 0.0).astype(bf16)
            bias = bs_ref[:, g:g + 1]
            for c in range(tm // CHUNK_B):
                rs = slice(c * CHUNK_B, (c + 1) * CHUNK_B)
                mixed = jnp.dot(wsg, vn_ref[rs, gs], preferred_element_type=f32) + bias
                gd_ref[rs, gs] = (u[rs] * mixed).astype(bf16)
    y = jnp.dot(gd_ref[...], wo_ref[...], preferred_element_type=f32)
    y_ref[...] = x + (1.0 + gt_ref[...]) * y


def _b_mixer(x, mods, l, li, nw3, wi, vw, vb, ws, bs, wo, decode):
    G, R, _ = x.shape
    tm = min(R, 512)
    sub = 1
    if decode:
        ws_spec = pl.BlockSpec((None, 1, B_WIDTH), lambda g, i: (li, 0, 0))
        bs_spec = pl.BlockSpec((None, 1, B_WIDTH), lambda g, i: (li, 0, 0))
        out_specs = [_tok_spec(tm, D_MODEL), _tok_spec(tm, B_WIDTH)]
        out_shape = [jax.ShapeDtypeStruct(x.shape, f32), jax.ShapeDtypeStruct((G, R, B_WIDTH), f32)]
    else:
        ws_spec = pl.BlockSpec((None, B_GROUPS, CHUNK_B, CHUNK_B), lambda g, i: (li, 0, 0, 0))
        bs_spec = pl.BlockSpec((None, CHUNK_B, B_GROUPS), lambda g, i: (li, 0, 0))
        out_specs = _tok_spec(tm, D_MODEL)
        out_shape = jax.ShapeDtypeStruct(x.shape, f32)
    return pl.pallas_call(
        functools.partial(_b_kernel, decode=decode),
        grid=(G, R // tm),
        in_specs=[
            _tok_spec(tm, D_MODEL),
            mods.spec(l, sub, 0, tm), mods.spec(l, sub, 1, tm), mods.spec(l, sub, 2, tm),
            pl.BlockSpec((None, 1, D_MODEL), lambda g, i: (l * N_SUB + sub, 0, 0)),
            _resident((None, D_MODEL, 2 * B_WIDTH), lambda g, i: (0, 0, 0)),
            pl.BlockSpec((None, 1, B_WIDTH), lambda g, i: (li, 0, 0)),
            pl.BlockSpec((None, 1, B_WIDTH), lambda g, i: (li, 0, 0)),
            ws_spec, bs_spec,
            _resident((None, B_WIDTH, D_MODEL), lambda g, i: (0, 0, 0)),
        ],
        out_specs=out_specs,
        out_shape=out_shape,
        scratch_shapes=[pltpu.VMEM((tm, B_WIDTH), bf16), pltpu.VMEM((tm, B_WIDTH), bf16)],
        compiler_params=_params(("arbitrary", "arbitrary")),
        name="b_mixer",
    )(x, mods.arr, mods.arr, mods.arr, nw3, wi, vw, vb, ws, bs, wo)


def _trunk(xp, mp, xs, ms, conv_state, s_state, W):
    conv_p, s_p, conv_s, s_s, v_s = [], [], [], [], []
    wgu = W["ffn_gu"][0, 0][None].astype(bf16)
    wdn = W["ffn_dn"][0, 0][None].astype(bf16)
    for i in range(DEPTH):
        li = i // N_MIXERS
        is_a = i % N_MIXERS == 0
        mix = ("a_w_in", "a_w_out") if is_a else ("b_w_in", "b_w_out")
        jobs = [(W["ffn_gu"], (i, 1)), (W["ffn_dn"], (i, 1))] + [(W[n], (li,)) for n in mix]
        xp, xs, wgu, wdn, w_in, w_out = _ffn(xp, mp, xs, ms, i, 0, W["nw3"], wgu, wdn, jobs=jobs)
        if is_a:
            proj, gates = _a_in(xs, ms, i, li, W["nw3"], w_in, W["a_ws"], W["a_alog"], W["a_dtb"])
            o, s_new = _a_step(proj, conv_state, gates, W["a_wconv"], s_state, li,
                               s_s[-1] if s_s else None)
            s_s.append(s_new)
            pre = proj[0, :, None, :A_CONV_DIM]
            conv_s.append(jnp.concatenate([conv_state[li][:, 1:], pre], axis=1))
            xs = _a_out(xs, o, proj, ms, i, li, W["a_won"], w_out)
            xp, s_new, tail = _a_prompt(xp, mp, i, li, W["nw3"], w_in, W["a_ws"], W["a_alog"],
                                        W["a_dtb"], W["a_wconv"], W["a_won"], w_out)
            s_p.append(s_new)
            conv_p.append(tail[:, 7::8])
        else:
            xs, v = _b_mixer(xs, ms, i, li, W["nw3"], w_in, W["b_vw"], W["b_vb"],
                             W["b_ws00"], W["b_bs0"], w_out, decode=True)
            v_s.append(v)
            xp = _b_mixer(xp, mp, i, li, W["nw3"], w_in, W["b_vw"], W["b_vb"],
                          W["b_ws"], W["b_bsT"], w_out, decode=False)
        last = i == DEPTH - 1
        jobs = [] if last else [(W["ffn_gu"], (i + 1, 0)), (W["ffn_dn"], (i + 1, 0))]
        out = _ffn(xp, mp, xs, ms, i, 2, W["nw3"], wgu, wdn, jobs=jobs,
                   final_w=W["final_w"] if last else None)
        xp, xs = out[:2]
        if jobs:
            wgu, wdn = out[2:]
    return xp, xs, conv_p, s_p, conv_s, s_s, v_s


def kernel(x_prompt, x_sample, state_a_conv, state_a_S, c_prompt, c_sample, w_ada, b_ada, norm_w, ffn_w_gu, ffn_w_down, a_w_in, a_w_conv, a_log, a_dt_bias, a_w_onorm, a_w_out, b_w_in, b_vnorm_w, b_vnorm_b, b_w_s, b_b_s, b_w_out, final_norm_w):
    n_a = a_w_in.shape[0]
    n_b = b_w_in.shape[0]
    bp, seq, _ = x_prompt.shape
    bs = x_sample.shape[0]
    assert x_sample.shape[1] == 1 and seq % A_LT == 0 and seq % CHUNK_B == 0

    mod_p, mod_s = _ada(c_prompt, c_sample, w_ada, b_ada)
    mods_p = _Mods(mod_p.reshape(DEPTH, bp, 1, N_SUB * 3 * D_MODEL), per_row=False)
    mods_s = _Mods(mod_s.reshape(DEPTH, 1, bs, N_SUB * 3 * D_MODEL), per_row=True)

    pad = LANES - 2 * A_HEADS
    gate_pad = lambda a: jnp.pad(a, ((0, 0), (A_HEADS, pad)))[:, None, :]
    W = {
        "nw3": norm_w.reshape(DEPTH * N_SUB, 1, D_MODEL),
        "ffn_gu": ffn_w_gu,
        "ffn_dn": ffn_w_down,
        "a_w_in": a_w_in,
        "a_w_out": a_w_out,
        "b_w_in": b_w_in,
        "b_w_out": b_w_out,
        "a_ws": jnp.pad(a_w_in[:, :, A_MAIN_DIM:], ((0, 0), (0, 0), (0, pad))),
        "a_alog": gate_pad(a_log),
        "a_dtb": gate_pad(a_dt_bias),
        "a_wconv": a_w_conv,
        "a_won": a_w_onorm[:, None, :],
        "b_vw": b_vnorm_w[:, None, :],
        "b_vb": b_vnorm_b[:, None, :],
        "b_ws": b_w_s,
        "b_bsT": jnp.swapaxes(b_b_s, 1, 2),
        "b_ws00": jnp.repeat(b_w_s[:, :, 0, 0], B_GROUP_DIM, axis=1)[:, None, :],
        "b_bs0": jnp.repeat(b_b_s[:, :, 0], B_GROUP_DIM, axis=1)[:, None, :],
        "final_w": final_norm_w,
    }

    y_p, y_s, conv_p, s_p, conv_s, s_s, v_s = _trunk(
        x_prompt, mods_p, x_sample.reshape(1, bs, D_MODEL), mods_s, state_a_conv, state_a_S, W)
    return (y_p,
            y_s.reshape(bs, 1, D_MODEL),
            jnp.stack(conv_p),
            jnp.stack(s_p),
            jnp.stack(conv_s),
            s_s[-1],
            jnp.stack(v_s).reshape(n_b, bs, 1, B_WIDTH))
```

```python
import functools

import jax
import jax.numpy as jnp
import numpy as np
from jax import lax
from jax.experimental import pallas as pl
from jax.experimental.pallas import tpu as pltpu

f32 = jnp.float32
bf16 = jnp.bfloat16

D_MODEL = 1024
DEPTH = 4
N_MIXERS = 2
A_HEADS = 8
A_DK = 128
A_DV = 128
A_QK_DIM = A_HEADS * A_DK
A_V_DIM = A_HEADS * A_DV
A_CONV_DIM = 2 * A_QK_DIM + A_V_DIM
A_MAIN_DIM = A_CONV_DIM + A_V_DIM
CONV_W = 4
CHUNK_A = 64
B_WIDTH = 2 * D_MODEL
B_GROUPS = 8
B_GROUP_DIM = B_WIDTH // B_GROUPS
CHUNK_B = 128
D_FF = 2816
FFN_RES = 0.5
N_SUB = 3
EPS = 1e-6

LANES = 128
FF_TILE = 256
FFN_PIECES = 4
B_PIECES = 4
VMEM_LIMIT = 56 * 1024 * 1024


def _params(sem):
    return pltpu.CompilerParams(dimension_semantics=sem, vmem_limit_bytes=VMEM_LIMIT)


def _resident(shape, index_map):
    return pl.BlockSpec(shape, index_map, pipeline_mode=pl.Buffered(1))


def _bdot(a, b):
    return jnp.dot(a.astype(bf16), b.astype(bf16), preferred_element_type=f32)


def _split2(a):
    hi = a.astype(bf16)
    lo = (a - hi.astype(f32)).astype(bf16)
    return hi, lo


def _split3(a):
    hi = a.astype(bf16)
    r = a - hi.astype(f32)
    mid = r.astype(bf16)
    lo = (r - mid.astype(f32)).astype(bf16)
    return hi, mid, lo


def _dot3(a, b):
    ah, al = _split2(a)
    bh, bl = _split2(b)
    d = functools.partial(jnp.dot, preferred_element_type=f32)
    rows = a.shape[0]
    p = d(jnp.concatenate([ah, al], axis=0), bh)
    return p[:rows] + p[rows:] + d(ah, bl)


LOG2E = 1.4426950408889634


def _silu(x):
    return x / (1.0 + jnp.exp2(x * (-LOG2E)))


def _softplus(x):
    return jnp.maximum(x, 0.0) + jnp.log1p(jnp.exp(-jnp.abs(x)))


GELU_A = -LOG2E * 2.0 * (2.0 / 3.141592653589793) ** 0.5
GELU_B = GELU_A * 0.044715


def _gelu(x):
    return x / (1.0 + jnp.exp2(x * (GELU_A + GELU_B * (x * x))))


def _rms(x):
    return x * lax.rsqrt(jnp.mean(x * x, axis=-1, keepdims=True) + EPS)


def _norm_mod(x, nw, sc, sh):
    return (_rms(x) * nw) * (1.0 + sc) + sh


def _l2n(x):
    return x * lax.rsqrt(jnp.sum(x * x, axis=-1, keepdims=True) + EPS)


def _ada_kernel(c_ref, w_ref, b_ref, op_ref, os_ref):
    bp = op_ref.shape[0]
    mod = _bdot(_silu(c_ref[...]), w_ref[...].astype(bf16)) + b_ref[...]
    op_ref[...] = mod[:bp]
    os_ref[...] = mod[bp:]


def _ada(c_p, c_s, w_ada, b_ada):
    n = N_SUB * 3 * D_MODEL
    tn = 18 * LANES
    bp, bs = c_p.shape[0], c_s.shape[0]
    assert bp % 8 == 0
    return pl.pallas_call(
        _ada_kernel,
        grid=(DEPTH, n // tn),
        in_specs=[
            pl.BlockSpec((bp + bs, D_MODEL), lambda l, j: (0, 0)),
            pl.BlockSpec((None, D_MODEL, tn), lambda l, j: (l, 0, j)),
            pl.BlockSpec((None, 1, tn), lambda l, j: (l, 0, j)),
        ],
        out_specs=[
            pl.BlockSpec((None, bp, tn), lambda l, j: (l, 0, j)),
            pl.BlockSpec((None, bs, tn), lambda l, j: (l, 0, j)),
        ],
        out_shape=[jax.ShapeDtypeStruct((DEPTH, bp, n), f32),
                   jax.ShapeDtypeStruct((DEPTH, bs, n), f32)],
        compiler_params=_params(("arbitrary", "arbitrary")),
        name="ada",
    )(jnp.concatenate([c_p, c_s], axis=0), w_ada, b_ada.reshape(DEPTH, 1, n))


class _Mods:
    def __init__(self, arr, per_row):
        self.arr = arr
        self.per_row = per_row

    def spec(self, l, sub, kind, tm):
        col = sub * 3 + kind
        if self.per_row:
            return pl.BlockSpec((None, None, tm, D_MODEL), lambda g, i: (l, g, i, col))
        return pl.BlockSpec((None, None, 1, D_MODEL), lambda g, i: (l, g, 0, col))


def _tok_spec(tm, width, colblk=0):
    return pl.BlockSpec((None, tm, width), lambda g, i: (g, i, colblk))


def _ffn_kernel(xp_ref, shp_ref, scp_ref, gtp_ref, xs_ref, shs_ref, scs_ref, gts_ref,
                nw_ref, wg_ref, wu_ref, wd_ref, *rest, final, n_prompt, n_jobs):
    rest = list(rest)
    fw_ref = rest.pop(0) if final else None
    src_refs = [rest.pop(0) for _ in range(n_jobs)]
    op_ref, os_ref = rest.pop(0), rest.pop(0)
    dst_refs = [rest.pop(0) for _ in range(n_jobs)]
    act_ref, h_ref = rest

    def body(x_ref, sh_ref, sc_ref, gt_ref, o_ref, cast):
        if cast:
            for src, dst in zip(src_refs, dst_refs):
                dst[...] = src[...].astype(bf16)
        rows = x_ref.shape[0]
        per_row = sc_ref.shape[0] == rows
        mod = lambda ref, rs: ref[rs, :] if per_row else ref[...]
        pieces = FFN_PIECES if rows % (FFN_PIECES * LANES) == 0 else 1
        pr = rows // pieces
        sl0 = slice(0, FF_TILE)
        for p in range(pieces):
            rs = slice(p * pr, (p + 1) * pr)
            hq = _norm_mod(x_ref[rs, :], nw_ref[...], mod(sc_ref, rs), mod(sh_ref, rs)).astype(bf16)
            h_ref[rs, :] = hq
            g = jnp.dot(hq, wg_ref[:, sl0], preferred_element_type=f32)
            u = jnp.dot(hq, wu_ref[:, sl0], preferred_element_type=f32)
            act_ref[rs, sl0] = (_silu(g) * u).astype(bf16)
        h = h_ref[0:rows, :]
        for c in range(1, D_FF // FF_TILE):
            sl = slice(c * FF_TILE, (c + 1) * FF_TILE)
            g = jnp.dot(h, wg_ref[:, sl], preferred_element_type=f32)
            u = jnp.dot(h, wu_ref[:, sl], preferred_element_type=f32)
            act_ref[0:rows, sl] = (_silu(g) * u).astype(bf16)
        halves = 1
        hr = rows // halves
        ys = [jnp.dot(act_ref[k * hr:(k + 1) * hr, :], wd_ref[...], preferred_element_type=f32)
              for k in range(halves)]
        for k in range(halves):
            rs = slice(k * hr, (k + 1) * hr)
            out = x_ref[rs, :] + FFN_RES * (1.0 + mod(gt_ref, rs)) * ys[k]
            if final:
                out = _rms(out) * fw_ref[...]
            o_ref[rs, :] = out

    i = pl.program_id(0)
    pl.when(i < n_prompt)(lambda: body(xp_ref, shp_ref, scp_ref, gtp_ref, op_ref, True))
    pl.when(i == n_prompt)(lambda: body(xs_ref, shs_ref, scs_ref, gts_ref, os_ref, False))


def _ffn(xp, mods_p, xs, mods_s, l, sub, nw3, wgu, wdn, jobs=(), final_w=None):
    B, L, _ = xp.shape
    bs = xs.shape[1]
    tm = min(L, 1024)
    tps = L // tm
    n_prompt = B * tps
    final = final_w is not None
    tile = lambda i: jnp.minimum(i, n_prompt - 1)
    pmod = lambda kind: pl.BlockSpec((None, None, 1, D_MODEL),
                                     lambda i: (l, tile(i) // tps, 0, sub * 3 + kind))
    smod = lambda kind: pl.BlockSpec((None, None, bs, D_MODEL), lambda i: (l, 0, 0, sub * 3 + kind))
    in_specs = [
        pl.BlockSpec((None, tm, D_MODEL), lambda i: (tile(i) // tps, tile(i) % tps, 0)),
        pmod(0), pmod(1), pmod(2),
        pl.BlockSpec((None, bs, D_MODEL), lambda i: (0, 0, 0)),
        smod(0), smod(1), smod(2),
        pl.BlockSpec((None, 1, D_MODEL), lambda i: (l * N_SUB + sub, 0, 0)),
        _resident((None, D_MODEL, D_FF), lambda i: (0, 0, 0)),
        _resident((None, D_MODEL, D_FF), lambda i: (0, 0, 1)),
        _resident((None, D_FF, D_MODEL), lambda i: (0, 0, 0)),
    ]
    args = [xp, mods_p.arr, mods_p.arr, mods_p.arr, xs, mods_s.arr, mods_s.arr, mods_s.arr,
            nw3, wgu, wgu, wdn]
    if final:
        in_specs.append(pl.BlockSpec((1, D_MODEL), lambda i: (0, 0)))
        args.append(final_w.reshape(1, D_MODEL))
    out_specs = [
        pl.BlockSpec((None, tm, D_MODEL), lambda i: (tile(i) // tps, tile(i) % tps, 0)),
        pl.BlockSpec((None, bs, D_MODEL), lambda i: (0, 0, 0)),
    ]
    out_shape = [jax.ShapeDtypeStruct(xp.shape, f32), jax.ShapeDtypeStruct(xs.shape, f32)]
    for arr, idx in jobs:
        lead = arr.shape[:-2]
        flat = arr.reshape((-1,) + arr.shape[-2:])
        k = int(np.ravel_multi_index(idx, lead))
        rows, cols = flat.shape[1:]
        assert rows % (n_prompt * 16) == 0
        chunk = rows // n_prompt
        in_specs.append(pl.BlockSpec((None, chunk, cols), lambda i, k=k: (k, tile(i), 0)))
        args.append(flat)
        out_specs.append(pl.BlockSpec((None, chunk, cols), lambda i: (0, tile(i), 0)))
        out_shape.append(jax.ShapeDtypeStruct((1, rows, cols), bf16))
    return pl.pallas_call(
        functools.partial(_ffn_kernel, final=final, n_prompt=n_prompt, n_jobs=len(jobs)),
        grid=(n_prompt + 1,),
        in_specs=in_specs,
        out_specs=out_specs,
        out_shape=out_shape,
        scratch_shapes=[pltpu.VMEM((tm, D_FF), bf16), pltpu.VMEM((tm, D_MODEL), bf16)],
        compiler_params=_params(("arbitrary",)),
        name="ffn",
    )(*args)


def _gate_values(hf, ws_ref, alog_ref, dtb_ref):
    p = _dot3(hf, ws_ref[...])
    lane = lax.broadcasted_iota(jnp.int32, p.shape, 1)
    beta = jax.nn.sigmoid(p)
    g = -jnp.exp(alog_ref[...]) * _softplus(p + dtb_ref[...])
    is_g = (lane >= A_HEADS) & (lane < 2 * A_HEADS)
    return jnp.where(lane < A_HEADS, beta, jnp.where(is_g, g, 0.0))


def _a_in_kernel(x_ref, sh_ref, sc_ref, nw_ref, wm_ref, ws_ref, alog_ref, dtb_ref, proj_ref, gates_ref):
    hf = _norm_mod(x_ref[...], nw_ref[...], sc_ref[...], sh_ref[...])
    h = hf.astype(bf16)
    for c in range(A_MAIN_DIM // D_MODEL):
        sl = slice(c * D_MODEL, (c + 1) * D_MODEL)
        proj_ref[:, sl] = jnp.dot(h, wm_ref[:, sl], preferred_element_type=f32)
    gates_ref[...] = _gate_values(hf, ws_ref, alog_ref, dtb_ref)


def _a_in(x, mods, l, li, nw3, wm, ws, alog, dtb):
    G, R, _ = x.shape
    tm = min(R, 512)
    sub = 1
    return pl.pallas_call(
        _a_in_kernel,
        grid=(G, R // tm),
        in_specs=[
            _tok_spec(tm, D_MODEL),
            mods.spec(l, sub, 0, tm), mods.spec(l, sub, 1, tm),
            pl.BlockSpec((None, 1, D_MODEL), lambda g, i: (l * N_SUB + sub, 0, 0)),
            _resident((None, D_MODEL, A_MAIN_DIM), lambda g, i: (0, 0, 0)),
            _resident((None, D_MODEL, LANES), lambda g, i: (li, 0, 0)),
            pl.BlockSpec((None, 1, LANES), lambda g, i: (li, 0, 0)),
            pl.BlockSpec((None, 1, LANES), lambda g, i: (li, 0, 0)),
        ],
        out_specs=[_tok_spec(tm, A_MAIN_DIM), _tok_spec(tm, LANES)],
        out_shape=[jax.ShapeDtypeStruct((G, R, A_MAIN_DIM), f32),
                   jax.ShapeDtypeStruct((G, R, LANES), f32)],
        compiler_params=_params(("arbitrary", "arbitrary")),
        name="a_in",
    )(x, mods.arr, mods.arr, nw3, wm, ws, alog, dtb)


A_LT = 256
A_GROUP = 2


def _a_prompt_kernel(x_ref, sh_ref, sc_ref, gt_ref, nw_ref, wm_ref, ws_ref, alog_ref, dtb_ref, wc_ref,
                     won_ref, wo_ref, perm_ref, permt_ref, ptri_ref,
                     y_ref, s_ref, tail_ref, hist_ref, act_ref, o_scr, og_ref):
    t = pl.program_id(1)
    lt = x_ref.shape[0]
    nc = lt // CHUNK_A

    @pl.when(t == 0)
    def _():
        s_ref[...] = jnp.zeros(s_ref.shape, f32)
        hist_ref[...] = jnp.zeros(hist_ref.shape, f32)

    x = x_ref[...]
    hf = _norm_mod(x, nw_ref[...], sc_ref[...], sh_ref[...])
    h = hf.astype(bf16)

    ri = lax.broadcasted_iota(jnp.int32, (CHUNK_A, CHUNK_A), 0)
    ci = lax.broadcasted_iota(jnp.int32, (CHUNK_A, CHUNK_A), 1)
    pos_r = (ri % 8) * (CHUNK_A // 8) + ri // 8
    pos_c = (ci % 8) * (CHUNK_A // 8) + ci // 8
    causal = pos_r >= pos_c
    strict = pos_r > pos_c
    hp = jnp.dot(perm_ref[...], h, preferred_element_type=f32).astype(bf16)

    gates = _gate_values(hf, ws_ref, alog_ref, dtb_ref)
    lane = lax.broadcasted_iota(jnp.int32, gates.shape, 1)
    is_g = (lane >= A_HEADS) & (lane < 2 * A_HEADS)
    gparts = _split3(gates)
    sparts = _split3(pltpu.roll(jnp.where(is_g, gates, 0.0), A_HEADS, 1))
    ptri = ptri_ref[...]
    gcols, grows = [], []
    for c in range(nc):
        rs = slice(c * CHUNK_A, (c + 1) * CHUNK_A)
        gc = None
        for gp, sp in zip(gparts, sparts):
            term = jnp.dot(ptri, jnp.concatenate([gp[rs], sp[rs]], axis=0), preferred_element_type=f32)
            gc = term if gc is None else gc + term
        gcols.append(gc)
        grows.append(gc.T)

    ng = CHUNK_A // 8
    sub = lax.broadcasted_iota(jnp.int32, (8, 2 * A_DK), 0)
    pair = 2 * A_DK
    for p in range(A_HEADS // 2):
        for part in range(3):
            col = part * A_QK_DIM + p * pair
            cs = slice(col, col + pair)
            pre = jnp.dot(hp, wm_ref[:, cs], preferred_element_type=f32)
            grp = lambda c, i: pre[c * CHUNK_A + 8 * i:c * CHUNK_A + 8 * i + 8]
            acc = pre * wc_ref[CONV_W - 1:CONV_W, cs]
            shifted = [[] for _ in range(CONV_W)]
            for c in range(nc):
                w = []
                for j in range(CONV_W - 1):
                    i = ng - (CONV_W - 1) + j
                    prev = hist_ref[8 * j:8 * j + 8, cs] if c == 0 else grp(c - 1, i)
                    w.append(pltpu.roll(jnp.where(sub == 7, prev, grp(c, i)), 1, 0))
                for k_ in range(1, CONV_W):
                    shifted[k_] += w[CONV_W - 1 - k_:] + [pre[c * CHUNK_A:(c + 1) * CHUNK_A - 8 * k_]]
            hist_ref[:, cs] = pre[lt - 8 * (CONV_W - 1):]
            for k_ in range(1, CONV_W):
                acc = acc + jnp.concatenate(shifted[k_], axis=0) * wc_ref[CONV_W - 1 - k_:CONV_W - k_, cs]
            a2 = _silu(acc)
            for hh in range(2):
                a = a2[:, hh * A_DK:(hh + 1) * A_DK]
                if part == 0:
                    a = a * (lax.rsqrt(jnp.sum(a * a, axis=-1, keepdims=True) + EPS) * (A_DK ** -0.5))
                elif part == 1:
                    a = _l2n(a)
                act_ref[part * A_HEADS + 2 * p + hh] = a
    tail_ref[...] = hist_ref[...]

    z = jnp.dot(hp, wm_ref[:, A_CONV_DIM:A_MAIN_DIM], preferred_element_type=f32)
    won = won_ref[...]
    zg = [_silu(z[:, h_ * A_DV:(h_ + 1) * A_DV]) * won for h_ in range(A_HEADS)]

    states = [s_ref[h_] for h_ in range(A_HEADS)]

    def recurrence_pieces(c, insts):
        rs = slice(c * CHUNK_A, (c + 1) * CHUNK_A)
        tmp = {}

        def first():
            tmp["ws_qs"] = [jnp.dot(i["lhs1"], states[h_].astype(bf16), preferred_element_type=f32)
                            for h_, i in enumerate(insts)]

        def second():
            tmp["r"] = [jnp.dot(i["lhs2"], (i["u"] - w[:CHUNK_A]).astype(bf16), preferred_element_type=f32)
                        for i, w in zip(insts, tmp["ws_qs"])]

        def third():
            for h_, (i, w, r) in enumerate(zip(insts, tmp["ws_qs"], tmp["r"])):
                o_scr[rs, h_ * A_DV:(h_ + 1) * A_DV] = w[CHUNK_A:] + r[:CHUNK_A]
                states[h_] = states[h_] * i["edl"] + r[CHUNK_A:]

        return [first, second, third]

    pending = []
    for c0 in range(0, nc, A_GROUP):
        chunks = range(c0, min(c0 + A_GROUP, nc))

        inst = []
        for c in chunks:
            rs = slice(c * CHUNK_A, (c + 1) * CHUNK_A)
            for h_ in range(A_HEADS):
                qc = act_ref[h_, rs, :]
                kc = act_ref[A_HEADS + h_, rs, :]
                vc = act_ref[2 * A_HEADS + h_, rs, :]
                bcol = gcols[c][:, h_:h_ + 1]
                dcol = gcols[c][:, 2 * A_HEADS + h_:2 * A_HEADS + h_ + 1]
                drow = grows[c][2 * A_HEADS + h_:2 * A_HEADS + h_ + 1, :]
                dlast = dcol[CHUNK_A - 1:CHUNK_A, :]
                gam = jnp.where(causal, jnp.exp(jnp.where(causal, dcol - drow, 0.0)), 0.0)
                kb = kc * bcol
                ed = jnp.exp(dcol)
                kt = kc.T
                kq = _bdot(jnp.concatenate([kb, qc], axis=0), kt)
                m = jnp.where(strict, kq[:CHUNK_A] * gam, 0.0)
                attn = kq[CHUNK_A:] * gam
                kdec_t = kt * jnp.exp(dlast - drow)
                inst.append(dict(
                    m=m,
                    rhs=jnp.concatenate([vc * bcol, kb * ed], axis=1),
                    qed=(qc * ed).astype(bf16),
                    lhs2=jnp.concatenate([attn, kdec_t], axis=0).astype(bf16),
                    edl=jnp.exp(dlast)))

        ts = [-i["m"] for i in inst]
        ms = [_bdot(i["m"], i["m"]) for i in inst]
        n = 2
        while n < CHUNK_A:
            if 2 * n < CHUNK_A:
                prod = [_bdot(jnp.concatenate([t, m], axis=0), m) for t, m in zip(ts, ms)]
                ts = [t + m + p[:CHUNK_A] for t, m, p in zip(ts, ms, prod)]
                ms = [p[CHUNK_A:] for p in prod]
            else:
                ts = [t + m + _bdot(t, m) for t, m in zip(ts, ms)]
            n *= 2
            if pending:
                pending.pop(0)()
        rs_ = [-(i["m"] + t + _dot3(i["m"], t)) for i, t in zip(inst, ts)]
        if pending:
            pending.pop(0)()
        ts = [t + r + _bdot(t, r) for t, r in zip(ts, rs_)]
        while pending:
            pending.pop(0)()

        for i, t in zip(inst, ts):
            uw = i["rhs"] + _bdot(t, i["rhs"])
            i["u"] = uw[:, :A_DV]
            i["lhs1"] = jnp.concatenate([uw[:, A_DV:].astype(bf16), i["qed"]], axis=0)
        for k_, c in enumerate(chunks):
            pending += recurrence_pieces(c, inst[k_ * A_HEADS:(k_ + 1) * A_HEADS])
    while pending:
        pending.pop(0)()
    for h_ in range(A_HEADS):
        s_ref[h_] = states[h_]

    for h_ in range(A_HEADS):
        sl = slice(h_ * A_DV, (h_ + 1) * A_DV)
        og_ref[:, sl] = (_rms(o_scr[:, sl]) * zg[h_]).astype(bf16)
    og = jnp.dot(permt_ref[...], og_ref[...], preferred_element_type=f32).astype(bf16)
    y = jnp.dot(og, wo_ref[...], preferred_element_type=f32)
    y_ref[...] = x + (1.0 + gt_ref[...]) * y


def _a_prompt(x, mods, l, li, nw3, wm, ws, alog, dtb, wconv, won, wo):
    B, L, _ = x.shape
    lt = A_LT
    sub = 1
    const = lambda shape: pl.BlockSpec((None,) + shape, lambda b, t: (li,) + (0,) * len(shape))
    r = np.arange(CHUNK_A)
    pc = (np.arange(CHUNK_A)[None, :] == ((r % 8) * (CHUNK_A // 8) + r // 8)[:, None]).astype(np.float32)
    perm_np = np.kron(np.eye(lt // CHUNK_A, dtype=np.float32), pc)
    perm = jnp.asarray(perm_np, dtype=bf16)
    permt = jnp.asarray(perm_np.T, dtype=bf16)
    ptri = jnp.asarray(np.concatenate([pc, pc @ np.tril(np.ones((CHUNK_A, CHUNK_A), np.float32))], axis=1),
                       dtype=bf16)
    hist_rows = 8 * (CONV_W - 1)
    return pl.pallas_call(
        _a_prompt_kernel,
        grid=(B, L // lt),
        in_specs=[
            _tok_spec(lt, D_MODEL),
            mods.spec(l, sub, 0, lt), mods.spec(l, sub, 1, lt), mods.spec(l, sub, 2, lt),
            pl.BlockSpec((None, 1, D_MODEL), lambda b, t: (l * N_SUB + sub, 0, 0)),
            _resident((None, D_MODEL, A_MAIN_DIM), lambda b, t: (0, 0, 0)),
            _resident((None, D_MODEL, LANES), lambda b, t: (li, 0, 0)),
            const((1, LANES)), const((1, LANES)),
            const((CONV_W, A_CONV_DIM)),
            const((1, A_DV)),
            _resident((None, A_V_DIM, D_MODEL), lambda b, t: (0, 0, 0)),
            _resident((lt, lt), lambda b, t: (0, 0)),
            _resident((lt, lt), lambda b, t: (0, 0)),
            _resident((CHUNK_A, 2 * CHUNK_A), lambda b, t: (0, 0)),
        ],
        out_specs=[
            _tok_spec(lt, D_MODEL),
            pl.BlockSpec((None, A_HEADS, A_DK, A_DV), lambda b, t: (b, 0, 0, 0)),
            pl.BlockSpec((None, hist_rows, A_CONV_DIM), lambda b, t: (b, 0, 0)),
        ],
        out_shape=[jax.ShapeDtypeStruct(x.shape, f32),
                   jax.ShapeDtypeStruct((B, A_HEADS, A_DK, A_DV), f32),
                   jax.ShapeDtypeStruct((B, hist_rows, A_CONV_DIM), f32)],
        scratch_shapes=[pltpu.VMEM((hist_rows, A_CONV_DIM), f32),
                        pltpu.VMEM((3 * A_HEADS, lt, A_DK), f32),
                        pltpu.VMEM((lt, A_V_DIM), f32),
                        pltpu.VMEM((lt, A_V_DIM), bf16)],
        compiler_params=_params(("arbitrary", "arbitrary")),
        name="a_prompt",
    )(x, mods.arr, mods.arr, mods.arr, nw3, wm, ws, alog, dtb, wconv, won, wo, perm, permt, ptri)


A_BT = 8


def _a_step_kernel(pre_ref, conv_ref, gates_ref, wc_ref, s_ref, *rest, li):
    if li:
        prev_ref, o_ref, so_all_ref = rest
        so_all_ref[0:li] = prev_ref[...]
    else:
        o_ref, so_all_ref = rest
    so_ref = so_all_ref.at[li]
    bt = pre_ref.shape[0]
    wc = wc_ref[...]
    acc = pre_ref[...] * wc[CONV_W - 1:CONV_W, :]
    for j in range(CONV_W - 1):
        acc = acc + conv_ref[:, j, :] * wc[j:j + 1, :]
    qkv = _silu(acc)
    gates = gates_ref[...]
    row = lax.broadcasted_iota(jnp.int32, (bt, 8, LANES), 1)
    for h in range(A_HEADS):
        q = _l2n(qkv[:, h * A_DK:(h + 1) * A_DK]) * (A_DK ** -0.5)
        k = _l2n(qkv[:, A_QK_DIM + h * A_DK:A_QK_DIM + (h + 1) * A_DK])
        v = qkv[:, 2 * A_QK_DIM + h * A_DV:2 * A_QK_DIM + (h + 1) * A_DV]
        beta = gates[:, h:h + 1]
        eg = jnp.exp(gates[:, A_HEADS + h:A_HEADS + h + 1])
        s = s_ref[:, h]
        kq = jnp.where(row == 0, k[:, None, :], jnp.where(row == 1, q[:, None, :], 0.0))
        ks = jnp.einsum('bmk,bkn->bmn', kq.astype(bf16), s.astype(bf16),
                        preferred_element_type=f32)
        v_new = beta * (v - eg * ks[:, 0, :])
        qk = jnp.sum(q * k, axis=-1, keepdims=True)
        o_ref[:, h * A_DV:(h + 1) * A_DV] = eg * ks[:, 1, :] + qk * v_new
        kh = k.astype(bf16).astype(f32)
        vh = v_new.astype(bf16).astype(f32)
        kl, vl = k - kh, v_new - vh
        lhs = jnp.where(row == 2, kl[:, None, :], jnp.where(row < 2, kh[:, None, :], 0.0))
        rhs = jnp.where(row == 1, vl[:, None, :], jnp.where(row < 3, vh[:, None, :], 0.0))
        outer = lax.dot_general(lhs.astype(bf16), rhs.astype(bf16), (((1,), (1,)), ((0,), (0,))),
                                preferred_element_type=f32)
        so_ref[:, h] = s * eg[:, :, None] + outer


def _a_step(proj, conv_state, gates, wconv, s0, li, s_prev):
    nb = proj.shape[1]
    bt = A_BT
    in_specs = [
        pl.BlockSpec((None, bt, A_CONV_DIM), lambda i: (0, i, 0)),
        pl.BlockSpec((None, bt, CONV_W - 1, A_CONV_DIM), lambda i: (li, i, 0, 0)),
        pl.BlockSpec((None, bt, LANES), lambda i: (0, i, 0)),
        pl.BlockSpec((None, CONV_W, A_CONV_DIM), lambda i: (li, 0, 0)),
        pl.BlockSpec((None, bt, A_HEADS, A_DK, A_DV), lambda i: (li, i, 0, 0, 0)),
    ]
    args = [proj, conv_state, gates, wconv, s0]
    if li:
        in_specs.append(pl.BlockSpec((li, bt, A_HEADS, A_DK, A_DV), lambda i: (0, i, 0, 0, 0)))
        args.append(s_prev)
    return pl.pallas_call(
        functools.partial(_a_step_kernel, li=li),
        grid=(nb // bt,),
        in_specs=in_specs,
        out_specs=[
            pl.BlockSpec((None, bt, A_V_DIM), lambda i: (0, i, 0)),
            pl.BlockSpec((li + 1, bt, A_HEADS, A_DK, A_DV), lambda i: (0, i, 0, 0, 0)),
        ],
        out_shape=[jax.ShapeDtypeStruct((1, nb, A_V_DIM), f32),
                   jax.ShapeDtypeStruct((li + 1,) + s0.shape[1:], f32)],
        compiler_params=_params(("arbitrary",)),
        name="a_step",
    )(*args)


def _a_out_kernel(x_ref, o_ref, z_ref, gt_ref, won_ref, wo_ref, y_ref, og_ref):
    won = won_ref[...]
    for h in range(A_HEADS):
        sl = slice(h * A_DV, (h + 1) * A_DV)
        og_ref[:, sl] = (_rms(o_ref[:, sl]) * won * _silu(z_ref[:, sl])).astype(bf16)
    y = jnp.dot(og_ref[...], wo_ref[...], preferred_element_type=f32)
    y_ref[...] = x_ref[...] + (1.0 + gt_ref[...]) * y


def _a_out(x, o, proj, mods, l, li, won, wo):
    G, R, _ = x.shape
    tm = min(R, 512)
    return pl.pallas_call(
        _a_out_kernel,
        grid=(G, R // tm),
        in_specs=[
            _tok_spec(tm, D_MODEL),
            _tok_spec(tm, A_V_DIM),
            _tok_spec(tm, A_V_DIM, A_CONV_DIM // A_V_DIM),
            mods.spec(l, 1, 2, tm),
            pl.BlockSpec((None, 1, A_DV), lambda g, i: (li, 0, 0)),
            _resident((None, A_V_DIM, D_MODEL), lambda g, i: (0, 0, 0)),
        ],
        out_specs=_tok_spec(tm, D_MODEL),
        out_shape=jax.ShapeDtypeStruct(x.shape, f32),
        scratch_shapes=[pltpu.VMEM((tm, A_V_DIM), bf16)],
        compiler_params=_params(("arbitrary", "arbitrary")),
        name="a_out",
    )(x, o, proj, mods.arr, won, wo)


def _b_kernel(x_ref, sh_ref, sc_ref, gt_ref, nw_ref, wi_ref, vw_ref, vb_ref, ws_ref, bs_ref, wo_ref,
              *rest, decode):
    if decode:
        y_ref, vout_ref, vn_ref, gd_ref = rest
    else:
        y_ref, vn_ref, gd_ref = rest
    tm = x_ref.shape[0]
    per_row = sc_ref.shape[0] == tm
    mod = lambda ref, rs: ref[rs, :] if per_row else ref[...]
    pieces = B_PIECES if tm % (B_PIECES * LANES) == 0 else 1
    pr = tm // pieces
    hs, v0 = [], []
    for p in range(pieces):
        rs = slice(p * pr, (p + 1) * pr)
        hs.append(_norm_mod(x_ref[rs, :], nw_ref[...], mod(sc_ref, rs), mod(sh_ref, rs)).astype(bf16))
        v0.append(jnp.dot(hs[-1], wi_ref[:, B_WIDTH:B_WIDTH + B_GROUP_DIM], preferred_element_type=f32))
    h = jnp.concatenate(hs, axis=0)
    x = x_ref[...]
    cut = B_WIDTH + B_WIDTH // 2
    vd = [jnp.dot(h, wi_ref[:, B_WIDTH + B_GROUP_DIM:cut], preferred_element_type=f32),
          jnp.dot(h, wi_ref[:, cut:], preferred_element_type=f32)]
    udot = lambda g: jnp.dot(h, wi_ref[:, g * B_GROUP_DIM:(g + 1) * B_GROUP_DIM], preferred_element_type=f32)
    ahead = 2
    ud = [udot(g) for g in range(ahead)]
    v = jnp.concatenate([_gelu(jnp.concatenate(v0, axis=0)), _gelu(vd[0]), _gelu(vd[1])], axis=1)
    mu = jnp.mean(v, axis=-1, keepdims=True)
    vc = v - mu
    vn = vc * lax.rsqrt(jnp.mean(vc * vc, axis=-1, keepdims=True) + EPS) * vw_ref[...] + vb_ref[...]
    if decode:
        vout_ref[...] = vn
    else:
        vn_ref[...] = vn.astype(bf16)
        ri = lax.broadcasted_iota(jnp.int32, (CHUNK_B, CHUNK_B), 0)
        ci = lax.broadcasted_iota(jnp.int32, (CHUNK_B, CHUNK_B), 1)
        causal = ri >= ci
    for g in range(B_GROUPS):
        gs = slice(g * B_GROUP_DIM, (g + 1) * B_GROUP_DIM)
        if g + ahead < B_GROUPS:
            ud.append(udot(g + ahead))
        u = _gelu(ud[g])
        if decode:
            gd_ref[:, gs] = (u * (ws_ref[:, gs] * vn[:, gs] + bs_ref[:, gs])).astype(bf16)
        else:
            wsg = jnp.where(causal, ws_ref[g], 0.0).astype(bf16)
            bias = bs_ref[:, g:g + 1]
            for c in range(tm // CHUNK_B):
                rs = slice(c * CHUNK_B, (c + 1) * CHUNK_B)
                mixed = jnp.dot(wsg, vn_ref[rs, gs], preferred_element_type=f32) + bias
                gd_ref[rs, gs] = (u[rs] * mixed).astype(bf16)
    y = jnp.dot(gd_ref[...], wo_ref[...], preferred_element_type=f32)
    y_ref[...] = x + (1.0 + gt_ref[...]) * y


def _b_mixer(x, mods, l, li, nw3, wi, vw, vb, ws, bs, wo, decode):
    G, R, _ = x.shape
    tm = min(R, 512)
    sub = 1
    if decode:
        ws_spec = pl.BlockSpec((None, 1, B_WIDTH), lambda g, i: (li, 0, 0))
        bs_spec = pl.BlockSpec((None, 1, B_WIDTH), lambda g, i: (li, 0, 0))
        out_specs = [_tok_spec(tm, D_MODEL), _tok_spec(tm, B_WIDTH)]
        out_shape = [jax.ShapeDtypeStruct(x.shape, f32), jax.ShapeDtypeStruct((G, R, B_WIDTH), f32)]
    else:
        ws_spec = pl.BlockSpec((None, B_GROUPS, CHUNK_B, CHUNK_B), lambda g, i: (li, 0, 0, 0))
        bs_spec = pl.BlockSpec((None, CHUNK_B, B_GROUPS), lambda g, i: (li, 0, 0))
        out_specs = _tok_spec(tm, D_MODEL)
        out_shape = jax.ShapeDtypeStruct(x.shape, f32)
    return pl.pallas_call(
        functools.partial(_b_kernel, decode=decode),
        grid=(G, R // tm),
        in_specs=[
            _tok_spec(tm, D_MODEL),
            mods.spec(l, sub, 0, tm), mods.spec(l, sub, 1, tm), mods.spec(l, sub, 2, tm),
            pl.BlockSpec((None, 1, D_MODEL), lambda g, i: (l * N_SUB + sub, 0, 0)),
            _resident((None, D_MODEL, 2 * B_WIDTH), lambda g, i: (0, 0, 0)),
            pl.BlockSpec((None, 1, B_WIDTH), lambda g, i: (li, 0, 0)),
            pl.BlockSpec((None, 1, B_WIDTH), lambda g, i: (li, 0, 0)),
            ws_spec, bs_spec,
            _resident((None, B_WIDTH, D_MODEL), lambda g, i: (0, 0, 0)),
        ],
        out_specs=out_specs,
        out_shape=out_shape,
        scratch_shapes=[pltpu.VMEM((tm, B_WIDTH), bf16), pltpu.VMEM((tm, B_WIDTH), bf16)],
        compiler_params=_params(("arbitrary", "arbitrary")),
        name="b_mixer",
    )(x, mods.arr, mods.arr, mods.arr, nw3, wi, vw, vb, ws, bs, wo)


def _trunk(xp, mp, xs, ms, conv_state, s_state, W):
    conv_p, s_p, conv_s, s_s, v_s = [], [], [], [], []
    wgu = W["ffn_gu"][0, 0][None].astype(bf16)
    wdn = W["ffn_dn"][0, 0][None].astype(bf16)
    for i in range(DEPTH):
        li = i // N_MIXERS
        is_a = i % N_MIXERS == 0
        mix = ("a_w_in", "a_w_out") if is_a else ("b_w_in", "b_w_out")
        jobs = [(W["ffn_gu"], (i, 1)), (W["ffn_dn"], (i, 1))] + [(W[n], (li,)) for n in mix]
        xp, xs, wgu, wdn, w_in, w_out = _ffn(xp, mp, xs, ms, i, 0, W["nw3"], wgu, wdn, jobs=jobs)
        if is_a:
            proj, gates = _a_in(xs, ms, i, li, W["nw3"], w_in, W["a_ws"], W["a_alog"], W["a_dtb"])
            o, s_new = _a_step(proj, conv_state, gates, W["a_wconv"], s_state, li,
                               s_s[-1] if s_s else None)
            s_s.append(s_new)
            pre = proj[0, :, None, :A_CONV_DIM]
            conv_s.append(jnp.concatenate([conv_state[li][:, 1:], pre], axis=1))
            xs = _a_out(xs, o, proj, ms, i, li, W["a_won"], w_out)
            xp, s_new, tail = _a_prompt(xp, mp, i, li, W["nw3"], w_in, W["a_ws"], W["a_alog"],
                                        W["a_dtb"], W["a_wconv"], W["a_won"], w_out)
            s_p.append(s_new)
            conv_p.append(tail[:, 7::8])
        else:
            xs, v = _b_mixer(xs, ms, i, li, W["nw3"], w_in, W["b_vw"], W["b_vb"],
                             W["b_ws00"], W["b_bs0"], w_out, decode=True)
            v_s.append(v)
            xp = _b_mixer(xp, mp, i, li, W["nw3"], w_in, W["b_vw"], W["b_vb"],
                          W["b_ws"], W["b_bsT"], w_out, decode=False)
        last = i == DEPTH - 1
        jobs = [] if last else [(W["ffn_gu"], (i + 1, 0)), (W["ffn_dn"], (i + 1, 0))]
        out = _ffn(xp, mp, xs, ms, i, 2, W["nw3"], wgu, wdn, jobs=jobs,
                   final_w=W["final_w"] if last else None)
        xp, xs = out[:2]
        if jobs:
            wgu, wdn = out[2:]
    return xp, xs, conv_p, s_p, conv_s, s_s, v_s


def kernel(x_prompt, x_sample, state_a_conv, state_a_S, c_prompt, c_sample, w_ada, b_ada, norm_w, ffn_w_gu, ffn_w_down, a_w_in, a_w_conv, a_log, a_dt_bias, a_w_onorm, a_w_out, b_w_in, b_vnorm_w, b_vnorm_b, b_w_s, b_b_s, b_w_out, final_norm_w):
    n_a = a_w_in.shape[0]
    n_b = b_w_in.shape[0]
    bp, seq, _ = x_prompt.shape
    bs = x_sample.shape[0]
    assert x_sample.shape[1] == 1 and seq % A_LT == 0 and seq % CHUNK_B == 0

    mod_p, mod_s = _ada(c_prompt, c_sample, w_ada, b_ada)
    mods_p = _Mods(mod_p.reshape(DEPTH, bp, 1, N_SUB * 3 * D_MODEL), per_row=False)
    mods_s = _Mods(mod_s.reshape(DEPTH, 1, bs, N_SUB * 3 * D_MODEL), per_row=True)

    pad = LANES - 2 * A_HEADS
    gate_pad = lambda a: jnp.pad(a, ((0, 0), (A_HEADS, pad)))[:, None, :]
    W = {
        "nw3": norm_w.reshape(DEPTH * N_SUB, 1, D_MODEL),
        "ffn_gu": ffn_w_gu,
        "ffn_dn": ffn_w_down,
        "a_w_in": a_w_in,
        "a_w_out": a_w_out,
        "b_w_in": b_w_in,
        "b_w_out": b_w_out,
        "a_ws": jnp.pad(a_w_in[:, :, A_MAIN_DIM:], ((0, 0), (0, 0), (0, pad))),
        "a_alog": gate_pad(a_log),
        "a_dtb": gate_pad(a_dt_bias),
        "a_wconv": a_w_conv,
        "a_won": a_w_onorm[:, None, :],
        "b_vw": b_vnorm_w[:, None, :],
        "b_vb": b_vnorm_b[:, None, :],
        "b_ws": b_w_s,
        "b_bsT": jnp.swapaxes(b_b_s, 1, 2),
        "b_ws00": jnp.repeat(b_w_s[:, :, 0, 0], B_GROUP_DIM, axis=1)[:, None, :],
        "b_bs0": jnp.repeat(b_b_s[:, :, 0], B_GROUP_DIM, axis=1)[:, None, :],
        "final_w": final_norm_w,
    }

    y_p, y_s, conv_p, s_p, conv_s, s_s, v_s = _trunk(
        x_prompt, mods_p, x_sample.reshape(1, bs, D_MODEL), mods_s, state_a_conv, state_a_S, W)
    return (y_p,
            y_s.reshape(bs, 1, D_MODEL),
            jnp.stack(conv_p),
            jnp.stack(s_p),
            jnp.stack(conv_s),
            s_s[-1],
            jnp.stack(v_s).reshape(n_b, bs, 1, B_WIDTH))
```
